```python
import math
import jax, jax.numpy as jnp
from jax import lax
import numpy as np

D_MODEL = 1024
BATCH = 32
SEQ = 2048
DEPTH = 2
DEC_BATCH = 16
DEC_SEQ = 32
PAST_LEN = 4096

CHUNK = 64
Q_BLOCK = 128
HEAD_DIM = 64
SB_WIDTH = D_MODEL // 2
SB_HEADS = SB_WIDTH // HEAD_DIM
SSM_WIDTH = D_MODEL // 4
SSM_GROUP = 16
SSM_GROUPS = SSM_WIDTH // SSM_GROUP
SSM_STATE = 64
RWKV_WIDTH = D_MODEL // 4
RWKV_HEADS = RWKV_WIDTH // HEAD_DIM
RWKV_LORA = 32
RWKV_SHIFT = 3 * RWKV_WIDTH + 2 * RWKV_LORA
MIX_WIDTH = SB_WIDTH + SSM_WIDTH + RWKV_WIDTH
IN_SPLITS = (SB_WIDTH, 2 * SB_WIDTH, 3 * SB_WIDTH, 4 * SB_WIDTH,
             4 * SB_WIDTH + SSM_WIDTH, 4 * SB_WIDTH + 2 * SSM_WIDTH,
             4 * SB_WIDTH + 2 * SSM_WIDTH + RWKV_SHIFT)
D_IN = IN_SPLITS[-1] + RWKV_WIDTH
RMS_EPS = 1e-6
GN_EPS = 64e-5
DECAY_SCALE = math.exp(-0.5)

kernel_name = 'hybrid_stickbreak_s5_rwkv7_stream_step'


def rms_norm(x, w):
    xf = x.astype(jnp.float32)
    return xf * lax.rsqrt(jnp.mean(xf * xf, axis=-1, keepdims=True) + RMS_EPS) * w.astype(jnp.float32)


def sb_block(qb, k, v, q_start):
    tq, tk = qb.shape[1], k.shape[1]
    z = jnp.einsum('bqhd,bkhd->bhqk', qb, k).astype(jnp.float32) * (HEAD_DIM ** -0.5)
    qpos = q_start + jnp.arange(tq)
    kpos = jnp.arange(tk)
    mask = kpos[None, :] < qpos[:, None]
    log_keep = jnp.where(mask, jax.nn.log_sigmoid(-z), 0.0)
    later = lax.cumsum(log_keep, axis=3, reverse=True) - log_keep
    a = jnp.where(mask, jnp.exp(jax.nn.log_sigmoid(z) + later), 0.0)
    return jnp.einsum('bhqk,bkhd->bqhd', a.astype(v.dtype), v)


def stick_breaking(q, k_all, v_all, q_start):
    lq = q.shape[1]
    outs = []
    for s in range(0, lq, Q_BLOCK):
        e = min(s + Q_BLOCK, lq)
        kend = q_start + e
        outs.append(sb_block(q[:, s:e], k_all[:, :kend], v_all[:, :kend], q_start + s))
    return jnp.concatenate(outs, axis=1)


def _complex_affine_combine(e1, e2):
    a1r, a1i, b1r, b1i = e1
    a2r, a2i, b2r, b2i = e2
    return (a1r * a2r - a1i * a2i, a1r * a2i + a1i * a2r,
            a2r * b1r - a2i * b1i + b2r, a2r * b1i + a2i * b1r + b2i)


def s5_branch(u, h0_re, h0_im, lam_re, lam_im, log_dt, b_re, b_im, c_re, c_im, d, w_glu, b_glu):
    f32 = jnp.float32
    bsz, l, _ = u.shape
    dt = jnp.exp(log_dt.astype(f32))[:, None]
    lr = jnp.minimum(lam_re.astype(f32), -1e-4)
    li = lam_im.astype(f32)
    er = jnp.exp(lr * dt)
    ar, ai = er * jnp.cos(li * dt), er * jnp.sin(li * dt)
    den = lr * lr + li * li
    fr = ((ar - 1.0) * lr + ai * li) / den
    fi = (ai * lr - (ar - 1.0) * li) / den
    br, bi = b_re.astype(f32), b_im.astype(f32)
    bbr = fr[..., None] * br - fi[..., None] * bi
    bbi = fr[..., None] * bi + fi[..., None] * br
    ug = u.astype(f32).reshape(bsz, l, SSM_GROUPS, SSM_GROUP)
    bu_r = jnp.einsum('blgc,gpc->blgp', ug, bbr)
    bu_i = jnp.einsum('blgc,gpc->blgp', ug, bbi)
    a_r = jnp.broadcast_to(ar, bu_r.shape)
    a_i = jnp.broadcast_to(ai, bu_r.shape)
    _, _, hr, hi = lax.associative_scan(_complex_affine_combine, (a_r, a_i, bu_r, bu_i), axis=1)
    if h0_re is not None:
        t = jnp.arange(1, l + 1, dtype=f32)[:, None, None]
        et = jnp.exp(lr * dt * t)
        p_r, p_i = et * jnp.cos(li * dt * t), et * jnp.sin(li * dt * t)
        h0r = h0_re.astype(f32)[:, None]
        h0i = h0_im.astype(f32)[:, None]
        hr = hr + p_r * h0r - p_i * h0i
        hi = hi + p_r * h0i + p_i * h0r
    y = (jnp.einsum('blgp,gcp->blgc', hr, c_re.astype(f32))
         - jnp.einsum('blgp,gcp->blgc', hi, c_im.astype(f32))
         + d.astype(f32) * ug)
    y = jax.nn.gelu(y.reshape(bsz, l, SSM_WIDTH))
    z = y @ w_glu.astype(f32) + b_glu.astype(f32)
    val, gate = jnp.split(z, 2, axis=-1)
    return val * jax.nn.sigmoid(gate), hr[:, -1], hi[:, -1]


def wkv7_scan(s0, r, decay, kt, v, kh, bh):
    def step(s, inp):
        r_t, w_t, k_t, v_t, kh_t, b_t = inp
        sa = jnp.einsum('bhvk,bhk->bhv', s, kh_t)
        s = s * w_t[:, :, None, :] - sa[..., None] * b_t[:, :, None, :] + v_t[..., None] * k_t[:, :, None, :]
        return s, jnp.einsum('bhvk,bhk->bhv', s, r_t)
    xs = tuple(jnp.moveaxis(t, 1, 0) for t in (r, decay, kt, v, kh, bh))
    s_final, ys = lax.scan(step, s0, xs)
    return jnp.moveaxis(ys, 0, 1), s_final


def rwkv7_branch(c_proj, prev_row, s0, mu, w0, w2, a0, a2, k_k, k_a, u_bonus, ln_w, ln_b):
    f32 = jnp.float32
    bsz, l, _ = c_proj.shape
    cp = c_proj.astype(f32)
    prev = jnp.zeros((bsz, 1, RWKV_SHIFT), f32) if prev_row is None else prev_row.astype(f32)[:, None]
    shifted = jnp.concatenate([prev, cp[:, :-1]], axis=1)
    xc = cp + mu.astype(f32) * (shifted - cp)
    r, k, v, w_lo, a_lo = jnp.split(
        xc, (RWKV_WIDTH, 2 * RWKV_WIDTH, 3 * RWKV_WIDTH, 3 * RWKV_WIDTH + RWKV_LORA), axis=-1)
    decay = jnp.exp(-DECAY_SCALE * jax.nn.sigmoid(w0.astype(f32) + jnp.tanh(w_lo) @ w2.astype(f32)))
    a = jax.nn.sigmoid(a0.astype(f32) + a_lo @ a2.astype(f32))
    hs = lambda t: t.reshape(bsz, l, RWKV_HEADS, HEAD_DIM)
    kk = hs(k * k_k.astype(f32))
    kh = kk * lax.rsqrt(jnp.sum(kk * kk, axis=-1, keepdims=True) + 1e-12)
    kt = hs(k * (1.0 + (a - 1.0) * k_a.astype(f32)))
    r, v, decay, a = hs(r), hs(v), hs(decay), hs(a)
    if s0 is None:
        s0 = jnp.zeros((bsz, RWKV_HEADS, HEAD_DIM, HEAD_DIM), f32)
    ys, s_final = wkv7_scan(s0.astype(f32), r, decay, kt, v, kh, a * kh)
    mean = jnp.mean(ys, axis=-1, keepdims=True)
    var = jnp.mean((ys - mean) ** 2, axis=-1, keepdims=True)
    y = ((ys - mean) * lax.rsqrt(var + GN_EPS)).reshape(bsz, l, RWKV_WIDTH) * ln_w.astype(f32) + ln_b.astype(f32)
    bonus = jnp.sum(r * kt * u_bonus.astype(f32), axis=-1, keepdims=True) * v
    y = y + bonus.reshape(bsz, l, RWKV_WIDTH)
    return y, s_final, cp[:, -1]


def mixer_layer(x, p, past):
    f32 = jnp.float32
    bsz, l, _ = x.shape
    h = rms_norm(x, p['norm_w']).astype(x.dtype)
    proj = h @ p['w_in']
    q, k, v, g_a, u, g_b, c_proj, g_c = jnp.split(proj, IN_SPLITS, axis=-1)
    heads = lambda t: t.reshape(bsz, l, SB_HEADS, HEAD_DIM)
    q = rms_norm(heads(q), p['q_norm_w']).astype(x.dtype)
    k = rms_norm(heads(k), p['k_norm_w']).astype(x.dtype)
    v = heads(v)
    if past is None:
        k_all, v_all, q_start = k, v, 0
        h0_re, h0_im, s0, prev_row = None, None, None, None
    else:
        k_all = jnp.concatenate([past['k'].astype(k.dtype), k], axis=1)
        v_all = jnp.concatenate([past['v'].astype(v.dtype), v], axis=1)
        q_start = past['k'].shape[1]
        h0_re, h0_im, s0, prev_row = past['ssm_re'], past['ssm_im'], past['wkv'], past['shift']
    y_a = stick_breaking(q, k_all, v_all, q_start).reshape(bsz, l, SB_WIDTH).astype(f32)
    y_a = y_a * jax.nn.silu(g_a.astype(f32))
    y_b, ssm_re, ssm_im = s5_branch(u, h0_re, h0_im, p['ssm_lambda_re'], p['ssm_lambda_im'], p['ssm_log_dt'],
                                    p['ssm_b_re'], p['ssm_b_im'], p['ssm_c_re'], p['ssm_c_im'], p['ssm_d'],
                                    p['ssm_w_glu'], p['ssm_b_glu'])
    y_b = y_b * jax.nn.silu(g_b.astype(f32))
    y_c, wkv, shift = rwkv7_branch(c_proj, prev_row, s0, p['rwkv_mu'], p['rwkv_w0'], p['rwkv_w2'],
                                   p['rwkv_a0'], p['rwkv_a2'], p['rwkv_k_k'], p['rwkv_k_a'], p['rwkv_u'],
                                   p['rwkv_ln_w'], p['rwkv_ln_b'])
    y_c = y_c * jax.nn.silu(g_c.astype(f32))
    mix = jnp.concatenate([y_a, y_b, y_c], axis=-1).astype(x.dtype)
    x_new = x + mix @ p['w_out']
    return x_new, (k, v, ssm_re, ssm_im, wkv, shift)


def setup_inputs(seed: int = 0) -> dict:
    key = jax.random.key(seed)
    keys = list(jax.random.split(key, 40))
    f32 = jnp.float32

    def nrm(shape, scale):
        return scale * jax.random.normal(keys.pop(), shape, f32)

    def unif(shape, lo, hi):
        return jax.random.uniform(keys.pop(), shape, f32, lo, hi)

    lam_im0 = math.pi * jnp.arange(SSM_STATE, dtype=f32)
    return {
        'x_prompt': nrm((BATCH, SEQ, D_MODEL), 1.0),
        'x_sample': nrm((DEC_BATCH, DEC_SEQ, D_MODEL), 1.0),
        'cache_k': nrm((DEPTH, DEC_BATCH, PAST_LEN, SB_HEADS, HEAD_DIM), 1.0),
        'cache_v': nrm((DEPTH, DEC_BATCH, PAST_LEN, SB_HEADS, HEAD_DIM), 1.0),
        'state_ssm_re': nrm((DEPTH, DEC_BATCH, SSM_GROUPS, SSM_STATE), 0.3),
        'state_ssm_im': nrm((DEPTH, DEC_BATCH, SSM_GROUPS, SSM_STATE), 0.3),
        'state_wkv': nrm((DEPTH, DEC_BATCH, RWKV_HEADS, HEAD_DIM, HEAD_DIM), 0.3),
        'state_shift': nrm((DEPTH, DEC_BATCH, RWKV_SHIFT), 1.0),
        'norm_w': 1.0 + nrm((DEPTH, D_MODEL), 0.01),
        'w_in': nrm((DEPTH, D_MODEL, D_IN), D_MODEL ** -0.5),
        'q_norm_w': 1.0 + nrm((DEPTH, HEAD_DIM), 0.01),
        'k_norm_w': 1.0 + nrm((DEPTH, HEAD_DIM), 0.01),
        'ssm_lambda_re': -0.5 + nrm((DEPTH, SSM_GROUPS, SSM_STATE), 0.01),
        'ssm_lambda_im': lam_im0 + nrm((DEPTH, SSM_GROUPS, SSM_STATE), 0.01),
        'ssm_log_dt': unif((DEPTH, SSM_GROUPS), math.log(1e-3), math.log(1e-1)),
        'ssm_b_re': nrm((DEPTH, SSM_GROUPS, SSM_STATE, SSM_GROUP), (2 * SSM_GROUP) ** -0.5),
        'ssm_b_im': nrm((DEPTH, SSM_GROUPS, SSM_STATE, SSM_GROUP), (2 * SSM_GROUP) ** -0.5),
        'ssm_c_re': nrm((DEPTH, SSM_GROUPS, SSM_GROUP, SSM_STATE), SSM_STATE ** -0.5),
        'ssm_c_im': nrm((DEPTH, SSM_GROUPS, SSM_GROUP, SSM_STATE), SSM_STATE ** -0.5),
        'ssm_d': nrm((DEPTH, SSM_GROUPS, SSM_GROUP), 1.0),
        'ssm_w_glu': nrm((DEPTH, SSM_WIDTH, 2 * SSM_WIDTH), SSM_WIDTH ** -0.5),
        'ssm_b_glu': nrm((DEPTH, 2 * SSM_WIDTH), 0.01),
        'rwkv_mu': unif((DEPTH, RWKV_SHIFT), 0.0, 1.0),
        'rwkv_w0': nrm((DEPTH, RWKV_WIDTH), 0.5),
        'rwkv_w2': nrm((DEPTH, RWKV_LORA, RWKV_WIDTH), RWKV_LORA ** -0.5),
        'rwkv_a0': nrm((DEPTH, RWKV_WIDTH), 0.1),
        'rwkv_a2': nrm((DEPTH, RWKV_LORA, RWKV_WIDTH), RWKV_LORA ** -0.5),
        'rwkv_k_k': 0.85 + nrm((DEPTH, RWKV_WIDTH), 0.01),
        'rwkv_k_a': 1.0 + nrm((DEPTH, RWKV_WIDTH), 0.01),
        'rwkv_u': nrm((DEPTH, RWKV_HEADS, HEAD_DIM), 0.1),
        'rwkv_ln_w': 1.0 + nrm((DEPTH, RWKV_WIDTH), 0.01),
        'rwkv_ln_b': nrm((DEPTH, RWKV_WIDTH), 0.01),
        'w_out': nrm((DEPTH, MIX_WIDTH, D_MODEL), MIX_WIDTH ** -0.5),
    }


def _stack(states, j):
    return jnp.stack([s[j] for s in states], axis=0)


def reference(x_prompt, x_sample, cache_k, cache_v, state_ssm_re, state_ssm_im, state_wkv, state_shift,
              norm_w, w_in, q_norm_w, k_norm_w, ssm_lambda_re, ssm_lambda_im, ssm_log_dt,
              ssm_b_re, ssm_b_im, ssm_c_re, ssm_c_im, ssm_d, ssm_w_glu, ssm_b_glu,
              rwkv_mu, rwkv_w0, rwkv_w2, rwkv_a0, rwkv_a2, rwkv_k_k, rwkv_k_a, rwkv_u,
              rwkv_ln_w, rwkv_ln_b, w_out):
    xp, xs = x_prompt, x_sample
    p_states, s_states = [], []
    for i in range(DEPTH):
        p = {
            'norm_w': norm_w[i], 'w_in': w_in[i], 'q_norm_w': q_norm_w[i], 'k_norm_w': k_norm_w[i],
            'ssm_lambda_re': ssm_lambda_re[i], 'ssm_lambda_im': ssm_lambda_im[i], 'ssm_log_dt': ssm_log_dt[i],
            'ssm_b_re': ssm_b_re[i], 'ssm_b_im': ssm_b_im[i], 'ssm_c_re': ssm_c_re[i], 'ssm_c_im': ssm_c_im[i],
            'ssm_d': ssm_d[i], 'ssm_w_glu': ssm_w_glu[i], 'ssm_b_glu': ssm_b_glu[i],
            'rwkv_mu': rwkv_mu[i], 'rwkv_w0': rwkv_w0[i], 'rwkv_w2': rwkv_w2[i], 'rwkv_a0': rwkv_a0[i],
            'rwkv_a2': rwkv_a2[i], 'rwkv_k_k': rwkv_k_k[i], 'rwkv_k_a': rwkv_k_a[i], 'rwkv_u': rwkv_u[i],
            'rwkv_ln_w': rwkv_ln_w[i], 'rwkv_ln_b': rwkv_ln_b[i], 'w_out': w_out[i],
        }
        xp, st_p = mixer_layer(xp, p, None)
        past = {'k': cache_k[i], 'v': cache_v[i], 'ssm_re': state_ssm_re[i], 'ssm_im': state_ssm_im[i],
                'wkv': state_wkv[i], 'shift': state_shift[i]}
        xs, st_s = mixer_layer(xs, p, past)
        p_states.append(st_p)
        s_states.append(st_s)
    new_k_prompt, new_v_prompt = _stack(p_states, 0), _stack(p_states, 1)
    ssm_re_prompt, ssm_im_prompt = _stack(p_states, 2), _stack(p_states, 3)
    wkv_prompt, shift_prompt = _stack(p_states, 4), _stack(p_states, 5)
    new_k_sample, new_v_sample = _stack(s_states, 0), _stack(s_states, 1)
    ssm_re_sample, ssm_im_sample = _stack(s_states, 2), _stack(s_states, 3)
    wkv_sample, shift_sample = _stack(s_states, 4), _stack(s_states, 5)
    return (xp, xs, new_k_prompt, new_v_prompt, ssm_re_prompt, ssm_im_prompt, wkv_prompt, shift_prompt,
            new_k_sample, new_v_sample, ssm_re_sample, ssm_im_sample, wkv_sample, shift_sample)
```

```python
import functools
import math

import jax
import jax.numpy as jnp
from jax import lax
from jax.experimental import pallas as pl
from jax.experimental.pallas import tpu as pltpu

F32 = jnp.float32
BF16 = jnp.bfloat16

D_MODEL = 1024
HEAD_DIM = 64
SB_WIDTH = D_MODEL // 2
SSM_WIDTH = D_MODEL // 4
SSM_GROUP = 16
SSM_GROUPS = SSM_WIDTH // SSM_GROUP
SSM_STATE = 64
SSM_FLAT = SSM_GROUPS * SSM_STATE
RWKV_WIDTH = D_MODEL // 4
RWKV_HEADS = RWKV_WIDTH // HEAD_DIM
RWKV_LORA = 32
RWKV_SHIFT = 3 * RWKV_WIDTH + 2 * RWKV_LORA
RMS_EPS = 1e-6
GN_EPS = 64e-5
DECAY_SCALE = math.exp(-0.5)

LANES = 128
SUBLANES = 8
CP_PAD = 7 * LANES
COL_Q, COL_K, COL_V, COL_GA = 0, 512, 1024, 1536
COL_U, COL_GB, COL_CP, COL_GC = 2048, 2304, 2560, 2560 + CP_PAD
D_IN_PAD = COL_GC + RWKV_WIDTH
VMEM_LIMIT = 56 * 1024 * 1024

NN = (((1,), (0,)), ((), ()))
NT = (((1,), (1,)), ((), ()))
TN = (((0,), (0,)), ((), ()))


def _dot(a, b, dims=NN):
    return lax.dot_general(a, b, dims, preferred_element_type=F32)


def _split(a):
    hi = a.astype(BF16)
    lo = (a - hi.astype(F32)).astype(BF16)
    return hi, lo


def _dot3(a, b, dims=NN):
    ah, al = _split(a)
    bh, bl = _split(b)
    return _dot(ah, bh, dims) + (_dot(ah, bl, dims) + _dot(al, bh, dims))


def _dot2l(a, b_exact, dims=NN):
    ah, al = _split(a)
    return _dot(ah, b_exact, dims) + _dot(al, b_exact, dims)


def _dot2r(a_exact, b, dims=NN):
    bh, bl = _split(b)
    return _dot(a_exact, bh, dims) + _dot(a_exact, bl, dims)


def _params(sem):
    return pltpu.CompilerParams(dimension_semantics=sem, vmem_limit_bytes=VMEM_LIMIT)


def _full(shape):
    nd = len(shape)
    return pl.BlockSpec(shape, lambda *_: (0,) * nd)


def _head_rms(t, w):
    lane = lax.broadcasted_iota(jnp.int32, (1, LANES), 1)
    lo = lane < HEAD_DIM
    outs = []
    for j in range(t.shape[1] // LANES):
        blk = t[:, j * LANES:(j + 1) * LANES]
        sq = blk * blk
        s_lo = jnp.sum(jnp.where(lo, sq, 0.0), axis=-1, keepdims=True)
        s_hi = jnp.sum(jnp.where(lo, 0.0, sq), axis=-1, keepdims=True)
        ms = jnp.where(lo, s_lo, s_hi) * (1.0 / HEAD_DIM)
        outs.append(blk * lax.rsqrt(ms + RMS_EPS) * w[:, j * LANES:(j + 1) * LANES])
    return jnp.concatenate(outs, axis=-1)


def _inproj_kernel(x_ref, nw_ref, w_ref, qw_ref, kw_ref,
                   q_o, k_o, v_o, ga_o, u_o, gb_o, cp_o, gc_o):
    x = x_ref[...]
    ms = jnp.mean(x * x, axis=-1, keepdims=True)
    h = (x * lax.rsqrt(ms + RMS_EPS) * nw_ref[...]).astype(BF16)

    def proj(a, b):
        return _dot(h, w_ref[:, a:b])

    q = _head_rms(proj(COL_Q, COL_K), qw_ref[...])
    q_o[...] = (q * (HEAD_DIM ** -0.5)).astype(BF16)
    k_o[...] = _head_rms(proj(COL_K, COL_V), kw_ref[...])
    v_o[...] = proj(COL_V, COL_GA)
    ga_o[...] = proj(COL_GA, COL_U)
    u_o[...] = proj(COL_U, COL_GB)
    gb_o[...] = proj(COL_GB, COL_CP)
    cp_o[...] = proj(COL_CP, COL_GC)
    gc_o[...] = proj(COL_GC, D_IN_PAD)


def _inproj(x2d, nw, w_pad, qw, kw):
    n = x2d.shape[0]
    tm = min(256, n)
    widths = (SB_WIDTH, SB_WIDTH, SB_WIDTH, SB_WIDTH, SSM_WIDTH, SSM_WIDTH, CP_PAD, RWKV_WIDTH)
    dtypes = (BF16,) + (F32,) * 7
    row = lambda w: pl.BlockSpec((tm, w), lambda i: (i, 0))
    return pl.pallas_call(
        _inproj_kernel,
        grid=(n // tm,),
        in_specs=[row(D_MODEL), _full((1, D_MODEL)), _full((D_MODEL, D_IN_PAD)),
                  _full((1, SB_WIDTH)), _full((1, SB_WIDTH))],
        out_specs=[row(w) for w in widths],
        out_shape=[jax.ShapeDtypeStruct((n, w), dt) for w, dt in zip(widths, dtypes)],
        compiler_params=_params(("parallel",)),
        name="inproj",
    )(x2d, nw, w_pad, qw, kw)


def _sb_block(qm, kb, vb, carry, tri, mask):
    z = _dot(qm, kb, NT)
    lk = -(jnp.maximum(z, 0.0) + jnp.log1p(jnp.exp(-jnp.abs(z))))
    if mask is not None:
        lk = jnp.where(mask, lk, 0.0)
    incl = _dot2l(lk, tri)
    a = jnp.exp(z + incl + carry)
    if mask is not None:
        a = jnp.where(mask, a, 0.0)
    pv = _dot(a.astype(BF16), vb)
    return pv, carry + incl[:, 0:1]


def _sb_heads(q, diag_k, diag_v, past_k_ref, past_v_ref, n_past, tkp):
    tq = q.shape[0]
    lane = lax.broadcasted_iota(jnp.int32, (1, LANES), 1)
    head_lo = lane < HEAD_DIM
    zero = jnp.zeros_like(q)
    qms = (jnp.where(head_lo, q, zero), jnp.where(head_lo, zero, q))

    row = lax.broadcasted_iota(jnp.int32, (tq, tq), 0)
    col = lax.broadcasted_iota(jnp.int32, (tq, tq), 1)
    causal = col < row
    tri_d = (row >= col).astype(BF16)
    dk, dv = diag_k.astype(BF16), diag_v.astype(BF16)
    state = []
    for qm in qms:
        pv, carry = _sb_block(qm, dk, dv, jnp.zeros((tq, 1), F32), tri_d, causal)
        state += [pv, carry]

    rp = lax.broadcasted_iota(jnp.int32, (tkp, tkp), 0)
    cpi = lax.broadcasted_iota(jnp.int32, (tkp, tkp), 1)
    tri_p = (rp >= cpi).astype(BF16)

    def body(jj, st):
        j = n_past - 1 - jj
        start = pl.multiple_of(j * tkp, tkp)
        kb = past_k_ref[pl.ds(start, tkp), :].astype(BF16)
        vb = past_v_ref[pl.ds(start, tkp), :].astype(BF16)
        new = []
        for h, qm in enumerate(qms):
            pv, carry = _sb_block(qm, kb, vb, st[2 * h + 1], tri_p, None)
            new += [st[2 * h] + pv, carry]
        return tuple(new)

    st = lax.fori_loop(0, n_past, body, tuple(state))
    return jnp.where(head_lo, st[0], st[2])


def _sb_prompt_kernel(q_ref, k_ref, v_ref, g_ref, o_ref, *, tq):
    qi = pl.program_id(2)
    start = pl.multiple_of(qi * tq, tq)
    k2, v2 = k_ref.at[0], v_ref.at[0]
    y = _sb_heads(q_ref[0], k2[pl.ds(start, tq), :], v2[pl.ds(start, tq), :], k2, v2, qi, tq)
    o_ref[0] = (y * jax.nn.silu(g_ref[0])).astype(BF16)


def _sb_prompt(q, k, v, g):
    b, l, _ = q.shape
    tq = min(128, l)
    npair = SB_WIDTH // LANES
    qblk = pl.BlockSpec((1, tq, LANES), lambda bi, hp, qi: (bi, qi, hp))
    kvblk = pl.BlockSpec((1, l, LANES), lambda bi, hp, qi: (bi, 0, hp))
    return pl.pallas_call(
        functools.partial(_sb_prompt_kernel, tq=tq),
        grid=(b, npair, l // tq),
        in_specs=[qblk, kvblk, kvblk, qblk],
        out_specs=qblk,
        out_shape=jax.ShapeDtypeStruct((b, l, SB_WIDTH), BF16),
        compiler_params=_params(("parallel", "parallel", "arbitrary")),
        name="sb_prompt",
    )(q, k, v, g)


def _sb_sample_kernel(q_ref, k_ref, v_ref, ck_ref, cv_ref, g_ref, o_ref, *, n_past, tkp):
    y = _sb_heads(q_ref[0], k_ref[0], v_ref[0], ck_ref.at[0], cv_ref.at[0], n_past, tkp)
    o_ref[0] = (y * jax.nn.silu(g_ref[0])).astype(BF16)


def _sb_sample(q, k, v, ck, cv, g):
    b, l, _ = q.shape
    past = ck.shape[1]
    tkp = 128
    npair = SB_WIDTH // LANES
    qblk = pl.BlockSpec((1, l, LANES), lambda bi, hp: (bi, 0, hp))
    cblk = pl.BlockSpec((1, past, LANES), lambda bi, hp: (bi, 0, hp))
    return pl.pallas_call(
        functools.partial(_sb_sample_kernel, n_past=past // tkp, tkp=tkp),
        grid=(b, npair),
        in_specs=[qblk, qblk, qblk, cblk, cblk, qblk],
        out_specs=qblk,
        out_shape=jax.ShapeDtypeStruct((b, l, SB_WIDTH), BF16),
        compiler_params=_params(("parallel", "parallel")),
        name="sb_sample",
    )(q, k, v, ck, cv, g)


def _s5_kernel(u_ref, g_ref, h0r_ref, h0i_ref, ar_ref, ai_ref, bh_ref, bl_ref, c_ref, d_ref,
               wg_ref, bg_ref, o_ref, hr_o, hi_o, bu_s, h_s, *, tt):
    ti = pl.program_id(1)
    rows = tt * SUBLANES

    @pl.when(ti == 0)
    def _():
        hr_o[...] = h0r_ref[...]
        hi_o[...] = h0i_ref[...]

    u = u_ref[...].reshape(rows, SSM_WIDTH)
    uh, ul = _split(u)
    bu_s[...] = _dot(uh, bh_ref[...]) + (_dot(uh, bl_ref[...]) + _dot(ul, bh_ref[...]))

    ar = jnp.broadcast_to(ar_ref[...], (SUBLANES, SSM_FLAT))
    ai = jnp.broadcast_to(ai_ref[...], (SUBLANES, SSM_FLAT))

    def step(t, carry):
        hr, hi = carry
        r0 = pl.multiple_of(t * SUBLANES, SUBLANES)
        br = bu_s[pl.ds(r0, SUBLANES), 0:SSM_FLAT]
        bi = bu_s[pl.ds(r0, SUBLANES), SSM_FLAT:2 * SSM_FLAT]
        nr = ar * hr - ai * hi + br
        ni = ar * hi + ai * hr + bi
        h_s[pl.ds(r0, SUBLANES), 0:SSM_FLAT] = nr
        h_s[pl.ds(r0, SUBLANES), SSM_FLAT:2 * SSM_FLAT] = ni
        return nr, ni

    hr, hi = lax.fori_loop(0, tt, step, (hr_o[...], hi_o[...]))
    hr_o[...] = hr
    hi_o[...] = hi

    y = _dot(h_s[...].astype(BF16), c_ref[...]) + d_ref[...] * u
    y = jax.nn.gelu(y)
    z = _dot(y.astype(BF16), wg_ref[...]) + bg_ref[...]
    out = z[:, :SSM_WIDTH] * jax.nn.sigmoid(z[:, SSM_WIDTH:])
    out = out * jax.nn.silu(g_ref[...].reshape(rows, SSM_WIDTH))
    o_ref[...] = out.reshape(tt, SUBLANES, SSM_WIDTH)


def _s5(u_t, g_t, h0r, h0i, sp):
    l, b, _ = u_t.shape
    tt = min(64, l)
    rows = tt * SUBLANES
    ublk = pl.BlockSpec((tt, SUBLANES, SSM_WIDTH), lambda bi, ti: (ti, bi, 0))
    sblk = pl.BlockSpec((SUBLANES, SSM_FLAT), lambda bi, ti: (bi, 0))
    return pl.pallas_call(
        functools.partial(_s5_kernel, tt=tt),
        grid=(b // SUBLANES, l // tt),
        in_specs=[ublk, ublk, sblk, sblk,
                  _full((1, SSM_FLAT)), _full((1, SSM_FLAT)),
                  _full((SSM_WIDTH, 2 * SSM_FLAT)), _full((SSM_WIDTH, 2 * SSM_FLAT)),
                  _full((2 * SSM_FLAT, SSM_WIDTH)), _full((1, SSM_WIDTH)),
                  _full((SSM_WIDTH, 2 * SSM_WIDTH)), _full((1, 2 * SSM_WIDTH))],
        out_specs=[ublk, sblk, sblk],
        out_shape=[jax.ShapeDtypeStruct((l, b, SSM_WIDTH), F32),
                   jax.ShapeDtypeStruct((b, SSM_FLAT), F32),
                   jax.ShapeDtypeStruct((b, SSM_FLAT), F32)],
        scratch_shapes=[pltpu.VMEM((rows, 2 * SSM_FLAT), F32), pltpu.VMEM((rows, 2 * SSM_FLAT), F32)],
        compiler_params=_params(("parallel", "arbitrary")),
        name="s5",
    )(u_t, g_t, h0r, h0i, sp["ar"], sp["ai"], sp["b_hi"], sp["b_lo"], sp["c"], sp["d"], sp["w_glu"], sp["b_glu"])


def _s5_params(lam_re, lam_im, log_dt, b_re, b_im, c_re, c_im, d, w_glu, b_glu):
    dt = jnp.exp(log_dt.astype(F32))[:, None]
    lr = jnp.minimum(lam_re.astype(F32), -1e-4)
    li = lam_im.astype(F32)
    er = jnp.exp(lr * dt)
    ar, ai = er * jnp.cos(li * dt), er * jnp.sin(li * dt)
    den = lr * lr + li * li
    fr = ((ar - 1.0) * lr + ai * li) / den
    fi = (ai * lr - (ar - 1.0) * li) / den
    br, bi = b_re.astype(F32), b_im.astype(F32)
    bbr = fr[..., None] * br - fi[..., None] * bi
    bbi = fr[..., None] * bi + fi[..., None] * br
    eye = jnp.eye(SSM_GROUPS, dtype=F32)
    bd_in = lambda m: jnp.einsum("gpc,gh->gchp", m, eye).reshape(SSM_WIDTH, SSM_FLAT)
    bd_out = lambda m: jnp.einsum("gcp,gh->gphc", m, eye).reshape(SSM_FLAT, SSM_WIDTH)
    b_all = jnp.concatenate([bd_in(bbr), bd_in(bbi)], axis=1)
    b_hi = b_all.astype(BF16)
    b_lo = (b_all - b_hi.astype(F32)).astype(BF16)
    c_all = jnp.concatenate([bd_out(c_re.astype(F32)), -bd_out(c_im.astype(F32))], axis=0)
    return dict(ar=ar.reshape(1, SSM_FLAT), ai=ai.reshape(1, SSM_FLAT), b_hi=b_hi, b_lo=b_lo,
                c=c_all.astype(BF16), d=d.astype(F32).reshape(1, SSM_WIDTH),
                w_glu=w_glu.astype(BF16), b_glu=b_glu.astype(F32).reshape(1, 2 * SSM_WIDTH))


def _rwkv_kernel(cp_ref, g_ref, prev_ref, s0_ref, mu_ref, w0_ref, w2_ref, a0_ref, a2_ref, kk_ref, ka_ref,
                 ub_ref, lnw_ref, lnb_ref, o_ref, s_o, prev_s, y_s, *, tt, ck):
    ti = pl.program_id(1)

    @pl.when(ti == 0)
    def _():
        s_o[...] = s0_ref[...]
        prev_s[...] = jnp.broadcast_to(prev_ref[0], (SUBLANES, CP_PAD))

    cp = cp_ref[0]
    trow = lax.broadcasted_iota(jnp.int32, (tt, 1), 0)
    shifted = jnp.where(trow == 0, prev_s[0:1, :], pltpu.roll(cp, 1, 0))
    prev_s[...] = jnp.broadcast_to(cp[tt - 1:tt, :], (SUBLANES, CP_PAD))
    xc = cp + mu_ref[...] * (shifted - cp)
    w = RWKV_WIDTH
    r, k, v, lora = xc[:, 0:w], xc[:, w:2 * w], xc[:, 2 * w:3 * w], xc[:, 3 * w:]

    logw = -DECAY_SCALE * jax.nn.sigmoid(w0_ref[...] + _dot(jnp.tanh(lora).astype(BF16), w2_ref[...]))
    a = jax.nn.sigmoid(a0_ref[...] + _dot(lora.astype(BF16), a2_ref[...]))

    hl = lax.broadcasted_iota(jnp.int32, (w, w), 0) // HEAD_DIM
    hc = lax.broadcasted_iota(jnp.int32, (w, w), 1) // HEAD_DIM
    head_ones = (hl == hc).astype(BF16)
    tr = lax.broadcasted_iota(jnp.int32, (tt, tt), 0)
    tc = lax.broadcasted_iota(jnp.int32, (tt, tt), 1)
    same_chunk = (tr // ck) == (tc // ck)
    chunk_tri = (same_chunk & (tc <= tr)).astype(BF16)
    chunk_ones = same_chunk.astype(BF16)

    cum = _dot2r(chunk_tri, logw)
    tot = _dot2r(chunk_ones, logw)
    kkv = k * kk_ref[...]
    kh = kkv * lax.rsqrt(_dot2l(kkv * kkv, head_ones) + 1e-12)
    kt = k * (1.0 + (a - 1.0) * ka_ref[...])
    bvec = a * kh
    w_inv = jnp.exp(-cum)
    khw = kh * jnp.exp(cum - logw)
    k_til = kt * w_inv
    b_til = bvec * w_inv
    rw = r * jnp.exp(cum)
    w_end = jnp.exp(tot - cum)
    k_hat = kt * w_end
    b_hat = bvec * w_end
    w_chunk = jnp.exp(tot)

    lane = lax.broadcasted_iota(jnp.int32, (1, LANES), 1)
    m_lo = lane < HEAD_DIM
    ci = lax.broadcasted_iota(jnp.int32, (ck, ck), 0)
    cj = lax.broadcasted_iota(jnp.int32, (ck, ck), 1)
    strict = cj < ci
    lower = cj <= ci
    eye = (ci == cj).astype(F32)
    sr = lax.broadcasted_iota(jnp.int32, (LANES, LANES), 0) // HEAD_DIM
    sc = lax.broadcasted_iota(jnp.int32, (LANES, LANES), 1) // HEAD_DIM
    blockdiag = sr == sc

    for hp in range(w // LANES):
        ls = slice(hp * LANES, (hp + 1) * LANES)
        s_pair = s_o[0, hp]
        for c in range(tt // ck):
            rs = slice(c * ck, (c + 1) * ck)
            khw_c, ktl_c, btl_c, rw_c = khw[rs, ls], k_til[rs, ls], b_til[rs, ls], rw[rs, ls]
            v_c, kht_c, bht_c = v[rs, ls], k_hat[rs, ls], b_hat[rs, ls]
            zero = jnp.zeros_like(khw_c)
            rhs = _dot3(khw_c, s_pair, NT)
            y_c = _dot3(rw_c, s_pair, NT)
            sa = jnp.zeros_like(rhs)
            parts = []
            for hh in range(2):
                mh = m_lo if hh == 0 else jnp.logical_not(m_lo)
                khw_h = jnp.where(mh, khw_c, zero)
                rw_h = jnp.where(mh, rw_c, zero)
                akk = jnp.where(strict, _dot3(khw_h, ktl_c, NT), 0.0)
                nmat = jnp.where(strict, _dot3(khw_h, btl_c, NT), 0.0)
                ark = jnp.where(lower, _dot3(rw_h, ktl_c, NT), 0.0)
                arb = jnp.where(lower, _dot3(rw_h, btl_c, NT), 0.0)
                inv = eye - nmat
                pw = nmat
                steps = 1
                while 2 * steps < ck:
                    pw = _dot3(pw, pw)
                    inv = _dot3(inv, eye + pw)
                    steps *= 2
                parts.append((mh, akk, ark, arb, inv))
            rhs_full = rhs
            for mh, akk, _, _, _ in parts:
                rhs_full = rhs_full + jnp.where(mh, _dot3(akk, v_c), 0.0)
            for mh, _, _, _, inv in parts:
                sa = sa + jnp.where(mh, _dot3(inv, rhs_full), 0.0)
            for mh, _, ark, arb, _ in parts:
                y_c = y_c + jnp.where(mh, _dot3(ark, v_c) - _dot3(arb, sa), 0.0)
            y_s[rs, ls] = y_c
            upd = _dot3(v_c, kht_c, TN) - _dot3(sa, bht_c, TN)
            s_pair = s_pair * w_chunk[c * ck:c * ck + 1, ls] + jnp.where(blockdiag, upd, 0.0)
        s_o[0, hp] = s_pair

    ys = y_s[...]
    mean = _dot2l(ys, head_ones) * (1.0 / HEAD_DIM)
    cen = ys - mean
    var = _dot2l(cen * cen, head_ones) * (1.0 / HEAD_DIM)
    y = cen * lax.rsqrt(var + GN_EPS) * lnw_ref[...] + lnb_ref[...]
    y = y + _dot2l(r * kt * ub_ref[...], head_ones) * v
    o_ref[0] = (y * jax.nn.silu(g_ref[0])).astype(BF16)


def _rwkv(cp, g, prev, s0p, rp):
    b, l, _ = cp.shape
    tt = min(256, l)
    ck = min(32, tt)
    tblk = lambda w: pl.BlockSpec((1, tt, w), lambda bi, ti: (bi, ti, 0))
    sblk = pl.BlockSpec((1, 2, LANES, LANES), lambda bi, ti: (bi, 0, 0, 0))
    vec = _full((1, RWKV_WIDTH))
    return pl.pallas_call(
        functools.partial(_rwkv_kernel, tt=tt, ck=ck),
        grid=(b, l // tt),
        in_specs=[tblk(CP_PAD), tblk(RWKV_WIDTH),
                  pl.BlockSpec((1, 1, CP_PAD), lambda bi, ti: (bi, 0, 0)), sblk,
                  _full((1, CP_PAD)), vec, _full((LANES, RWKV_WIDTH)), vec, _full((LANES, RWKV_WIDTH)),
                  vec, vec, vec, vec, vec],
        out_specs=[tblk(RWKV_WIDTH), sblk],
        out_shape=[jax.ShapeDtypeStruct((b, l, RWKV_WIDTH), BF16),
                   jax.ShapeDtypeStruct((b, 2, LANES, LANES), F32)],
        scratch_shapes=[pltpu.VMEM((SUBLANES, CP_PAD), F32), pltpu.VMEM((tt, RWKV_WIDTH), F32)],
        compiler_params=_params(("parallel", "arbitrary")),
        name="rwkv",
    )(cp, g, prev, s0p, rp["mu"], rp["w0"], rp["w2"], rp["a0"], rp["a2"], rp["k_k"], rp["k_a"],
      rp["u"], rp["ln_w"], rp["ln_b"])


def _rwkv_params(mu, w0, w2, a0, a2, k_k, k_a, u, ln_w, ln_b):
    vec = lambda t: t.astype(F32).reshape(1, RWKV_WIDTH)
    mu_p = jnp.zeros((1, CP_PAD), F32).at[0, :RWKV_SHIFT].set(mu.astype(F32))
    w2_p = jnp.zeros((LANES, RWKV_WIDTH), F32).at[:RWKV_LORA].set(w2.astype(F32)).astype(BF16)
    a2_p = jnp.zeros((LANES, RWKV_WIDTH), F32).at[RWKV_LORA:2 * RWKV_LORA].set(a2.astype(F32)).astype(BF16)
    return dict(mu=mu_p, w0=vec(w0), w2=w2_p, a0=vec(a0), a2=a2_p, k_k=vec(k_k), k_a=vec(k_a),
                u=vec(u), ln_w=vec(ln_w), ln_b=vec(ln_b))


def _pair_state(s):
    b = s.shape[0]
    s = s.astype(F32).reshape(b, 2, 2, HEAD_DIM, HEAD_DIM)
    eye = jnp.eye(2, dtype=F32)
    return jnp.einsum("bphvk,hg->bphvgk", s, eye).reshape(b, 2, LANES, LANES)


def _unpair_state(sp):
    b = sp.shape[0]
    s = sp.reshape(b, 2, 2, HEAD_DIM, 2, HEAD_DIM)
    return jnp.stack([s[:, :, 0, :, 0, :], s[:, :, 1, :, 1, :]], axis=2).reshape(b, RWKV_HEADS, HEAD_DIM, HEAD_DIM)


def _outproj_kernel(x_ref, a_ref, b_ref, c_ref, w_ref, o_ref):
    acc = _dot(a_ref[...], w_ref[0:SB_WIDTH, :])
    acc = acc + _dot(b_ref[...], w_ref[SB_WIDTH:SB_WIDTH + SSM_WIDTH, :])
    acc = acc + _dot(c_ref[...], w_ref[SB_WIDTH + SSM_WIDTH:, :])
    o_ref[...] = x_ref[...] + acc


def _outproj(x2d, ma, mb, mc, w_out):
    n = x2d.shape[0]
    tm = min(512, n)
    row = lambda w: pl.BlockSpec((tm, w), lambda i: (i, 0))
    return pl.pallas_call(
        _outproj_kernel,
        grid=(n // tm,),
        in_specs=[row(D_MODEL), row(SB_WIDTH), row(SSM_WIDTH), row(RWKV_WIDTH), _full((D_MODEL, D_MODEL))],
        out_specs=row(D_MODEL),
        out_shape=jax.ShapeDtypeStruct((n, D_MODEL), F32),
        compiler_params=_params(("parallel",)),
        name="outproj",
    )(x2d, ma, mb, mc, w_out)


def _pad_w_in(w_in):
    w = w_in.astype(F32)
    pad = jnp.zeros((D_MODEL, CP_PAD - RWKV_SHIFT), F32)
    return jnp.concatenate([w[:, :COL_CP + RWKV_SHIFT], pad, w[:, COL_CP + RWKV_SHIFT:]], axis=1).astype(BF16)


def _layer(x, lp, past):
    b, l, _ = x.shape
    n = b * l
    x2d = x.reshape(n, D_MODEL)
    q, k, v, ga, u, gb, cp, gc = _inproj(x2d, lp["norm_w"], lp["w_in"], lp["q_w"], lp["k_w"])
    r3 = lambda t: t.reshape(b, l, t.shape[-1])
    q, k, v, ga, u, gb, cp, gc = map(r3, (q, k, v, ga, u, gb, cp, gc))

    if past is None:
        mix_a = _sb_prompt(q, k, v, ga)
        h0r = h0i = jnp.zeros((b, SSM_FLAT), F32)
        prev = jnp.zeros((b, 1, CP_PAD), F32)
        s0p = jnp.zeros((b, 2, LANES, LANES), F32)
    else:
        ck = past["k"].reshape(b, -1, SB_WIDTH)
        cv = past["v"].reshape(b, -1, SB_WIDTH)
        mix_a = _sb_sample(q, k, v, ck, cv, ga)
        h0r = past["ssm_re"].astype(F32).reshape(b, SSM_FLAT)
        h0i = past["ssm_im"].astype(F32).reshape(b, SSM_FLAT)
        prev = jnp.pad(past["shift"].astype(F32), ((0, 0), (0, CP_PAD - RWKV_SHIFT)))[:, None, :]
        s0p = _pair_state(past["wkv"])

    tm = lambda t: jnp.swapaxes(t, 0, 1)
    yb_t, hr, hi = _s5(tm(u), tm(gb), h0r, h0i, lp["s5"])
    mix_b = tm(yb_t).astype(BF16)
    mix_c, s_pair = _rwkv(cp, gc, prev, s0p, lp["rwkv"])

    x_new = _outproj(x2d, mix_a.reshape(n, SB_WIDTH), mix_b.reshape(n, SSM_WIDTH),
                     mix_c.reshape(n, RWKV_WIDTH), lp["w_out"]).reshape(b, l, D_MODEL)
    heads = lambda t: t.reshape(b, l, SB_WIDTH // HEAD_DIM, HEAD_DIM)
    state = (heads(k), heads(v), hr.reshape(b, SSM_GROUPS, SSM_STATE), hi.reshape(b, SSM_GROUPS, SSM_STATE),
             _unpair_state(s_pair), cp[:, -1, :RWKV_SHIFT])
    return x_new, state


def kernel(x_prompt, x_sample, cache_k, cache_v, state_ssm_re, state_ssm_im, state_wkv, state_shift, norm_w, w_in, q_norm_w, k_norm_w, ssm_lambda_re, ssm_lambda_im, ssm_log_dt, ssm_b_re, ssm_b_im, ssm_c_re, ssm_c_im, ssm_d, ssm_w_glu, ssm_b_glu, rwkv_mu, rwkv_w0, rwkv_w2, rwkv_a0, rwkv_a2, rwkv_k_k, rwkv_k_a, rwkv_u, rwkv_ln_w, rwkv_ln_b, w_out):
    depth = w_in.shape[0]
    xp, xs = x_prompt, x_sample
    p_states, s_states = [], []
    for i in range(depth):
        tile8 = lambda t: jnp.tile(t.astype(F32), SB_WIDTH // HEAD_DIM).reshape(1, SB_WIDTH)
        lp = dict(
            norm_w=norm_w[i].astype(F32).reshape(1, D_MODEL), w_in=_pad_w_in(w_in[i]),
            q_w=tile8(q_norm_w[i]), k_w=tile8(k_norm_w[i]),
            s5=_s5_params(ssm_lambda_re[i], ssm_lambda_im[i], ssm_log_dt[i], ssm_b_re[i], ssm_b_im[i],
                          ssm_c_re[i], ssm_c_im[i], ssm_d[i], ssm_w_glu[i], ssm_b_glu[i]),
            rwkv=_rwkv_params(rwkv_mu[i], rwkv_w0[i], rwkv_w2[i], rwkv_a0[i], rwkv_a2[i], rwkv_k_k[i],
                              rwkv_k_a[i], rwkv_u[i].reshape(-1), rwkv_ln_w[i], rwkv_ln_b[i]),
            w_out=w_out[i].astype(BF16))
        xp, st_p = _layer(xp, lp, None)
        past = dict(k=cache_k[i], v=cache_v[i], ssm_re=state_ssm_re[i], ssm_im=state_ssm_im[i],
                    wkv=state_wkv[i], shift=state_shift[i])
        xs, st_s = _layer(xs, lp, past)
        p_states.append(st_p)
        s_states.append(st_s)
    stack = lambda states, j: jnp.stack([s[j] for s in states], axis=0)
    return ((xp, xs) + tuple(stack(p_states, j) for j in range(6))
            + tuple(stack(s_states, j) for j in range(6)))
```

```python
import functools
import math

import jax
import jax.numpy as jnp
from jax import lax
from jax.experimental import pallas as pl
from jax.experimental.pallas import tpu as pltpu

F32 = jnp.float32
BF16 = jnp.bfloat16

D_MODEL = 1024
HEAD_DIM = 64
SB_WIDTH = D_MODEL // 2
SSM_WIDTH = D_MODEL // 4
SSM_GROUP = 16
SSM_GROUPS = SSM_WIDTH // SSM_GROUP
SSM_STATE = 64
SSM_FLAT = SSM_GROUPS * SSM_STATE
RWKV_WIDTH = D_MODEL // 4
RWKV_HEADS = RWKV_WIDTH // HEAD_DIM
RWKV_LORA = 32
RWKV_SHIFT = 3 * RWKV_WIDTH + 2 * RWKV_LORA
RMS_EPS = 1e-6
GN_EPS = 64e-5
DECAY_SCALE = math.exp(-0.5)

LANES = 128
SUBLANES = 8
CP_PAD = 7 * LANES
COL_Q, COL_K, COL_V, COL_GA = 0, 512, 1024, 1536
COL_U, COL_GB, COL_CP, COL_GC = 2048, 2304, 2560, 2560 + CP_PAD
D_IN_PAD = COL_GC + RWKV_WIDTH
VMEM_LIMIT = 56 * 1024 * 1024

NN = (((1,), (0,)), ((), ()))
NT = (((1,), (1,)), ((), ()))
TN = (((0,), (0,)), ((), ()))


def _dot(a, b, dims=NN):
    return lax.dot_general(a, b, dims, preferred_element_type=F32)


def _split(a):
    hi = a.astype(BF16)
    lo = (a - hi.astype(F32)).astype(BF16)
    return hi, lo


def _dot3(a, b, dims=NN):
    ah, al = _split(a)
    bh, bl = _split(b)
    return _dot(ah, bh, dims) + (_dot(ah, bl, dims) + _dot(al, bh, dims))


def _dot2l(a, b_exact, dims=NN):
    ah, al = _split(a)
    return _dot(ah, b_exact, dims) + _dot(al, b_exact, dims)


def _dot2r(a_exact, b, dims=NN):
    bh, bl = _split(b)
    return _dot(a_exact, bh, dims) + _dot(a_exact, bl, dims)


def _params(sem):
    return pltpu.CompilerParams(dimension_semantics=sem, vmem_limit_bytes=VMEM_LIMIT)


def _full(shape):
    nd = len(shape)
    return pl.BlockSpec(shape, lambda *_: (0,) * nd)


def _head_rms(t, w):
    lane = lax.broadcasted_iota(jnp.int32, (1, LANES), 1)
    lo = lane < HEAD_DIM
    outs = []
    for j in range(t.shape[1] // LANES):
        blk = t[:, j * LANES:(j + 1) * LANES]
        sq = blk * blk
        s_lo = jnp.sum(jnp.where(lo, sq, 0.0), axis=-1, keepdims=True)
        s_hi = jnp.sum(jnp.where(lo, 0.0, sq), axis=-1, keepdims=True)
        ms = jnp.where(lo, s_lo, s_hi) * (1.0 / HEAD_DIM)
        outs.append(blk * lax.rsqrt(ms + RMS_EPS) * w[:, j * LANES:(j + 1) * LANES])
    return jnp.concatenate(outs, axis=-1)


def _inproj_kernel(x_ref, nw_ref, w_ref, qw_ref, kw_ref,
                   q_o, k_o, v_o, ga_o, u_o, gb_o, cp_o, gc_o):
    x = x_ref[...]
    ms = jnp.mean(x * x, axis=-1, keepdims=True)
    h = (x * lax.rsqrt(ms + RMS_EPS) * nw_ref[...]).astype(BF16)

    def proj(a, b):
        return _dot(h, w_ref[:, a:b])

    q = _head_rms(proj(COL_Q, COL_K), qw_ref[...])
    q_o[...] = (q * (HEAD_DIM ** -0.5)).astype(BF16)
    k_o[...] = _head_rms(proj(COL_K, COL_V), kw_ref[...])
    v_o[...] = proj(COL_V, COL_GA)
    ga_o[...] = proj(COL_GA, COL_U)
    u_o[...] = proj(COL_U, COL_GB)
    gb_o[...] = proj(COL_GB, COL_CP)
    cp_o[...] = proj(COL_CP, COL_GC)
    gc_o[...] = proj(COL_GC, D_IN_PAD)


def _inproj(x2d, nw, w_pad, qw, kw):
    n = x2d.shape[0]
    tm = min(256, n)
    widths = (SB_WIDTH, SB_WIDTH, SB_WIDTH, SB_WIDTH, SSM_WIDTH, SSM_WIDTH, CP_PAD, RWKV_WIDTH)
    dtypes = (BF16,) + (F32,) * 7
    row = lambda w: pl.BlockSpec((tm, w), lambda i: (i, 0))
    return pl.pallas_call(
        _inproj_kernel,
        grid=(n // tm,),
        in_specs=[row(D_MODEL), _full((1, D_MODEL)), _full((D_MODEL, D_IN_PAD)),
                  _full((1, SB_WIDTH)), _full((1, SB_WIDTH))],
        out_specs=[row(w) for w in widths],
        out_shape=[jax.ShapeDtypeStruct((n, w), dt) for w, dt in zip(widths, dtypes)],
        compiler_params=_params(("parallel",)),
        name="inproj",
    )(x2d, nw, w_pad, qw, kw)


SB_TK = 128
SB_PAIRS = SB_WIDTH // LANES


SB_U = 2
LOG2E = math.log2(math.e)


def _sb_weights(qs, kb, carry, tri2, mask):
    tk = kb.shape[0]
    z2 = _dot(qs, kb, NT) * LOG2E
    sp = jnp.maximum(z2, 0.0) + jnp.log2(1.0 + jnp.exp2(-jnp.abs(z2)))
    if mask is not None:
        sp = jnp.where(mask, sp, 0.0)
    hi, lo = _split(sp)
    cs = _dot(jnp.concatenate([hi, lo], axis=1), tri2)
    a = jnp.exp2(z2 - cs[:, :tk] - carry)
    if mask is not None:
        a = jnp.where(mask, a, 0.0)
    return a.astype(BF16), carry + cs[:, tk:]


def _sb_values(a2, vbs, head_lo):
    parts = []
    for vb in vbs:
        zero = jnp.zeros_like(vb)
        parts += [jnp.where(head_lo, vb, zero), jnp.where(head_lo, zero, vb)]
    return _dot(a2, jnp.concatenate(parts, axis=0))


def _sb_all_pairs(q, diag_kv, n_diag, past_k, past_v, n_iter, tri2, qs_s, a_s, acc_s, carry_s):
    tq, tk = q.shape[0], SB_TK
    head_lo = lax.broadcasted_iota(jnp.int32, (1, LANES), 1) < HEAD_DIM
    row = lax.broadcasted_iota(jnp.int32, (2 * tq, tk), 0)
    col = lax.broadcasted_iota(jnp.int32, (2 * tq, tk), 1)
    qrow = jnp.where(row >= tq, row - tq, row)
    for p in range(SB_PAIRS):
        qp = q[:, p * LANES:(p + 1) * LANES]
        zero = jnp.zeros_like(qp)
        qs = jnp.concatenate([jnp.where(head_lo, qp, zero), jnp.where(head_lo, zero, qp)], axis=0)
        qs_s[p] = qs
        carry = jnp.zeros((2 * tq, tk), F32)
        ws, vbs = [], []
        for d in reversed(range(n_diag)):
            kb, vb = diag_kv(d, p)
            a, carry = _sb_weights(qs, kb, carry, tri2, col + d * tk < qrow)
            ws += [a[:tq], a[tq:]]
            vbs.append(vb)
        acc_s[p] = _sb_values(jnp.concatenate(ws, axis=1), vbs, head_lo)
        carry_s[p] = carry
    a_s[...] = jnp.zeros_like(a_s)

    def values(prev_start):
        for p in range(SB_PAIRS):
            vbs = [past_v(prev_start + (SB_U - 1 - u) * tk, p) for u in range(SB_U)]
            acc_s[p] += _sb_values(a_s[p], vbs, head_lo)

    def body(it, prev_start):
        start = pl.multiple_of((n_iter - 1 - it) * (SB_U * tk), SB_U * tk)
        values(prev_start)
        for p in range(SB_PAIRS):
            carry = carry_s[p]
            for u in range(SB_U):
                kb = past_k(start + (SB_U - 1 - u) * tk, p)
                a, carry = _sb_weights(qs_s[p], kb, carry, tri2, None)
                a_s[p, :, (2 * u) * tk:(2 * u + 1) * tk] = a[:tq]
                a_s[p, :, (2 * u + 1) * tk:(2 * u + 2) * tk] = a[tq:]
            carry_s[p] = carry
        return start

    values(lax.fori_loop(0, n_iter, body, 0))
    return jnp.concatenate([acc_s[p] for p in range(SB_PAIRS)], axis=1)


def _sb_tri2():
    j = lax.broadcasted_iota(jnp.int32, (2 * SB_TK, 2 * SB_TK), 0) % SB_TK
    s = lax.broadcasted_iota(jnp.int32, (2 * SB_TK, 2 * SB_TK), 1)
    return ((s >= SB_TK) | (j >= s)).astype(BF16)


def _sb_scratch(tq):
    return [pltpu.VMEM((SB_PAIRS, 2 * tq, LANES), BF16), pltpu.VMEM((SB_PAIRS, tq, SB_U * 2 * SB_TK), BF16),
            pltpu.VMEM((SB_PAIRS, tq, LANES), F32), pltpu.VMEM((SB_PAIRS, 2 * tq, SB_TK), F32)]


def _sb_block_reader(ref2d):
    def read(start, p):
        start = pl.multiple_of(start, SB_TK)
        return ref2d[pl.ds(start, SB_TK), p * LANES:(p + 1) * LANES].astype(BF16)
    return read


def _sb_prompt_kernel(q_ref, k_ref, v_ref, g_ref, tri_ref, o_ref, qs_s, a_s, acc_s, carry_s, *, tq):
    qi = pl.program_id(1)
    base = pl.multiple_of(qi * tq, tq)
    read_k, read_v = _sb_block_reader(k_ref.at[0]), _sb_block_reader(v_ref.at[0])

    def diag_kv(d, p):
        return read_k(base + d * SB_TK, p), read_v(base + d * SB_TK, p)

    n_iter = qi * (tq // (SB_U * SB_TK))
    y = _sb_all_pairs(q_ref[0], diag_kv, tq // SB_TK, read_k, read_v, n_iter, tri_ref[...],
                      qs_s, a_s, acc_s, carry_s)
    o_ref[0] = (y * jax.nn.silu(g_ref[0])).astype(BF16)


def _sb_prompt(q, k, v, g):
    b, l, _ = q.shape
    tq = SB_U * SB_TK
    qblk = pl.BlockSpec((1, tq, SB_WIDTH), lambda bi, qi: (bi, qi, 0))
    kvblk = pl.BlockSpec((1, l, SB_WIDTH), lambda bi, qi: (bi, 0, 0))
    return pl.pallas_call(
        functools.partial(_sb_prompt_kernel, tq=tq),
        grid=(b, l // tq),
        in_specs=[qblk, kvblk, kvblk, qblk, _full((2 * SB_TK, 2 * SB_TK))],
        out_specs=qblk,
        out_shape=jax.ShapeDtypeStruct((b, l, SB_WIDTH), BF16),
        scratch_shapes=_sb_scratch(tq),
        compiler_params=_params(("parallel", "arbitrary")),
        name="sb_prompt",
    )(q, k, v, g, _sb_tri2())


def _sb_sample_kernel(q_ref, k_ref, v_ref, ck_ref, cv_ref, g_ref, tri_ref, o_ref, qs_s, a_s, acc_s, carry_s,
                      *, n_iter):
    tq = q_ref.shape[1]

    def diag_kv(d, p):
        ls = slice(p * LANES, (p + 1) * LANES)
        pad = jnp.zeros((SB_TK - tq, LANES), BF16)
        return (jnp.concatenate([k_ref[0, :, ls].astype(BF16), pad], axis=0),
                jnp.concatenate([v_ref[0, :, ls].astype(BF16), pad], axis=0))

    y = _sb_all_pairs(q_ref[0], diag_kv, 1, _sb_block_reader(ck_ref.at[0]), _sb_block_reader(cv_ref.at[0]),
                      n_iter, tri_ref[...], qs_s, a_s, acc_s, carry_s)
    o_ref[0] = (y * jax.nn.silu(g_ref[0])).astype(BF16)


def _sb_sample(q, k, v, ck, cv, g):
    b, l, _ = q.shape
    past = ck.shape[1]
    qblk = pl.BlockSpec((1, l, SB_WIDTH), lambda bi: (bi, 0, 0))
    cblk = pl.BlockSpec((1, past, SB_WIDTH), lambda bi: (bi, 0, 0))
    return pl.pallas_call(
        functools.partial(_sb_sample_kernel, n_iter=past // (SB_U * SB_TK)),
        grid=(b,),
        in_specs=[qblk, qblk, qblk, cblk, cblk, qblk, _full((2 * SB_TK, 2 * SB_TK))],
        out_specs=qblk,
        out_shape=jax.ShapeDtypeStruct((b, l, SB_WIDTH), BF16),
        scratch_shapes=_sb_scratch(l),
        compiler_params=_params(("parallel",)),
        name="sb_sample",
    )(q, k, v, ck, cv, g, _sb_tri2())


def _s5_kernel(u_ref, g_ref, h0r_ref, h0i_ref, ar_ref, ai_ref, bh_ref, bl_ref, c_ref, d_ref,
               wg_ref, bg_ref, o_ref, hr_o, hi_o, bu_s, h_s, *, tt):
    ti = pl.program_id(1)
    rows = tt * SUBLANES

    @pl.when(ti == 0)
    def _():
        hr_o[...] = h0r_ref[...]
        hi_o[...] = h0i_ref[...]

    u = u_ref[...].reshape(rows, SSM_WIDTH)
    uh, ul = _split(u)
    bu_s[...] = _dot(uh, bh_ref[...]) + (_dot(uh, bl_ref[...]) + _dot(ul, bh_ref[...]))

    ar = jnp.broadcast_to(ar_ref[...], (SUBLANES, SSM_FLAT))
    ai = jnp.broadcast_to(ai_ref[...], (SUBLANES, SSM_FLAT))

    def step(t, carry):
        hr, hi = carry
        r0 = pl.multiple_of(t * SUBLANES, SUBLANES)
        br = bu_s[pl.ds(r0, SUBLANES), 0:SSM_FLAT]
        bi = bu_s[pl.ds(r0, SUBLANES), SSM_FLAT:2 * SSM_FLAT]
        nr = ar * hr - ai * hi + br
        ni = ar * hi + ai * hr + bi
        h_s[pl.ds(r0, SUBLANES), 0:SSM_FLAT] = nr
        h_s[pl.ds(r0, SUBLANES), SSM_FLAT:2 * SSM_FLAT] = ni
        return nr, ni

    hr, hi = lax.fori_loop(0, tt, step, (hr_o[...], hi_o[...]))
    hr_o[...] = hr
    hi_o[...] = hi

    y = _dot(h_s[...].astype(BF16), c_ref[...]) + d_ref[...] * u
    y = jax.nn.gelu(y)
    z = _dot(y.astype(BF16), wg_ref[...]) + bg_ref[...]
    out = z[:, :SSM_WIDTH] * jax.nn.sigmoid(z[:, SSM_WIDTH:])
    out = out * jax.nn.silu(g_ref[...].reshape(rows, SSM_WIDTH))
    o_ref[...] = out.reshape(tt, SUBLANES, SSM_WIDTH)


def _s5(u_t, g_t, h0r, h0i, sp):
    l, b, _ = u_t.shape
    tt = min(64, l)
    rows = tt * SUBLANES
    ublk = pl.BlockSpec((tt, SUBLANES, SSM_WIDTH), lambda bi, ti: (ti, bi, 0))
    sblk = pl.BlockSpec((SUBLANES, SSM_FLAT), lambda bi, ti: (bi, 0))
    return pl.pallas_call(
        functools.partial(_s5_kernel, tt=tt),
        grid=(b // SUBLANES, l // tt),
        in_specs=[ublk, ublk, sblk, sblk,
                  _full((1, SSM_FLAT)), _full((1, SSM_FLAT)),
                  _full((SSM_WIDTH, 2 * SSM_FLAT)), _full((SSM_WIDTH, 2 * SSM_FLAT)),
                  _full((2 * SSM_FLAT, SSM_WIDTH)), _full((1, SSM_WIDTH)),
                  _full((SSM_WIDTH, 2 * SSM_WIDTH)), _full((1, 2 * SSM_WIDTH))],
        out_specs=[ublk, sblk, sblk],
        out_shape=[jax.ShapeDtypeStruct((l, b, SSM_WIDTH), F32),
                   jax.ShapeDtypeStruct((b, SSM_FLAT), F32),
                   jax.ShapeDtypeStruct((b, SSM_FLAT), F32)],
        scratch_shapes=[pltpu.VMEM((rows, 2 * SSM_FLAT), F32), pltpu.VMEM((rows, 2 * SSM_FLAT), F32)],
        compiler_params=_params(("parallel", "arbitrary")),
        name="s5",
    )(u_t, g_t, h0r, h0i, sp["ar"], sp["ai"], sp["b_hi"], sp["b_lo"], sp["c"], sp["d"], sp["w_glu"], sp["b_glu"])


def _s5_params(lam_re, lam_im, log_dt, b_re, b_im, c_re, c_im, d, w_glu, b_glu):
    dt = jnp.exp(log_dt.astype(F32))[:, None]
    lr = jnp.minimum(lam_re.astype(F32), -1e-4)
    li = lam_im.astype(F32)
    er = jnp.exp(lr * dt)
    ar, ai = er * jnp.cos(li * dt), er * jnp.sin(li * dt)
    den = lr * lr + li * li
    fr = ((ar - 1.0) * lr + ai * li) / den
    fi = (ai * lr - (ar - 1.0) * li) / den
    br, bi = b_re.astype(F32), b_im.astype(F32)
    bbr = fr[..., None] * br - fi[..., None] * bi
    bbi = fr[..., None] * bi + fi[..., None] * br
    eye = jnp.eye(SSM_GROUPS, dtype=F32)
    bd_in = lambda m: jnp.einsum("gpc,gh->gchp", m, eye).reshape(SSM_WIDTH, SSM_FLAT)
    bd_out = lambda m: jnp.einsum("gcp,gh->gphc", m, eye).reshape(SSM_FLAT, SSM_WIDTH)
    b_all = jnp.concatenate([bd_in(bbr), bd_in(bbi)], axis=1)
    b_hi = b_all.astype(BF16)
    b_lo = (b_all - b_hi.astype(F32)).astype(BF16)
    c_all = jnp.concatenate([bd_out(c_re.astype(F32)), -bd_out(c_im.astype(F32))], axis=0)
    return dict(ar=ar.reshape(1, SSM_FLAT), ai=ai.reshape(1, SSM_FLAT), b_hi=b_hi, b_lo=b_lo,
                c=c_all.astype(BF16), d=d.astype(F32).reshape(1, SSM_WIDTH),
                w_glu=w_glu.astype(BF16), b_glu=b_glu.astype(F32).reshape(1, 2 * SSM_WIDTH))


def _rwkv_kernel(cp_ref, g_ref, prev_ref, s0_ref, mu_ref, w0_ref, w2_ref, a0_ref, a2_ref, kk_ref, ka_ref,
                 ub_ref, lnw_ref, lnb_ref, o_ref, s_o, prev_s, y_s, *, tt, ck):
    ti = pl.program_id(1)

    @pl.when(ti == 0)
    def _():
        s_o[...] = s0_ref[...]
        prev_s[...] = jnp.broadcast_to(prev_ref[0], (SUBLANES, CP_PAD))

    cp = cp_ref[0]
    trow = lax.broadcasted_iota(jnp.int32, (tt, 1), 0)
    shifted = jnp.where(trow == 0, prev_s[0:1, :], pltpu.roll(cp, 1, 0))
    prev_s[...] = jnp.broadcast_to(cp[tt - 1:tt, :], (SUBLANES, CP_PAD))
    xc = cp + mu_ref[...] * (shifted - cp)
    w = RWKV_WIDTH
    r, k, v, lora = xc[:, 0:w], xc[:, w:2 * w], xc[:, 2 * w:3 * w], xc[:, 3 * w:]

    logw = -DECAY_SCALE * jax.nn.sigmoid(w0_ref[...] + _dot(jnp.tanh(lora).astype(BF16), w2_ref[...]))
    a = jax.nn.sigmoid(a0_ref[...] + _dot(lora.astype(BF16), a2_ref[...]))

    hl = lax.broadcasted_iota(jnp.int32, (w, w), 0) // HEAD_DIM
    hc = lax.broadcasted_iota(jnp.int32, (w, w), 1) // HEAD_DIM
    head_ones = (hl == hc).astype(BF16)
    tr = lax.broadcasted_iota(jnp.int32, (tt, tt), 0)
    tc = lax.broadcasted_iota(jnp.int32, (tt, tt), 1)
    same_chunk = (tr // ck) == (tc // ck)
    chunk_tri = (same_chunk & (tc <= tr)).astype(BF16)
    chunk_ones = same_chunk.astype(BF16)

    cum = _dot2r(chunk_tri, logw)
    tot = _dot2r(chunk_ones, logw)
    kkv = k * kk_ref[...]
    kh = kkv * lax.rsqrt(_dot2l(kkv * kkv, head_ones) + 1e-12)
    kt = k * (1.0 + (a - 1.0) * ka_ref[...])
    bvec = a * kh
    w_inv = jnp.exp(-cum)
    khw = kh * jnp.exp(cum - logw)
    k_til = kt * w_inv
    b_til = bvec * w_inv
    rw = r * jnp.exp(cum)
    w_end = jnp.exp(tot - cum)
    k_hat = kt * w_end
    b_hat = bvec * w_end
    w_chunk = jnp.exp(tot)

    lane = lax.broadcasted_iota(jnp.int32, (1, LANES), 1)
    m_lo = lane < HEAD_DIM
    ci = lax.broadcasted_iota(jnp.int32, (ck, ck), 0)
    cj = lax.broadcasted_iota(jnp.int32, (ck, ck), 1)
    strict = cj < ci
    lower = cj <= ci
    eye = (ci == cj).astype(F32)
    sr = lax.broadcasted_iota(jnp.int32, (LANES, LANES), 0) // HEAD_DIM
    sc = lax.broadcasted_iota(jnp.int32, (LANES, LANES), 1) // HEAD_DIM
    blockdiag = sr == sc

    for hp in range(w // LANES):
        ls = slice(hp * LANES, (hp + 1) * LANES)
        s_pair = s_o[0, hp]
        for c in range(tt // ck):
            rs = slice(c * ck, (c + 1) * ck)
            khw_c, ktl_c, btl_c, rw_c = khw[rs, ls], k_til[rs, ls], b_til[rs, ls], rw[rs, ls]
            v_c, kht_c, bht_c = v[rs, ls], k_hat[rs, ls], b_hat[rs, ls]
            zero = jnp.zeros_like(khw_c)
            rhs = _dot3(khw_c, s_pair, NT)
            y_c = _dot3(rw_c, s_pair, NT)
            sa = jnp.zeros_like(rhs)
            parts = []
            for hh in range(2):
                mh = m_lo if hh == 0 else jnp.logical_not(m_lo)
                khw_h = jnp.where(mh, khw_c, zero)
                rw_h = jnp.where(mh, rw_c, zero)
                akk = jnp.where(strict, _dot3(khw_h, ktl_c, NT), 0.0)
                nmat = jnp.where(strict, _dot3(khw_h, btl_c, NT), 0.0)
                ark = jnp.where(lower, _dot3(rw_h, ktl_c, NT), 0.0)
                arb = jnp.where(lower, _dot3(rw_h, btl_c, NT), 0.0)
                inv = eye - nmat
                pw = nmat
                steps = 1
                while 2 * steps < ck:
                    pw = _dot3(pw, pw)
                    inv = _dot3(inv, eye + pw)
                    steps *= 2
                parts.append((mh, akk, ark, arb, inv))
            rhs_full = rhs
            for mh, akk, _, _, _ in parts:
                rhs_full = rhs_full + jnp.where(mh, _dot3(akk, v_c), 0.0)
            for mh, _, _, _, inv in parts:
                sa = sa + jnp.where(mh, _dot3(inv, rhs_full), 0.0)
            for mh, _, ark, arb, _ in parts:
                y_c = y_c + jnp.where(mh, _dot3(ark, v_c) - _dot3(arb, sa), 0.0)
            y_s[rs, ls] = y_c
            upd = _dot3(v_c, kht_c, TN) - _dot3(sa, bht_c, TN)
            s_pair = s_pair * w_chunk[c * ck:c * ck + 1, ls] + jnp.where(blockdiag, upd, 0.0)
        s_o[0, hp] = s_pair

    ys = y_s[...]
    mean = _dot2l(ys, head_ones) * (1.0 / HEAD_DIM)
    cen = ys - mean
    var = _dot2l(cen * cen, head_ones) * (1.0 / HEAD_DIM)
    y = cen * lax.rsqrt(var + GN_EPS) * lnw_ref[...] + lnb_ref[...]
    y = y + _dot2l(r * kt * ub_ref[...], head_ones) * v
    o_ref[0] = (y * jax.nn.silu(g_ref[0])).astype(BF16)


def _rwkv(cp, g, prev, s0p, rp):
    b, l, _ = cp.shape
    tt = min(256, l)
    ck = min(32, tt)
    tblk = lambda w: pl.BlockSpec((1, tt, w), lambda bi, ti: (bi, ti, 0))
    sblk = pl.BlockSpec((1, 2, LANES, LANES), lambda bi, ti: (bi, 0, 0, 0))
    vec = _full((1, RWKV_WIDTH))
    return pl.pallas_call(
        functools.partial(_rwkv_kernel, tt=tt, ck=ck),
        grid=(b, l // tt),
        in_specs=[tblk(CP_PAD), tblk(RWKV_WIDTH),
                  pl.BlockSpec((1, 1, CP_PAD), lambda bi, ti: (bi, 0, 0)), sblk,
                  _full((1, CP_PAD)), vec, _full((LANES, RWKV_WIDTH)), vec, _full((LANES, RWKV_WIDTH)),
                  vec, vec, vec, vec, vec],
        out_specs=[tblk(RWKV_WIDTH), sblk],
        out_shape=[jax.ShapeDtypeStruct((b, l, RWKV_WIDTH), BF16),
                   jax.ShapeDtypeStruct((b, 2, LANES, LANES), F32)],
        scratch_shapes=[pltpu.VMEM((SUBLANES, CP_PAD), F32), pltpu.VMEM((tt, RWKV_WIDTH), F32)],
        compiler_params=_params(("parallel", "arbitrary")),
        name="rwkv",
    )(cp, g, prev, s0p, rp["mu"], rp["w0"], rp["w2"], rp["a0"], rp["a2"], rp["k_k"], rp["k_a"],
      rp["u"], rp["ln_w"], rp["ln_b"])


def _rwkv_params(mu, w0, w2, a0, a2, k_k, k_a, u, ln_w, ln_b):
    vec = lambda t: t.astype(F32).reshape(1, RWKV_WIDTH)
    mu_p = jnp.zeros((1, CP_PAD), F32).at[0, :RWKV_SHIFT].set(mu.astype(F32))
    w2_p = jnp.zeros((LANES, RWKV_WIDTH), F32).at[:RWKV_LORA].set(w2.astype(F32)).astype(BF16)
    a2_p = jnp.zeros((LANES, RWKV_WIDTH), F32).at[RWKV_LORA:2 * RWKV_LORA].set(a2.astype(F32)).astype(BF16)
    return dict(mu=mu_p, w0=vec(w0), w2=w2_p, a0=vec(a0), a2=a2_p, k_k=vec(k_k), k_a=vec(k_a),
                u=vec(u), ln_w=vec(ln_w), ln_b=vec(ln_b))


def _pair_state(s):
    b = s.shape[0]
    s = s.astype(F32).reshape(b, 2, 2, HEAD_DIM, HEAD_DIM)
    eye = jnp.eye(2, dtype=F32)
    return jnp.einsum("bphvk,hg->bphvgk", s, eye).reshape(b, 2, LANES, LANES)


def _unpair_state(sp):
    b = sp.shape[0]
    s = sp.reshape(b, 2, 2, HEAD_DIM, 2, HEAD_DIM)
    return jnp.stack([s[:, :, 0, :, 0, :], s[:, :, 1, :, 1, :]], axis=2).reshape(b, RWKV_HEADS, HEAD_DIM, HEAD_DIM)


def _outproj_kernel(x_ref, a_ref, b_ref, c_ref, w_ref, o_ref):
    acc = _dot(a_ref[...], w_ref[0:SB_WIDTH, :])
    acc = acc + _dot(b_ref[...], w_ref[SB_WIDTH:SB_WIDTH + SSM_WIDTH, :])
    acc = acc + _dot(c_ref[...], w_ref[SB_WIDTH + SSM_WIDTH:, :])
    o_ref[...] = x_ref[...] + acc


def _outproj(x2d, ma, mb, mc, w_out):
    n = x2d.shape[0]
    tm = min(512, n)
    row = lambda w: pl.BlockSpec((tm, w), lambda i: (i, 0))
    return pl.pallas_call(
        _outproj_kernel,
        grid=(n // tm,),
        in_specs=[row(D_MODEL), row(SB_WIDTH), row(SSM_WIDTH), row(RWKV_WIDTH), _full((D_MODEL, D_MODEL))],
        out_specs=row(D_MODEL),
        out_shape=jax.ShapeDtypeStruct((n, D_MODEL), F32),
        compiler_params=_params(("parallel",)),
        name="outproj",
    )(x2d, ma, mb, mc, w_out)


def _pad_w_in(w_in):
    w = w_in.astype(F32)
    pad = jnp.zeros((D_MODEL, CP_PAD - RWKV_SHIFT), F32)
    return jnp.concatenate([w[:, :COL_CP + RWKV_SHIFT], pad, w[:, COL_CP + RWKV_SHIFT:]], axis=1).astype(BF16)


def _layer(x, lp, past):
    b, l, _ = x.shape
    n = b * l
    x2d = x.reshape(n, D_MODEL)
    q, k, v, ga, u, gb, cp, gc = _inproj(x2d, lp["norm_w"], lp["w_in"], lp["q_w"], lp["k_w"])
    r3 = lambda t: t.reshape(b, l, t.shape[-1])
    q, k, v, ga, u, gb, cp, gc = map(r3, (q, k, v, ga, u, gb, cp, gc))

    if past is None:
        mix_a = _sb_prompt(q, k, v, ga)
        h0r = h0i = jnp.zeros((b, SSM_FLAT), F32)
        prev = jnp.zeros((b, 1, CP_PAD), F32)
        s0p = jnp.zeros((b, 2, LANES, LANES), F32)
    else:
        ck = past["k"].reshape(b, -1, SB_WIDTH)
        cv = past["v"].reshape(b, -1, SB_WIDTH)
        mix_a = _sb_sample(q, k, v, ck, cv, ga)
        h0r = past["ssm_re"].astype(F32).reshape(b, SSM_FLAT)
        h0i = past["ssm_im"].astype(F32).reshape(b, SSM_FLAT)
        prev = jnp.pad(past["shift"].astype(F32), ((0, 0), (0, CP_PAD - RWKV_SHIFT)))[:, None, :]
        s0p = _pair_state(past["wkv"])

    tm = lambda t: jnp.swapaxes(t, 0, 1)
    yb_t, hr, hi = _s5(tm(u), tm(gb), h0r, h0i, lp["s5"])
    mix_b = tm(yb_t).astype(BF16)
    mix_c, s_pair = _rwkv(cp, gc, prev, s0p, lp["rwkv"])

    x_new = _outproj(x2d, mix_a.reshape(n, SB_WIDTH), mix_b.reshape(n, SSM_WIDTH),
                     mix_c.reshape(n, RWKV_WIDTH), lp["w_out"]).reshape(b, l, D_MODEL)
    heads = lambda t: t.reshape(b, l, SB_WIDTH // HEAD_DIM, HEAD_DIM)
    state = (heads(k), heads(v), hr.reshape(b, SSM_GROUPS, SSM_STATE), hi.reshape(b, SSM_GROUPS, SSM_STATE),
             _unpair_state(s_pair), cp[:, -1, :RWKV_SHIFT])
    return x_new, state


def kernel(x_prompt, x_sample, cache_k, cache_v, state_ssm_re, state_ssm_im, state_wkv, state_shift, norm_w, w_in, q_norm_w, k_norm_w, ssm_lambda_re, ssm_lambda_im, ssm_log_dt, ssm_b_re, ssm_b_im, ssm_c_re, ssm_c_im, ssm_d, ssm_w_glu, ssm_b_glu, rwkv_mu, rwkv_w0, rwkv_w2, rwkv_a0, rwkv_a2, rwkv_k_k, rwkv_k_a, rwkv_u, rwkv_ln_w, rwkv_ln_b, w_out):
    depth = w_in.shape[0]
    xp, xs = x_prompt, x_sample
    p_states, s_states = [], []
    for i in range(depth):
        tile8 = lambda t: jnp.tile(t.astype(F32), SB_WIDTH // HEAD_DIM).reshape(1, SB_WIDTH)
        lp = dict(
            norm_w=norm_w[i].astype(F32).reshape(1, D_MODEL), w_in=_pad_w_in(w_in[i]),
            q_w=tile8(q_norm_w[i]), k_w=tile8(k_norm_w[i]),
            s5=_s5_params(ssm_lambda_re[i], ssm_lambda_im[i], ssm_log_dt[i], ssm_b_re[i], ssm_b_im[i],
                          ssm_c_re[i], ssm_c_im[i], ssm_d[i], ssm_w_glu[i], ssm_b_glu[i]),
            rwkv=_rwkv_params(rwkv_mu[i], rwkv_w0[i], rwkv_w2[i], rwkv_a0[i], rwkv_a2[i], rwkv_k_k[i],
                              rwkv_k_a[i], rwkv_u[i].reshape(-1), rwkv_ln_w[i], rwkv_ln_b[i]),
            w_out=w_out[i].astype(BF16))
        xp, st_p = _layer(xp, lp, None)
        past = dict(k=cache_k[i], v=cache_v[i], ssm_re=state_ssm_re[i], ssm_im=state_ssm_im[i],
                    wkv=state_wkv[i], shift=state_shift[i])
        xs, st_s = _layer(xs, lp, past)
        p_states.append(st_p)
        s_states.append(st_s)
    stack = lambda states, j: jnp.stack([s[j] for s in states], axis=0)
    return ((xp, xs) + tuple(stack(p_states, j) for j in range(6))
            + tuple(stack(s_states, j) for j in range(6)))
```

```python
import functools
import math

import jax
import jax.numpy as jnp
from jax import lax
from jax.experimental import pallas as pl
from jax.experimental.pallas import tpu as pltpu

F32 = jnp.float32
BF16 = jnp.bfloat16

D_MODEL = 1024
HEAD_DIM = 64
SB_WIDTH = D_MODEL // 2
SSM_WIDTH = D_MODEL // 4
SSM_GROUP = 16
SSM_GROUPS = SSM_WIDTH // SSM_GROUP
SSM_STATE = 64
SSM_FLAT = SSM_GROUPS * SSM_STATE
RWKV_WIDTH = D_MODEL // 4
RWKV_HEADS = RWKV_WIDTH // HEAD_DIM
RWKV_LORA = 32
RWKV_SHIFT = 3 * RWKV_WIDTH + 2 * RWKV_LORA
RMS_EPS = 1e-6
GN_EPS = 64e-5
DECAY_SCALE = math.exp(-0.5)

LANES = 128
SUBLANES = 8
CP_PAD = 7 * LANES
COL_Q, COL_K, COL_V, COL_GA = 0, 512, 1024, 1536
COL_U, COL_GB, COL_CP, COL_GC = 2048, 2304, 2560, 2560 + CP_PAD
D_IN_PAD = COL_GC + RWKV_WIDTH
VMEM_LIMIT = 56 * 1024 * 1024

NN = (((1,), (0,)), ((), ()))
NT = (((1,), (1,)), ((), ()))
TN = (((0,), (0,)), ((), ()))


def _dot(a, b, dims=NN):
    return lax.dot_general(a, b, dims, preferred_element_type=F32)


def _split(a):
    hi = a.astype(BF16)
    lo = (a - hi.astype(F32)).astype(BF16)
    return hi, lo


def _dot3(a, b, dims=NN):
    ah, al = _split(a)
    bh, bl = _split(b)
    return _dot(ah, bh, dims) + (_dot(ah, bl, dims) + _dot(al, bh, dims))


def _dot2l(a, b_exact, dims=NN):
    ah, al = _split(a)
    return _dot(ah, b_exact, dims) + _dot(al, b_exact, dims)


def _dot2r(a_exact, b, dims=NN):
    bh, bl = _split(b)
    return _dot(a_exact, bh, dims) + _dot(a_exact, bl, dims)


def _params(sem):
    return pltpu.CompilerParams(dimension_semantics=sem, vmem_limit_bytes=VMEM_LIMIT)


def _full(shape):
    nd = len(shape)
    return pl.BlockSpec(shape, lambda *_: (0,) * nd)


def _head_rms(t, w):
    lane = lax.broadcasted_iota(jnp.int32, (1, LANES), 1)
    lo = lane < HEAD_DIM
    outs = []
    for j in range(t.shape[1] // LANES):
        blk = t[:, j * LANES:(j + 1) * LANES]
        sq = blk * blk
        s_lo = jnp.sum(jnp.where(lo, sq, 0.0), axis=-1, keepdims=True)
        s_hi = jnp.sum(jnp.where(lo, 0.0, sq), axis=-1, keepdims=True)
        ms = jnp.where(lo, s_lo, s_hi) * (1.0 / HEAD_DIM)
        outs.append(blk * lax.rsqrt(ms + RMS_EPS) * w[:, j * LANES:(j + 1) * LANES])
    return jnp.concatenate(outs, axis=-1)


def _inproj_kernel(x_ref, nw_ref, w_ref, qw_ref, kw_ref,
                   q_o, k_o, v_o, ga_o, u_o, gb_o, cp_o, gc_o):
    x = x_ref[...]
    ms = jnp.mean(x * x, axis=-1, keepdims=True)
    h = (x * lax.rsqrt(ms + RMS_EPS) * nw_ref[...]).astype(BF16)

    def proj(a, b):
        return _dot(h, w_ref[:, a:b])

    q = _head_rms(proj(COL_Q, COL_K), qw_ref[...])
    q_o[...] = (q * (HEAD_DIM ** -0.5)).astype(BF16)
    k_o[...] = _head_rms(proj(COL_K, COL_V), kw_ref[...])
    v_o[...] = proj(COL_V, COL_GA)
    ga_o[...] = proj(COL_GA, COL_U)
    u_o[...] = proj(COL_U, COL_GB)
    gb_o[...] = proj(COL_GB, COL_CP)
    cp_o[...] = proj(COL_CP, COL_GC)
    gc_o[...] = proj(COL_GC, D_IN_PAD)


def _inproj(x2d, nw, w_pad, qw, kw):
    n = x2d.shape[0]
    tm = min(256, n)
    widths = (SB_WIDTH, SB_WIDTH, SB_WIDTH, SB_WIDTH, SSM_WIDTH, SSM_WIDTH, CP_PAD, RWKV_WIDTH)
    dtypes = (BF16,) + (F32,) * 7
    row = lambda w: pl.BlockSpec((tm, w), lambda i: (i, 0))
    return pl.pallas_call(
        _inproj_kernel,
        grid=(n // tm,),
        in_specs=[row(D_MODEL), _full((1, D_MODEL)), _full((D_MODEL, D_IN_PAD)),
                  _full((1, SB_WIDTH)), _full((1, SB_WIDTH))],
        out_specs=[row(w) for w in widths],
        out_shape=[jax.ShapeDtypeStruct((n, w), dt) for w, dt in zip(widths, dtypes)],
        compiler_params=_params(("parallel",)),
        name="inproj",
    )(x2d, nw, w_pad, qw, kw)


SB_TK = 128
SB_PAIRS = SB_WIDTH // LANES


SB_U = 2
SB_GROUP = 4
LOG2E = math.log2(math.e)


def _sb_weights(qs, kbs, carries, tri, masks):
    weights, new_carries = [], []
    for p0 in range(0, len(qs), SB_GROUP):
        group = range(p0, min(p0 + SB_GROUP, len(qs)))
        pu = [(p, u) for p in group for u in range(len(kbs[p]))]
        z2 = {k: _dot(qs[k[0]], kbs[k[0]][k[1]], NT) * LOG2E for k in pu}
        hl = {}
        for k in pu:
            sp = jnp.maximum(z2[k], 0.0) + jnp.log2(1.0 + jnp.exp2(-jnp.abs(z2[k])))
            if masks[k[1]] is not None:
                sp = jnp.where(masks[k[1]], sp, 0.0)
            hl[k] = jnp.concatenate(_split(sp), axis=1)
        cs = {k: _dot(hl[k], tri) for k in pu}
        for p in group:
            carry, row = carries[p], []
            for u in range(len(kbs[p])):
                incl = cs[(p, u)] + carry
                a = jnp.exp2(z2[(p, u)] - incl)
                if masks[u] is not None:
                    a = jnp.where(masks[u], a, 0.0)
                row.append(a.astype(BF16))
                carry = jnp.broadcast_to(incl[:, 0:1], incl.shape)
            weights.append(row)
            new_carries.append(carry)
    return weights, new_carries


def _sb_values(a2, vbs, head_lo):
    parts = []
    for vb in vbs:
        zero = jnp.zeros_like(vb)
        parts += [jnp.where(head_lo, vb, zero), jnp.where(head_lo, zero, vb)]
    return _dot(a2, jnp.concatenate(parts, axis=0))


def _sb_all_pairs(q, diag_kv, n_diag, past_k, past_v, n_iter, tri, qs_s, a_s, acc_s, carry_s):
    tq, tk = q.shape[0], SB_TK
    head_lo = lax.broadcasted_iota(jnp.int32, (1, LANES), 1) < HEAD_DIM
    row = lax.broadcasted_iota(jnp.int32, (2 * tq, tk), 0)
    col = lax.broadcasted_iota(jnp.int32, (2 * tq, tk), 1)
    qrow = jnp.where(row >= tq, row - tq, row)
    pairs = range(SB_PAIRS)
    qs = []
    for p in pairs:
        qp = q[:, p * LANES:(p + 1) * LANES]
        zero = jnp.zeros_like(qp)
        qs.append(jnp.concatenate([jnp.where(head_lo, qp, zero), jnp.where(head_lo, zero, qp)], axis=0))
        qs_s[p] = qs[p]
    diag = [[diag_kv(d, p) for d in reversed(range(n_diag))] for p in pairs]
    masks = [col + d * tk < qrow for d in reversed(range(n_diag))]
    ws, carries = _sb_weights(qs, [[kv[0] for kv in diag[p]] for p in pairs],
                              [jnp.zeros((2 * tq, tk), F32)] * SB_PAIRS, tri, masks)
    for p in pairs:
        a2 = jnp.concatenate([half for a in ws[p] for half in (a[:tq], a[tq:])], axis=1)
        acc_s[p] = _sb_values(a2, [kv[1] for kv in diag[p]], head_lo)
        carry_s[p] = carries[p]
    a_s[...] = jnp.zeros_like(a_s)

    def values(prev_start):
        for p in pairs:
            vbs = [past_v(prev_start + (SB_U - 1 - u) * tk, p) for u in range(SB_U)]
            acc_s[p] += _sb_values(a_s[p], vbs, head_lo)

    def body(it, prev_start):
        start = pl.multiple_of((n_iter - 1 - it) * (SB_U * tk), SB_U * tk)
        values(prev_start)
        kbs = [[past_k(start + (SB_U - 1 - u) * tk, p) for u in range(SB_U)] for p in pairs]
        ws, carries = _sb_weights([qs_s[p] for p in pairs], kbs, [carry_s[p] for p in pairs], tri,
                                  [None] * SB_U)
        for p in pairs:
            for u in range(SB_U):
                a_s[p, :, (2 * u) * tk:(2 * u + 1) * tk] = ws[p][u][:tq]
                a_s[p, :, (2 * u + 1) * tk:(2 * u + 2) * tk] = ws[p][u][tq:]
            carry_s[p] = carries[p]
        return start

    values(lax.fori_loop(0, n_iter, body, 0))
    return jnp.concatenate([acc_s[p] for p in range(SB_PAIRS)], axis=1)


def _sb_tri():
    j = lax.broadcasted_iota(jnp.int32, (2 * SB_TK, SB_TK), 0) % SB_TK
    s = lax.broadcasted_iota(jnp.int32, (2 * SB_TK, SB_TK), 1)
    return (j >= s).astype(BF16)


def _sb_scratch(tq):
    return [pltpu.VMEM((SB_PAIRS, 2 * tq, LANES), BF16), pltpu.VMEM((SB_PAIRS, tq, SB_U * 2 * SB_TK), BF16),
            pltpu.VMEM((SB_PAIRS, tq, LANES), F32), pltpu.VMEM((SB_PAIRS, 2 * tq, SB_TK), F32)]


def _sb_block_reader(ref2d):
    def read(start, p):
        start = pl.multiple_of(start, SB_TK)
        return ref2d[pl.ds(start, SB_TK), p * LANES:(p + 1) * LANES].astype(BF16)
    return read


def _sb_prompt_kernel(q_ref, k_ref, v_ref, g_ref, tri_ref, o_ref, qs_s, a_s, acc_s, carry_s, *, tq):
    qi = pl.program_id(1)
    base = pl.multiple_of(qi * tq, tq)
    read_k, read_v = _sb_block_reader(k_ref.at[0]), _sb_block_reader(v_ref.at[0])

    def diag_kv(d, p):
        return read_k(base + d * SB_TK, p), read_v(base + d * SB_TK, p)

    n_iter = qi * (tq // (SB_U * SB_TK))
    y = _sb_all_pairs(q_ref[0], diag_kv, tq // SB_TK, read_k, read_v, n_iter, tri_ref[...],
                      qs_s, a_s, acc_s, carry_s)
    o_ref[0] = (y * jax.nn.silu(g_ref[0])).astype(BF16)


def _sb_prompt(q, k, v, g):
    b, l, _ = q.shape
    tq = SB_U * SB_TK
    qblk = pl.BlockSpec((1, tq, SB_WIDTH), lambda bi, qi: (bi, qi, 0))
    kvblk = pl.BlockSpec((1, l, SB_WIDTH), lambda bi, qi: (bi, 0, 0))
    return pl.pallas_call(
        functools.partial(_sb_prompt_kernel, tq=tq),
        grid=(b, l // tq),
        in_specs=[qblk, kvblk, kvblk, qblk, _full((2 * SB_TK, SB_TK))],
        out_specs=qblk,
        out_shape=jax.ShapeDtypeStruct((b, l, SB_WIDTH), BF16),
        scratch_shapes=_sb_scratch(tq),
        compiler_params=_params(("parallel", "arbitrary")),
        name="sb_prompt",
    )(q, k, v, g, _sb_tri())


def _sb_sample_kernel(q_ref, k_ref, v_ref, ck_ref, cv_ref, g_ref, tri_ref, o_ref, qs_s, a_s, acc_s, carry_s,
                      *, n_iter):
    tq = q_ref.shape[1]

    def diag_kv(d, p):
        ls = slice(p * LANES, (p + 1) * LANES)
        pad = jnp.zeros((SB_TK - tq, LANES), BF16)
        return (jnp.concatenate([k_ref[0, :, ls].astype(BF16), pad], axis=0),
                jnp.concatenate([v_ref[0, :, ls].astype(BF16), pad], axis=0))

    y = _sb_all_pairs(q_ref[0], diag_kv, 1, _sb_block_reader(ck_ref.at[0]), _sb_block_reader(cv_ref.at[0]),
                      n_iter, tri_ref[...], qs_s, a_s, acc_s, carry_s)
    o_ref[0] = (y * jax.nn.silu(g_ref[0])).astype(BF16)


def _sb_sample(q, k, v, ck, cv, g):
    b, l, _ = q.shape
    past = ck.shape[1]
    qblk = pl.BlockSpec((1, l, SB_WIDTH), lambda bi: (bi, 0, 0))
    cblk = pl.BlockSpec((1, past, SB_WIDTH), lambda bi: (bi, 0, 0))
    return pl.pallas_call(
        functools.partial(_sb_sample_kernel, n_iter=past // (SB_U * SB_TK)),
        grid=(b,),
        in_specs=[qblk, qblk, qblk, cblk, cblk, qblk, _full((2 * SB_TK, SB_TK))],
        out_specs=qblk,
        out_shape=jax.ShapeDtypeStruct((b, l, SB_WIDTH), BF16),
        scratch_shapes=_sb_scratch(l),
        compiler_params=_params(("parallel",)),
        name="sb_sample",
    )(q, k, v, ck, cv, g, _sb_tri())


def _s5_kernel(u_ref, g_ref, h0r_ref, h0i_ref, ar_ref, ai_ref, bh_ref, bl_ref, c_ref, d_ref,
               wg_ref, bg_ref, o_ref, hr_o, hi_o, bu_s, h_s, *, tt):
    ti = pl.program_id(1)
    rows = tt * SUBLANES

    @pl.when(ti == 0)
    def _():
        hr_o[...] = h0r_ref[...]
        hi_o[...] = h0i_ref[...]

    u = u_ref[...].reshape(rows, SSM_WIDTH)
    uh, ul = _split(u)
    bu_s[...] = _dot(uh, bh_ref[...]) + (_dot(uh, bl_ref[...]) + _dot(ul, bh_ref[...]))

    ar = jnp.broadcast_to(ar_ref[...], (SUBLANES, SSM_FLAT))
    ai = jnp.broadcast_to(ai_ref[...], (SUBLANES, SSM_FLAT))

    def step(t, carry):
        hr, hi = carry
        r0 = pl.multiple_of(t * SUBLANES, SUBLANES)
        br = bu_s[pl.ds(r0, SUBLANES), 0:SSM_FLAT]
        bi = bu_s[pl.ds(r0, SUBLANES), SSM_FLAT:2 * SSM_FLAT]
        nr = ar * hr - ai * hi + br
        ni = ar * hi + ai * hr + bi
        h_s[pl.ds(r0, SUBLANES), 0:SSM_FLAT] = nr
        h_s[pl.ds(r0, SUBLANES), SSM_FLAT:2 * SSM_FLAT] = ni
        return nr, ni

    hr, hi = lax.fori_loop(0, tt, step, (hr_o[...], hi_o[...]))
    hr_o[...] = hr
    hi_o[...] = hi

    y = _dot(h_s[...].astype(BF16), c_ref[...]) + d_ref[...] * u
    y = jax.nn.gelu(y)
    z = _dot(y.astype(BF16), wg_ref[...]) + bg_ref[...]
    out = z[:, :SSM_WIDTH] * jax.nn.sigmoid(z[:, SSM_WIDTH:])
    out = out * jax.nn.silu(g_ref[...].reshape(rows, SSM_WIDTH))
    o_ref[...] = out.reshape(tt, SUBLANES, SSM_WIDTH)


def _s5(u_t, g_t, h0r, h0i, sp):
    l, b, _ = u_t.shape
    tt = min(64, l)
    rows = tt * SUBLANES
    ublk = pl.BlockSpec((tt, SUBLANES, SSM_WIDTH), lambda bi, ti: (ti, bi, 0))
    sblk = pl.BlockSpec((SUBLANES, SSM_FLAT), lambda bi, ti: (bi, 0))
    return pl.pallas_call(
        functools.partial(_s5_kernel, tt=tt),
        grid=(b // SUBLANES, l // tt),
        in_specs=[ublk, ublk, sblk, sblk,
                  _full((1, SSM_FLAT)), _full((1, SSM_FLAT)),
                  _full((SSM_WIDTH, 2 * SSM_FLAT)), _full((SSM_WIDTH, 2 * SSM_FLAT)),
                  _full((2 * SSM_FLAT, SSM_WIDTH)), _full((1, SSM_WIDTH)),
                  _full((SSM_WIDTH, 2 * SSM_WIDTH)), _full((1, 2 * SSM_WIDTH))],
        out_specs=[ublk, sblk, sblk],
        out_shape=[jax.ShapeDtypeStruct((l, b, SSM_WIDTH), F32),
                   jax.ShapeDtypeStruct((b, SSM_FLAT), F32),
                   jax.ShapeDtypeStruct((b, SSM_FLAT), F32)],
        scratch_shapes=[pltpu.VMEM((rows, 2 * SSM_FLAT), F32), pltpu.VMEM((rows, 2 * SSM_FLAT), F32)],
        compiler_params=_params(("parallel", "arbitrary")),
        name="s5",
    )(u_t, g_t, h0r, h0i, sp["ar"], sp["ai"], sp["b_hi"], sp["b_lo"], sp["c"], sp["d"], sp["w_glu"], sp["b_glu"])


def _s5_params(lam_re, lam_im, log_dt, b_re, b_im, c_re, c_im, d, w_glu, b_glu):
    dt = jnp.exp(log_dt.astype(F32))[:, None]
    lr = jnp.minimum(lam_re.astype(F32), -1e-4)
    li = lam_im.astype(F32)
    er = jnp.exp(lr * dt)
    ar, ai = er * jnp.cos(li * dt), er * jnp.sin(li * dt)
    den = lr * lr + li * li
    fr = ((ar - 1.0) * lr + ai * li) / den
    fi = (ai * lr - (ar - 1.0) * li) / den
    br, bi = b_re.astype(F32), b_im.astype(F32)
    bbr = fr[..., None] * br - fi[..., None] * bi
    bbi = fr[..., None] * bi + fi[..., None] * br
    eye = jnp.eye(SSM_GROUPS, dtype=F32)
    bd_in = lambda m: jnp.einsum("gpc,gh->gchp", m, eye).reshape(SSM_WIDTH, SSM_FLAT)
    bd_out = lambda m: jnp.einsum("gcp,gh->gphc", m, eye).reshape(SSM_FLAT, SSM_WIDTH)
    b_all = jnp.concatenate([bd_in(bbr), bd_in(bbi)], axis=1)
    b_hi = b_all.astype(BF16)
    b_lo = (b_all - b_hi.astype(F32)).astype(BF16)
    c_all = jnp.concatenate([bd_out(c_re.astype(F32)), -bd_out(c_im.astype(F32))], axis=0)
    return dict(ar=ar.reshape(1, SSM_FLAT), ai=ai.reshape(1, SSM_FLAT), b_hi=b_hi, b_lo=b_lo,
                c=c_all.astype(BF16), d=d.astype(F32).reshape(1, SSM_WIDTH),
                w_glu=w_glu.astype(BF16), b_glu=b_glu.astype(F32).reshape(1, 2 * SSM_WIDTH))


RWKV_CK = 32
RWKV_NB = 8


def _rwkv_kernel(cp_ref, g_ref, prev_ref, s0_ref, mu_ref, w0_ref, w2_ref, a0_ref, a2_ref, kk_ref, ka_ref,
                 ub_ref, lnw_ref, lnb_ref, ho_ref, tri_ref, ones_ref, o_ref, s_o, prev_s,
                 khw_s, rw_s, ktl_s, btl_s, v_s, kht_s, bht_s, wch_s, bonus_s, y_s, *, tt, ck):
    ti = pl.program_id(1)

    @pl.when(ti == 0)
    def _():
        s_o[...] = s0_ref[...]
        prev_s[...] = jnp.broadcast_to(prev_ref[...], prev_s.shape)

    w = RWKV_WIDTH
    nb_all = cp_ref.shape[0]
    head_ones, chunk_tri, chunk_ones = ho_ref[...], tri_ref[...], ones_ref[...]

    def prep(nb, carry):
        cp = cp_ref[nb]
        trow = lax.broadcasted_iota(jnp.int32, (tt, 1), 0)
        shifted = jnp.where(trow == 0, prev_s[nb, 0:1, :], pltpu.roll(cp, 1, 0))
        prev_s[nb] = jnp.broadcast_to(cp[tt - 1:tt, :], (SUBLANES, CP_PAD))
        xc = cp + mu_ref[...] * (shifted - cp)
        r, k, v, lora = xc[:, 0:w], xc[:, w:2 * w], xc[:, 2 * w:3 * w], xc[:, 3 * w:]
        logw = -DECAY_SCALE * jax.nn.sigmoid(w0_ref[...] + _dot(jnp.tanh(lora).astype(BF16), w2_ref[...]))
        a = jax.nn.sigmoid(a0_ref[...] + _dot(lora.astype(BF16), a2_ref[...]))
        cum = _dot2r(chunk_tri, logw)
        tot = _dot2r(chunk_ones, logw)
        kkv = k * kk_ref[...]
        kh = kkv * lax.rsqrt(_dot2l(kkv * kkv, head_ones) + 1e-12)
        kt = k * (1.0 + (a - 1.0) * ka_ref[...])
        bvec = a * kh
        w_inv = jnp.exp(-cum)
        w_end = jnp.exp(tot - cum)
        khw_s[nb] = (kh * jnp.exp(cum - logw)).astype(BF16)
        rw_s[nb] = (r * jnp.exp(cum)).astype(BF16)
        ktl_s[nb] = (kt * w_inv).astype(BF16)
        btl_s[nb] = (bvec * w_inv).astype(BF16)
        v_s[nb] = v.astype(BF16)
        kht_s[nb] = (kt * w_end).astype(BF16)
        bht_s[nb] = (bvec * w_end).astype(BF16)
        wch_s[nb] = jnp.exp(tot)
        bonus_s[nb] = _dot2l(r * kt * ub_ref[...], head_ones) * v
        return carry

    lax.fori_loop(0, nb_all, prep, 0)

    lane = lax.broadcasted_iota(jnp.int32, (1, LANES), 1)
    m_lo = lane < HEAD_DIM
    ri = lax.broadcasted_iota(jnp.int32, (2 * ck, 2 * ck), 0)
    cj = lax.broadcasted_iota(jnp.int32, (2 * ck, 2 * ck), 1)
    same_head = (ri >= ck) == (cj >= ck)
    strict = same_head & (cj < ri)
    lower = same_head & (cj <= ri)
    eye = (ri == cj).astype(F32)
    rr = lax.broadcasted_iota(jnp.int32, (2 * ck, LANES), 0)
    rl = lax.broadcasted_iota(jnp.int32, (2 * ck, LANES), 1)
    own_lanes = (rr >= ck) == (rl >= HEAD_DIM)
    sr = lax.broadcasted_iota(jnp.int32, (LANES, LANES), 0) // HEAD_DIM
    sc = lax.broadcasted_iota(jnp.int32, (LANES, LANES), 1) // HEAD_DIM
    blockdiag = sr == sc
    dot1 = lambda x, y: _dot(x.astype(BF16), y.astype(BF16))

    cat = lambda xs: jnp.concatenate(xs, axis=0)

    def chunk_step(c, carry):
        r0 = pl.multiple_of(c * ck, ck)
        rows = pl.ds(r0, ck)
        chains = [(nb, hp) for nb in range(nb_all) for hp in range(w // LANES)]
        lanes = lambda hp: slice(hp * LANES, (hp + 1) * LANES)
        rd = lambda ref: [ref[nb, rows, lanes(hp)] for nb, hp in chains]
        khw, rw, ktl, btl, v = rd(khw_s), rd(rw_s), rd(ktl_s), rd(btl_s), rd(v_s)
        z = jnp.zeros_like(khw[0])
        l4 = [cat([jnp.where(m_lo, a, z), jnp.where(m_lo, z, a), jnp.where(m_lo, b, z), jnp.where(m_lo, z, b)])
              for a, b in zip(khw, rw)]
        scb = [_dot(a, cat([b, b]), NT) for a, b in zip(l4, btl)]
        sck = [_dot(a, cat([b, b]), NT) for a, b in zip(l4, ktl)]
        nmat = [jnp.where(strict, a[:2 * ck], 0.0) for a in scb]
        arb = [jnp.where(lower, a[2 * ck:], 0.0) for a in scb]
        akk = [jnp.where(strict, a[:2 * ck], 0.0) for a in sck]
        ark = [jnp.where(lower, a[2 * ck:], 0.0) for a in sck]
        inv = [eye - a for a in nmat]
        pw = nmat
        steps = 1
        while 2 * steps < ck:
            pw = [dot1(a, a) for a in pw]
            inv = [dot1(a, eye + b) for a, b in zip(inv, pw)]
            steps *= 2
        s_pair = [s_o[nb, hp] for nb, hp in chains]
        ks = [_dot(cat([a, b]), s.astype(BF16), NT) for a, b, s in zip(khw, rw, s_pair)]
        av = [dot1(cat([a, b]), cat([x, x])) for a, b, x in zip(akk, ark, v)]
        sa2 = [jnp.where(own_lanes, dot1(t, cat([k[:ck], k[:ck]]) + a[:2 * ck]), 0.0)
               for t, k, a in zip(inv, ks, av)]
        y2 = [jnp.where(own_lanes, a[2 * ck:] - dot1(b, s), 0.0) for a, b, s in zip(av, arb, sa2)]
        for (nb, hp), k, y in zip(chains, ks, y2):
            y_s[nb, rows, lanes(hp)] = k[ck:] + y[:ck] + y[ck:]
        kht, bht = rd(kht_s), rd(bht_s)
        upd = [_dot(cat([x, (-(s[:ck] + s[ck:])).astype(BF16)]), cat([a, b]), TN)
               for x, s, a, b in zip(v, sa2, kht, bht)]
        for (nb, hp), s, u in zip(chains, s_pair, upd):
            s_o[nb, hp] = s * wch_s[nb, pl.ds(r0, 1), lanes(hp)] + jnp.where(blockdiag, u, 0.0)
        return carry

    lax.fori_loop(0, tt // ck, chunk_step, 0)

    def finish(nb, carry):
        ys = y_s[nb]
        mean = _dot2l(ys, head_ones) * (1.0 / HEAD_DIM)
        cen = ys - mean
        var = _dot2l(cen * cen, head_ones) * (1.0 / HEAD_DIM)
        y = cen * lax.rsqrt(var + GN_EPS) * lnw_ref[...] + lnb_ref[...] + bonus_s[nb]
        o_ref[nb] = (y * jax.nn.silu(g_ref[nb])).astype(BF16)
        return carry

    lax.fori_loop(0, nb_all, finish, 0)


def _rwkv(cp, g, prev, s0p, rp):
    b, l, _ = cp.shape
    tt = min(256, l)
    ck = min(RWKV_CK, tt)
    nb = min(RWKV_NB, b)
    tblk = lambda w: pl.BlockSpec((nb, tt, w), lambda bi, ti: (bi, ti, 0))
    sblk = pl.BlockSpec((nb, 2, LANES, LANES), lambda bi, ti: (bi, 0, 0, 0))
    vec = _full((1, RWKV_WIDTH))
    hl = lax.broadcasted_iota(jnp.int32, (RWKV_WIDTH, RWKV_WIDTH), 0) // HEAD_DIM
    hc = lax.broadcasted_iota(jnp.int32, (RWKV_WIDTH, RWKV_WIDTH), 1) // HEAD_DIM
    tr = lax.broadcasted_iota(jnp.int32, (tt, tt), 0)
    tc = lax.broadcasted_iota(jnp.int32, (tt, tt), 1)
    same_chunk = (tr // ck) == (tc // ck)
    slab = lambda dt: pltpu.VMEM((nb, tt, RWKV_WIDTH), dt)
    return pl.pallas_call(
        functools.partial(_rwkv_kernel, tt=tt, ck=ck),
        grid=(b // nb, l // tt),
        in_specs=[tblk(CP_PAD), tblk(RWKV_WIDTH),
                  pl.BlockSpec((nb, 1, CP_PAD), lambda bi, ti: (bi, 0, 0)), sblk,
                  _full((1, CP_PAD)), vec, _full((LANES, RWKV_WIDTH)), vec, _full((LANES, RWKV_WIDTH)),
                  vec, vec, vec, vec, vec,
                  _full((RWKV_WIDTH, RWKV_WIDTH)), _full((tt, tt)), _full((tt, tt))],
        out_specs=[tblk(RWKV_WIDTH), sblk],
        out_shape=[jax.ShapeDtypeStruct((b, l, RWKV_WIDTH), BF16),
                   jax.ShapeDtypeStruct((b, 2, LANES, LANES), F32)],
        scratch_shapes=[pltpu.VMEM((nb, SUBLANES, CP_PAD), F32)] + [slab(BF16)] * 7 + [slab(F32)] * 3,
        compiler_params=_params(("parallel", "arbitrary")),
        name="rwkv",
    )(cp, g, prev, s0p, rp["mu"], rp["w0"], rp["w2"], rp["a0"], rp["a2"], rp["k_k"], rp["k_a"],
      rp["u"], rp["ln_w"], rp["ln_b"],
      (hl == hc).astype(BF16), (same_chunk & (tc <= tr)).astype(BF16), same_chunk.astype(BF16))


def _rwkv_params(mu, w0, w2, a0, a2, k_k, k_a, u, ln_w, ln_b):
    vec = lambda t: t.astype(F32).reshape(1, RWKV_WIDTH)
    mu_p = jnp.zeros((1, CP_PAD), F32).at[0, :RWKV_SHIFT].set(mu.astype(F32))
    w2_p = jnp.zeros((LANES, RWKV_WIDTH), F32).at[:RWKV_LORA].set(w2.astype(F32)).astype(BF16)
    a2_p = jnp.zeros((LANES, RWKV_WIDTH), F32).at[RWKV_LORA:2 * RWKV_LORA].set(a2.astype(F32)).astype(BF16)
    return dict(mu=mu_p, w0=vec(w0), w2=w2_p, a0=vec(a0), a2=a2_p, k_k=vec(k_k), k_a=vec(k_a),
                u=vec(u), ln_w=vec(ln_w), ln_b=vec(ln_b))


def _pair_state(s):
    b = s.shape[0]
    s = s.astype(F32).reshape(b, 2, 2, HEAD_DIM, HEAD_DIM)
    eye = jnp.eye(2, dtype=F32)
    return jnp.einsum("bphvk,hg->bphvgk", s, eye).reshape(b, 2, LANES, LANES)


def _unpair_state(sp):
    b = sp.shape[0]
    s = sp.reshape(b, 2, 2, HEAD_DIM, 2, HEAD_DIM)
    return jnp.stack([s[:, :, 0, :, 0, :], s[:, :, 1, :, 1, :]], axis=2).reshape(b, RWKV_HEADS, HEAD_DIM, HEAD_DIM)


def _outproj_kernel(x_ref, a_ref, b_ref, c_ref, w_ref, o_ref):
    acc = _dot(a_ref[...], w_ref[0:SB_WIDTH, :])
    acc = acc + _dot(b_ref[...], w_ref[SB_WIDTH:SB_WIDTH + SSM_WIDTH, :])
    acc = acc + _dot(c_ref[...], w_ref[SB_WIDTH + SSM_WIDTH:, :])
    o_ref[...] = x_ref[...] + acc


def _outproj(x2d, ma, mb, mc, w_out):
    n = x2d.shape[0]
    tm = min(512, n)
    row = lambda w: pl.BlockSpec((tm, w), lambda i: (i, 0))
    return pl.pallas_call(
        _outproj_kernel,
        grid=(n // tm,),
        in_specs=[row(D_MODEL), row(SB_WIDTH), row(SSM_WIDTH), row(RWKV_WIDTH), _full((D_MODEL, D_MODEL))],
        out_specs=row(D_MODEL),
        out_shape=jax.ShapeDtypeStruct((n, D_MODEL), F32),
        compiler_params=_params(("parallel",)),
        name="outproj",
    )(x2d, ma, mb, mc, w_out)


def _pad_w_in(w_in):
    w = w_in.astype(F32)
    pad = jnp.zeros((D_MODEL, CP_PAD - RWKV_SHIFT), F32)
    return jnp.concatenate([w[:, :COL_CP + RWKV_SHIFT], pad, w[:, COL_CP + RWKV_SHIFT:]], axis=1).astype(BF16)


def _layer(x, lp, past):
    b, l, _ = x.shape
    n = b * l
    x2d = x.reshape(n, D_MODEL)
    q, k, v, ga, u, gb, cp, gc = _inproj(x2d, lp["norm_w"], lp["w_in"], lp["q_w"], lp["k_w"])
    r3 = lambda t: t.reshape(b, l, t.shape[-1])
    q, k, v, ga, u, gb, cp, gc = map(r3, (q, k, v, ga, u, gb, cp, gc))

    if past is None:
        mix_a = _sb_prompt(q, k, v, ga)
        h0r = h0i = jnp.zeros((b, SSM_FLAT), F32)
        prev = jnp.zeros((b, 1, CP_PAD), F32)
        s0p = jnp.zeros((b, 2, LANES, LANES), F32)
    else:
        ck = past["k"].reshape(b, -1, SB_WIDTH)
        cv = past["v"].reshape(b, -1, SB_WIDTH)
        mix_a = _sb_sample(q, k, v, ck, cv, ga)
        h0r = past["ssm_re"].astype(F32).reshape(b, SSM_FLAT)
        h0i = past["ssm_im"].astype(F32).reshape(b, SSM_FLAT)
        prev = jnp.pad(past["shift"].astype(F32), ((0, 0), (0, CP_PAD - RWKV_SHIFT)))[:, None, :]
        s0p = _pair_state(past["wkv"])

    tm = lambda t: jnp.swapaxes(t, 0, 1)
    yb_t, hr, hi = _s5(tm(u), tm(gb), h0r, h0i, lp["s5"])
    mix_b = tm(yb_t).astype(BF16)
    mix_c, s_pair = _rwkv(cp, gc, prev, s0p, lp["rwkv"])

    x_new = _outproj(x2d, mix_a.reshape(n, SB_WIDTH), mix_b.reshape(n, SSM_WIDTH),
                     mix_c.reshape(n, RWKV_WIDTH), lp["w_out"]).reshape(b, l, D_MODEL)
    heads = lambda t: t.reshape(b, l, SB_WIDTH // HEAD_DIM, HEAD_DIM)
    state = (heads(k), heads(v), hr.reshape(b, SSM_GROUPS, SSM_STATE), hi.reshape(b, SSM_GROUPS, SSM_STATE),
             _unpair_state(s_pair), cp[:, -1, :RWKV_SHIFT])
    return x_new, state


def kernel(x_prompt, x_sample, cache_k, cache_v, state_ssm_re, state_ssm_im, state_wkv, state_shift, norm_w, w_in, q_norm_w, k_norm_w, ssm_lambda_re, ssm_lambda_im, ssm_log_dt, ssm_b_re, ssm_b_im, ssm_c_re, ssm_c_im, ssm_d, ssm_w_glu, ssm_b_glu, rwkv_mu, rwkv_w0, rwkv_w2, rwkv_a0, rwkv_a2, rwkv_k_k, rwkv_k_a, rwkv_u, rwkv_ln_w, rwkv_ln_b, w_out):
    depth = w_in.shape[0]
    xp, xs = x_prompt, x_sample
    p_states, s_states = [], []
    for i in range(depth):
        tile8 = lambda t: jnp.tile(t.astype(F32), SB_WIDTH // HEAD_DIM).reshape(1, SB_WIDTH)
        lp = dict(
            norm_w=norm_w[i].astype(F32).reshape(1, D_MODEL), w_in=_pad_w_in(w_in[i]),
            q_w=tile8(q_norm_w[i]), k_w=tile8(k_norm_w[i]),
            s5=_s5_params(ssm_lambda_re[i], ssm_lambda_im[i], ssm_log_dt[i], ssm_b_re[i], ssm_b_im[i],
                          ssm_c_re[i], ssm_c_im[i], ssm_d[i], ssm_w_glu[i], ssm_b_glu[i]),
            rwkv=_rwkv_params(rwkv_mu[i], rwkv_w0[i], rwkv_w2[i], rwkv_a0[i], rwkv_a2[i], rwkv_k_k[i],
                              rwkv_k_a[i], rwkv_u[i].reshape(-1), rwkv_ln_w[i], rwkv_ln_b[i]),
            w_out=w_out[i].astype(BF16))
        xp, st_p = _layer(xp, lp, None)
        past = dict(k=cache_k[i], v=cache_v[i], ssm_re=state_ssm_re[i], ssm_im=state_ssm_im[i],
                    wkv=state_wkv[i], shift=state_shift[i])
        xs, st_s = _layer(xs, lp, past)
        p_states.append(st_p)
        s_states.append(st_s)
    stack = lambda states, j: jnp.stack([s[j] for s in states], axis=0)
    return ((xp, xs) + tuple(stack(p_states, j) for j in range(6))
            + tuple(stack(s_states, j) for j in range(6)))
```

```python
import functools
import math

import jax
import jax.numpy as jnp
from jax import lax
from jax.experimental import pallas as pl
from jax.experimental.pallas import tpu as pltpu

F32 = jnp.float32
BF16 = jnp.bfloat16

D_MODEL = 1024
HEAD_DIM = 64
SB_WIDTH = D_MODEL // 2
SSM_WIDTH = D_MODEL // 4
SSM_GROUP = 16
SSM_GROUPS = SSM_WIDTH // SSM_GROUP
SSM_STATE = 64
SSM_FLAT = SSM_GROUPS * SSM_STATE
RWKV_WIDTH = D_MODEL // 4
RWKV_HEADS = RWKV_WIDTH // HEAD_DIM
RWKV_LORA = 32
RWKV_SHIFT = 3 * RWKV_WIDTH + 2 * RWKV_LORA
RMS_EPS = 1e-6
GN_EPS = 64e-5
DECAY_SCALE = math.exp(-0.5)

LANES = 128
SUBLANES = 8
CP_PAD = 7 * LANES
COL_Q, COL_K, COL_V, COL_GA = 0, 512, 1024, 1536
COL_U, COL_GB, COL_CP, COL_GC = 2048, 2304, 2560, 2560 + CP_PAD
D_IN_PAD = COL_GC + RWKV_WIDTH
VMEM_LIMIT = 56 * 1024 * 1024

NN = (((1,), (0,)), ((), ()))
NT = (((1,), (1,)), ((), ()))
TN = (((0,), (0,)), ((), ()))


def _dot(a, b, dims=NN):
    return lax.dot_general(a, b, dims, preferred_element_type=F32)


def _split(a):
    hi = a.astype(BF16)
    lo = (a - hi.astype(F32)).astype(BF16)
    return hi, lo


def _dot3(a, b, dims=NN):
    ah, al = _split(a)
    bh, bl = _split(b)
    return _dot(ah, bh, dims) + (_dot(ah, bl, dims) + _dot(al, bh, dims))


def _dot2l(a, b_exact, dims=NN):
    ah, al = _split(a)
    return _dot(ah, b_exact, dims) + _dot(al, b_exact, dims)


def _dot2r(a_exact, b, dims=NN):
    bh, bl = _split(b)
    return _dot(a_exact, bh, dims) + _dot(a_exact, bl, dims)


def _params(sem):
    return pltpu.CompilerParams(dimension_semantics=sem, vmem_limit_bytes=VMEM_LIMIT)


def _full(shape):
    nd = len(shape)
    return pl.BlockSpec(shape, lambda *_: (0,) * nd)


def _head_rms(t, w):
    lane = lax.broadcasted_iota(jnp.int32, (1, LANES), 1)
    lo = lane < HEAD_DIM
    outs = []
    for j in range(t.shape[1] // LANES):
        blk = t[:, j * LANES:(j + 1) * LANES]
        sq = blk * blk
        s_lo = jnp.sum(jnp.where(lo, sq, 0.0), axis=-1, keepdims=True)
        s_hi = jnp.sum(jnp.where(lo, 0.0, sq), axis=-1, keepdims=True)
        ms = jnp.where(lo, s_lo, s_hi) * (1.0 / HEAD_DIM)
        outs.append(blk * lax.rsqrt(ms + RMS_EPS) * w[:, j * LANES:(j + 1) * LANES])
    return jnp.concatenate(outs, axis=-1)


def _inproj_kernel(x_ref, nw_ref, w_ref, qw_ref, kw_ref,
                   q_o, k_o, v_o, ga_o, u_o, gb_o, cp_o, gc_o):
    x = x_ref[...]
    ms = jnp.mean(x * x, axis=-1, keepdims=True)
    h = (x * lax.rsqrt(ms + RMS_EPS) * nw_ref[...]).astype(BF16)

    def proj(a, b):
        return _dot(h, w_ref[:, a:b])

    q = _head_rms(proj(COL_Q, COL_K), qw_ref[...])
    q_o[...] = (q * (HEAD_DIM ** -0.5)).astype(BF16)
    k_o[...] = _head_rms(proj(COL_K, COL_V), kw_ref[...])
    v_o[...] = proj(COL_V, COL_GA)
    ga_o[...] = proj(COL_GA, COL_U)
    u_o[...] = proj(COL_U, COL_GB)
    gb_o[...] = proj(COL_GB, COL_CP)
    cp_o[...] = proj(COL_CP, COL_GC)
    gc_o[...] = proj(COL_GC, D_IN_PAD)


def _inproj(x2d, nw, w_pad, qw, kw):
    n = x2d.shape[0]
    tm = min(256, n)
    widths = (SB_WIDTH, SB_WIDTH, SB_WIDTH, SB_WIDTH, SSM_WIDTH, SSM_WIDTH, CP_PAD, RWKV_WIDTH)
    dtypes = (BF16,) + (F32,) * 7
    row = lambda w: pl.BlockSpec((tm, w), lambda i: (i, 0))
    return pl.pallas_call(
        _inproj_kernel,
        grid=(n // tm,),
        in_specs=[row(D_MODEL), _full((1, D_MODEL)), _full((D_MODEL, D_IN_PAD)),
                  _full((1, SB_WIDTH)), _full((1, SB_WIDTH))],
        out_specs=[row(w) for w in widths],
        out_shape=[jax.ShapeDtypeStruct((n, w), dt) for w, dt in zip(widths, dtypes)],
        compiler_params=_params(("parallel",)),
        name="inproj",
    )(x2d, nw, w_pad, qw, kw)


SB_TK = 128
SB_PAIRS = SB_WIDTH // LANES


SB_U = 2
LOG2E = math.log2(math.e)


def _sb_weights(qs, kbs, carries, tri, masks):
    tk = tri.shape[1]
    pu = [(p, u) for p in range(len(qs)) for u in range(len(kbs[p]))]
    z = {}
    for p in range(len(qs)):
        zp = _dot(qs[p], jnp.concatenate(kbs[p], axis=0), NT)
        for u in range(len(kbs[p])):
            z[(p, u)] = zp[:, u * tk:(u + 1) * tk]
    hl = {}
    for k in pu:
        sp = jnp.maximum(z[k], 0.0) + jnp.log(1.0 + jnp.exp2(jnp.abs(z[k]) * (-LOG2E)))
        if masks[k[1]] is not None:
            sp = jnp.where(masks[k[1]], sp, 0.0)
        hl[k] = jnp.concatenate(_split(sp), axis=1)
    cs = {k: _dot(hl[k], tri) for k in pu}
    weights, new_carries = [], []
    for p in range(len(qs)):
        carry, row = carries[p], []
        for u in range(len(kbs[p])):
            incl = cs[(p, u)] + carry
            a = jnp.exp2((z[(p, u)] - incl) * LOG2E)
            if masks[u] is not None:
                a = jnp.where(masks[u], a, 0.0)
            row.append(a.astype(BF16))
            carry = jnp.broadcast_to(incl[:, 0:1], incl.shape)
        weights.append(row)
        new_carries.append(carry)
    return weights, new_carries


def _sb_values(a2, vbs, head_lo):
    parts = []
    for vb in vbs:
        zero = jnp.zeros_like(vb)
        parts += [jnp.where(head_lo, vb, zero), jnp.where(head_lo, zero, vb)]
    return _dot(a2, jnp.concatenate(parts, axis=0))


def _sb_all_pairs(q, diag_kv, n_diag, past_k, past_v, n_iter, tri, qs_s, a_s, acc_s, carry_s):
    tq, tk = q.shape[0], SB_TK
    head_lo = lax.broadcasted_iota(jnp.int32, (1, LANES), 1) < HEAD_DIM
    pairs = range(SB_PAIRS)
    qs = []
    for p in pairs:
        qp = q[:, p * LANES:(p + 1) * LANES]
        zero = jnp.zeros_like(qp)
        qs.append(jnp.concatenate([jnp.where(head_lo, qp, zero), jnp.where(head_lo, zero, qp)], axis=0))
        qs_s[p] = qs[p]
    rg = tq // n_diag
    row = lax.broadcasted_iota(jnp.int32, (2 * rg, tk), 0)
    col = lax.broadcasted_iota(jnp.int32, (2 * rg, tk), 1)
    causal = col < jnp.where(row >= rg, row - rg, row)
    diag = [[diag_kv(d, p) for d in range(n_diag)] for p in pairs]
    groups = [(p, r) for p in pairs for r in range(n_diag)]
    rows_of = lambda x, r: jnp.concatenate([x[r * rg:(r + 1) * rg], x[tq + r * rg:tq + (r + 1) * rg]], axis=0)
    ws, carries = _sb_weights([rows_of(qs[p], r) for p, r in groups],
                              [[diag[p][d][0] for d in range(r, -1, -1)] for p, r in groups],
                              [jnp.zeros((2 * rg, tk), F32)] * len(groups), tri, [causal] + [None] * (n_diag - 1))
    for p in pairs:
        accs, c_lo, c_hi = [], [], []
        for r in range(n_diag):
            g = groups.index((p, r))
            a2 = jnp.concatenate([half for a in ws[g] for half in (a[:rg], a[rg:])], axis=1)
            accs.append(_sb_values(a2, [diag[p][d][1] for d in range(r, -1, -1)], head_lo))
            c_lo.append(carries[g][:rg])
            c_hi.append(carries[g][rg:])
        acc_s[p] = jnp.concatenate(accs, axis=0)
        carry_s[p] = jnp.concatenate(c_lo + c_hi, axis=0)
    a_s[...] = jnp.zeros_like(a_s)

    def values(prev_start):
        for p in pairs:
            vbs = [past_v(prev_start + (SB_U - 1 - u) * tk, p) for u in range(SB_U)]
            acc_s[p] += _sb_values(a_s[p], vbs, head_lo)

    def body(it, prev_start):
        start = pl.multiple_of((n_iter - 1 - it) * (SB_U * tk), SB_U * tk)
        values(prev_start)
        kbs = [[past_k(start + (SB_U - 1 - u) * tk, p) for u in range(SB_U)] for p in pairs]
        ws, carries = _sb_weights([qs_s[p] for p in pairs], kbs, [carry_s[p] for p in pairs], tri,
                                  [None] * SB_U)
        for p in pairs:
            for u in range(SB_U):
                a_s[p, :, (2 * u) * tk:(2 * u + 1) * tk] = ws[p][u][:tq]
                a_s[p, :, (2 * u + 1) * tk:(2 * u + 2) * tk] = ws[p][u][tq:]
            carry_s[p] = carries[p]
        return start

    values(lax.fori_loop(0, n_iter, body, 0))
    return jnp.concatenate([acc_s[p] for p in range(SB_PAIRS)], axis=1)


def _sb_tri():
    j = lax.broadcasted_iota(jnp.int32, (2 * SB_TK, SB_TK), 0) % SB_TK
    s = lax.broadcasted_iota(jnp.int32, (2 * SB_TK, SB_TK), 1)
    return (j >= s).astype(BF16)


def _sb_scratch(tq):
    return [pltpu.VMEM((SB_PAIRS, 2 * tq, LANES), BF16), pltpu.VMEM((SB_PAIRS, tq, SB_U * 2 * SB_TK), BF16),
            pltpu.VMEM((SB_PAIRS, tq, LANES), F32), pltpu.VMEM((SB_PAIRS, 2 * tq, SB_TK), F32)]


def _sb_block_reader(ref2d):
    def read(start, p):
        start = pl.multiple_of(start, SB_TK)
        return ref2d[pl.ds(start, SB_TK), p * LANES:(p + 1) * LANES].astype(BF16)
    return read


def _sb_prompt_kernel(q_ref, k_ref, v_ref, g_ref, tri_ref, o_ref, qs_s, a_s, acc_s, carry_s, *, tq):
    qi = pl.program_id(1)
    base = pl.multiple_of(qi * tq, tq)
    read_k, read_v = _sb_block_reader(k_ref.at[0]), _sb_block_reader(v_ref.at[0])

    def diag_kv(d, p):
        return read_k(base + d * SB_TK, p), read_v(base + d * SB_TK, p)

    n_iter = qi * (tq // (SB_U * SB_TK))
    y = _sb_all_pairs(q_ref[0], diag_kv, tq // SB_TK, read_k, read_v, n_iter, tri_ref[...],
                      qs_s, a_s, acc_s, carry_s)
    o_ref[0] = (y * jax.nn.silu(g_ref[0])).astype(BF16)


def _sb_prompt(q, k, v, g):
    b, l, _ = q.shape
    tq = SB_U * SB_TK
    qblk = pl.BlockSpec((1, tq, SB_WIDTH), lambda bi, qi: (bi, qi, 0))
    kvblk = pl.BlockSpec((1, l, SB_WIDTH), lambda bi, qi: (bi, 0, 0))
    return pl.pallas_call(
        functools.partial(_sb_prompt_kernel, tq=tq),
        grid=(b, l // tq),
        in_specs=[qblk, kvblk, kvblk, qblk, _full((2 * SB_TK, SB_TK))],
        out_specs=qblk,
        out_shape=jax.ShapeDtypeStruct((b, l, SB_WIDTH), BF16),
        scratch_shapes=_sb_scratch(tq),
        compiler_params=_params(("parallel", "arbitrary")),
        name="sb_prompt",
    )(q, k, v, g, _sb_tri())


def _sb_sample_kernel(q_ref, k_ref, v_ref, ck_ref, cv_ref, g_ref, tri_ref, o_ref, qs_s, a_s, acc_s, carry_s,
                      *, n_iter):
    tq = q_ref.shape[1]

    def diag_kv(d, p):
        ls = slice(p * LANES, (p + 1) * LANES)
        pad = jnp.zeros((SB_TK - tq, LANES), BF16)
        return (jnp.concatenate([k_ref[0, :, ls].astype(BF16), pad], axis=0),
                jnp.concatenate([v_ref[0, :, ls].astype(BF16), pad], axis=0))

    y = _sb_all_pairs(q_ref[0], diag_kv, 1, _sb_block_reader(ck_ref.at[0, 0]), _sb_block_reader(cv_ref.at[0, 0]),
                      n_iter, tri_ref[...], qs_s, a_s, acc_s, carry_s)
    o_ref[0] = (y * jax.nn.silu(g_ref[0])).astype(BF16)


def _sb_sample(q, k, v, ck, cv, layer, g):
    b, l, _ = q.shape
    past = ck.shape[2]
    qblk = pl.BlockSpec((1, l, SB_WIDTH), lambda bi: (bi, 0, 0))
    cblk = pl.BlockSpec((1, 1, past, SB_WIDTH), lambda bi: (layer, bi, 0, 0))
    return pl.pallas_call(
        functools.partial(_sb_sample_kernel, n_iter=past // (SB_U * SB_TK)),
        grid=(b,),
        in_specs=[qblk, qblk, qblk, cblk, cblk, qblk, _full((2 * SB_TK, SB_TK))],
        out_specs=qblk,
        out_shape=jax.ShapeDtypeStruct((b, l, SB_WIDTH), BF16),
        scratch_shapes=_sb_scratch(l),
        compiler_params=_params(("parallel",)),
        name="sb_sample",
    )(q, k, v, ck, cv, g, _sb_tri())


def _s5_kernel(u_ref, g_ref, h0r_ref, h0i_ref, ar_ref, ai_ref, bh_ref, bl_ref, c_ref, d_ref,
               wg_ref, bg_ref, o_ref, hr_o, hi_o, bu_s, h_s, ut_s, gt_s, ot_s, *, tt):
    ti = pl.program_id(1)
    rows = tt * SUBLANES
    slabs = SSM_WIDTH // LANES

    @pl.when(ti == 0)
    def _():
        hr_o[...] = h0r_ref[...]
        hi_o[...] = h0i_ref[...]

    def time_major(ref, slab_s):
        for s in range(SUBLANES):
            for j in range(slabs):
                slab_s[j, pl.ds(s, tt, stride=SUBLANES), :] = ref[s, :, j * LANES:(j + 1) * LANES]
        return jnp.concatenate([slab_s[j] for j in range(slabs)], axis=1)

    u = time_major(u_ref, ut_s)
    uh, ul = _split(u)
    bu_s[...] = _dot(uh, bh_ref[...]) + (_dot(uh, bl_ref[...]) + _dot(ul, bh_ref[...]))

    ar = jnp.broadcast_to(ar_ref[...], (SUBLANES, SSM_FLAT))
    ai = jnp.broadcast_to(ai_ref[...], (SUBLANES, SSM_FLAT))

    def step(t, carry):
        hr, hi = carry
        r0 = pl.multiple_of(t * SUBLANES, SUBLANES)
        br = bu_s[pl.ds(r0, SUBLANES), 0:SSM_FLAT]
        bi = bu_s[pl.ds(r0, SUBLANES), SSM_FLAT:2 * SSM_FLAT]
        nr = ar * hr - ai * hi + br
        ni = ar * hi + ai * hr + bi
        h_s[pl.ds(r0, SUBLANES), 0:SSM_FLAT] = nr
        h_s[pl.ds(r0, SUBLANES), SSM_FLAT:2 * SSM_FLAT] = ni
        return nr, ni

    hr, hi = lax.fori_loop(0, tt, step, (hr_o[...], hi_o[...]))
    hr_o[...] = hr
    hi_o[...] = hi

    y = _dot(h_s[...].astype(BF16), c_ref[...]) + d_ref[...] * u
    y = jax.nn.gelu(y)
    z = _dot(y.astype(BF16), wg_ref[...]) + bg_ref[...]
    out = z[:, :SSM_WIDTH] * jax.nn.sigmoid(z[:, SSM_WIDTH:])
    out = out * jax.nn.silu(time_major(g_ref, gt_s))
    for j in range(slabs):
        ot_s[j] = out[:, j * LANES:(j + 1) * LANES]
    for s in range(SUBLANES):
        for j in range(slabs):
            o_ref[s, :, j * LANES:(j + 1) * LANES] = ot_s[j, pl.ds(s, tt, stride=SUBLANES), :].astype(BF16)


def _s5(u, g, h0r, h0i, sp):
    b, l, _ = u.shape
    tt = min(64, l)
    rows = tt * SUBLANES
    ublk = pl.BlockSpec((SUBLANES, tt, SSM_WIDTH), lambda bi, ti: (bi, ti, 0))
    sblk = pl.BlockSpec((SUBLANES, SSM_FLAT), lambda bi, ti: (bi, 0))
    slab = pltpu.VMEM((SSM_WIDTH // LANES, rows, LANES), F32)
    return pl.pallas_call(
        functools.partial(_s5_kernel, tt=tt),
        grid=(b // SUBLANES, l // tt),
        in_specs=[ublk, ublk, sblk, sblk,
                  _full((1, SSM_FLAT)), _full((1, SSM_FLAT)),
                  _full((SSM_WIDTH, 2 * SSM_FLAT)), _full((SSM_WIDTH, 2 * SSM_FLAT)),
                  _full((2 * SSM_FLAT, SSM_WIDTH)), _full((1, SSM_WIDTH)),
                  _full((SSM_WIDTH, 2 * SSM_WIDTH)), _full((1, 2 * SSM_WIDTH))],
        out_specs=[ublk, sblk, sblk],
        out_shape=[jax.ShapeDtypeStruct((b, l, SSM_WIDTH), BF16),
                   jax.ShapeDtypeStruct((b, SSM_FLAT), F32),
                   jax.ShapeDtypeStruct((b, SSM_FLAT), F32)],
        scratch_shapes=[pltpu.VMEM((rows, 2 * SSM_FLAT), F32), pltpu.VMEM((rows, 2 * SSM_FLAT), F32),
                        slab, slab, slab],
        compiler_params=_params(("parallel", "arbitrary")),
        name="s5",
    )(u, g, h0r, h0i, sp["ar"], sp["ai"], sp["b_hi"], sp["b_lo"], sp["c"], sp["d"], sp["w_glu"], sp["b_glu"])


def _s5_params(lam_re, lam_im, log_dt, b_re, b_im, c_re, c_im, d, w_glu, b_glu):
    dt = jnp.exp(log_dt.astype(F32))[:, None]
    lr = jnp.minimum(lam_re.astype(F32), -1e-4)
    li = lam_im.astype(F32)
    er = jnp.exp(lr * dt)
    ar, ai = er * jnp.cos(li * dt), er * jnp.sin(li * dt)
    den = lr * lr + li * li
    fr = ((ar - 1.0) * lr + ai * li) / den
    fi = (ai * lr - (ar - 1.0) * li) / den
    br, bi = b_re.astype(F32), b_im.astype(F32)
    bbr = fr[..., None] * br - fi[..., None] * bi
    bbi = fr[..., None] * bi + fi[..., None] * br
    eye = jnp.eye(SSM_GROUPS, dtype=F32)
    bd_in = lambda m: jnp.einsum("gpc,gh->gchp", m, eye).reshape(SSM_WIDTH, SSM_FLAT)
    bd_out = lambda m: jnp.einsum("gcp,gh->gphc", m, eye).reshape(SSM_FLAT, SSM_WIDTH)
    b_all = jnp.concatenate([bd_in(bbr), bd_in(bbi)], axis=1)
    b_hi = b_all.astype(BF16)
    b_lo = (b_all - b_hi.astype(F32)).astype(BF16)
    c_all = jnp.concatenate([bd_out(c_re.astype(F32)), -bd_out(c_im.astype(F32))], axis=0)
    return dict(ar=ar.reshape(1, SSM_FLAT), ai=ai.reshape(1, SSM_FLAT), b_hi=b_hi, b_lo=b_lo,
                c=c_all.astype(BF16), d=d.astype(F32).reshape(1, SSM_WIDTH),
                w_glu=w_glu.astype(BF16), b_glu=b_glu.astype(F32).reshape(1, 2 * SSM_WIDTH))


RWKV_CK = 32
RWKV_NB = 8


def _rwkv_kernel(cp_ref, g_ref, prev_ref, s0_ref, mu_ref, w0_ref, w2_ref, a0_ref, a2_ref, kk_ref, ka_ref,
                 ub_ref, lnw_ref, lnb_ref, ho_ref, tri_ref, ones_ref, o_ref, s_o, prev_s,
                 khw_s, rw_s, ktl_s, btl_s, v_s, kht_s, bht_s, wch_s, bonus_s, y_s, *, tt, ck):
    ti = pl.program_id(1)

    @pl.when(ti == 0)
    def _():
        s_o[...] = s0_ref[...]
        prev_s[...] = jnp.broadcast_to(prev_ref[...], prev_s.shape)

    w = RWKV_WIDTH
    nb_all = cp_ref.shape[0]
    head_ones, chunk_tri, chunk_ones = ho_ref[...], tri_ref[...], ones_ref[...]

    def prep(nb, carry):
        cp = cp_ref[nb]
        trow = lax.broadcasted_iota(jnp.int32, (tt, 1), 0)
        shifted = jnp.where(trow == 0, prev_s[nb, 0:1, :], pltpu.roll(cp, 1, 0))
        prev_s[nb] = jnp.broadcast_to(cp[tt - 1:tt, :], (SUBLANES, CP_PAD))
        xc = cp + mu_ref[...] * (shifted - cp)
        r, k, v, lora = xc[:, 0:w], xc[:, w:2 * w], xc[:, 2 * w:3 * w], xc[:, 3 * w:]
        logw = -DECAY_SCALE * jax.nn.sigmoid(w0_ref[...] + _dot(jnp.tanh(lora).astype(BF16), w2_ref[...]))
        a = jax.nn.sigmoid(a0_ref[...] + _dot(lora.astype(BF16), a2_ref[...]))
        cum = _dot2r(chunk_tri, logw)
        tot = _dot2r(chunk_ones, logw)
        kkv = k * kk_ref[...]
        kh = kkv * lax.rsqrt(_dot2l(kkv * kkv, head_ones) + 1e-12)
        kt = k * (1.0 + (a - 1.0) * ka_ref[...])
        bvec = a * kh
        w_inv = jnp.exp(-cum)
        w_end = jnp.exp(tot - cum)
        khw_s[nb] = (kh * jnp.exp(cum - logw)).astype(BF16)
        rw_s[nb] = (r * jnp.exp(cum)).astype(BF16)
        ktl_s[nb] = (kt * w_inv).astype(BF16)
        btl_s[nb] = (bvec * w_inv).astype(BF16)
        v_s[nb] = v.astype(BF16)
        kht_s[nb] = (kt * w_end).astype(BF16)
        bht_s[nb] = (bvec * w_end).astype(BF16)
        wch_s[nb] = jnp.exp(tot)
        bonus_s[nb] = _dot2l(r * kt * ub_ref[...], head_ones) * v
        return carry

    lax.fori_loop(0, nb_all, prep, 0)

    lane = lax.broadcasted_iota(jnp.int32, (1, LANES), 1)
    m_lo = lane < HEAD_DIM
    ri = lax.broadcasted_iota(jnp.int32, (2 * ck, 2 * ck), 0)
    cj = lax.broadcasted_iota(jnp.int32, (2 * ck, 2 * ck), 1)
    same_head = (ri >= ck) == (cj >= ck)
    strict = same_head & (cj < ri)
    lower = same_head & (cj <= ri)
    eye = (ri == cj).astype(F32)
    rr = lax.broadcasted_iota(jnp.int32, (2 * ck, LANES), 0)
    rl = lax.broadcasted_iota(jnp.int32, (2 * ck, LANES), 1)
    own_lanes = (rr >= ck) == (rl >= HEAD_DIM)
    sr = lax.broadcasted_iota(jnp.int32, (LANES, LANES), 0) // HEAD_DIM
    sc = lax.broadcasted_iota(jnp.int32, (LANES, LANES), 1) // HEAD_DIM
    blockdiag = sr == sc
    dot1 = lambda x, y: _dot(x.astype(BF16), y.astype(BF16))

    cat = lambda xs: jnp.concatenate(xs, axis=0)

    def chunk_step(c, carry):
        r0 = pl.multiple_of(c * ck, ck)
        rows = pl.ds(r0, ck)
        chains = [(nb, hp) for nb in range(nb_all) for hp in range(w // LANES)]
        lanes = lambda hp: slice(hp * LANES, (hp + 1) * LANES)
        rd = lambda ref: [ref[nb, rows, lanes(hp)] for nb, hp in chains]
        khw, rw, ktl, btl, v = rd(khw_s), rd(rw_s), rd(ktl_s), rd(btl_s), rd(v_s)
        z = jnp.zeros_like(khw[0])
        l4 = [cat([jnp.where(m_lo, a, z), jnp.where(m_lo, z, a), jnp.where(m_lo, b, z), jnp.where(m_lo, z, b)])
              for a, b in zip(khw, rw)]
        scb = [_dot(a, cat([b, b]), NT) for a, b in zip(l4, btl)]
        sck = [_dot(a, cat([b, b]), NT) for a, b in zip(l4, ktl)]
        nmat = [jnp.where(strict, a[:2 * ck], 0.0) for a in scb]
        arb = [jnp.where(lower, a[2 * ck:], 0.0) for a in scb]
        akk = [jnp.where(strict, a[:2 * ck], 0.0) for a in sck]
        ark = [jnp.where(lower, a[2 * ck:], 0.0) for a in sck]
        inv = [eye - a for a in nmat]
        pw = nmat
        steps = 1
        while 2 * steps < ck:
            pw = [dot1(a, a) for a in pw]
            inv = [dot1(a, eye + b) for a, b in zip(inv, pw)]
            steps *= 2
        s_pair = [s_o[nb, hp] for nb, hp in chains]
        ks = [_dot(cat([a, b]), s.astype(BF16), NT) for a, b, s in zip(khw, rw, s_pair)]
        av = [dot1(cat([a, b]), cat([x, x])) for a, b, x in zip(akk, ark, v)]
        sa2 = [jnp.where(own_lanes, dot1(t, cat([k[:ck], k[:ck]]) + a[:2 * ck]), 0.0)
               for t, k, a in zip(inv, ks, av)]
        y2 = [jnp.where(own_lanes, a[2 * ck:] - dot1(b, s), 0.0) for a, b, s in zip(av, arb, sa2)]
        for (nb, hp), k, y in zip(chains, ks, y2):
            y_s[nb, rows, lanes(hp)] = k[ck:] + y[:ck] + y[ck:]
        kht, bht = rd(kht_s), rd(bht_s)
        upd = [_dot(cat([x, (-(s[:ck] + s[ck:])).astype(BF16)]), cat([a, b]), TN)
               for x, s, a, b in zip(v, sa2, kht, bht)]
        for (nb, hp), s, u in zip(chains, s_pair, upd):
            s_o[nb, hp] = s * wch_s[nb, pl.ds(r0, 1), lanes(hp)] + jnp.where(blockdiag, u, 0.0)
        return carry

    lax.fori_loop(0, tt // ck, chunk_step, 0)

    def finish(nb, carry):
        ys = y_s[nb]
        mean = _dot2l(ys, head_ones) * (1.0 / HEAD_DIM)
        cen = ys - mean
        var = _dot2l(cen * cen, head_ones) * (1.0 / HEAD_DIM)
        y = cen * lax.rsqrt(var + GN_EPS) * lnw_ref[...] + lnb_ref[...] + bonus_s[nb]
        o_ref[nb] = (y * jax.nn.silu(g_ref[nb])).astype(BF16)
        return carry

    lax.fori_loop(0, nb_all, finish, 0)


def _rwkv(cp, g, prev, s0p, rp):
    b, l, _ = cp.shape
    tt = min(256, l)
    ck = min(RWKV_CK, tt)
    nb = min(RWKV_NB, b)
    tblk = lambda w: pl.BlockSpec((nb, tt, w), lambda bi, ti: (bi, ti, 0))
    sblk = pl.BlockSpec((nb, 2, LANES, LANES), lambda bi, ti: (bi, 0, 0, 0))
    vec = _full((1, RWKV_WIDTH))
    hl = lax.broadcasted_iota(jnp.int32, (RWKV_WIDTH, RWKV_WIDTH), 0) // HEAD_DIM
    hc = lax.broadcasted_iota(jnp.int32, (RWKV_WIDTH, RWKV_WIDTH), 1) // HEAD_DIM
    tr = lax.broadcasted_iota(jnp.int32, (tt, tt), 0)
    tc = lax.broadcasted_iota(jnp.int32, (tt, tt), 1)
    same_chunk = (tr // ck) == (tc // ck)
    slab = lambda dt: pltpu.VMEM((nb, tt, RWKV_WIDTH), dt)
    return pl.pallas_call(
        functools.partial(_rwkv_kernel, tt=tt, ck=ck),
        grid=(b // nb, l // tt),
        in_specs=[tblk(CP_PAD), tblk(RWKV_WIDTH),
                  pl.BlockSpec((nb, 1, CP_PAD), lambda bi, ti: (bi, 0, 0)), sblk,
                  _full((1, CP_PAD)), vec, _full((LANES, RWKV_WIDTH)), vec, _full((LANES, RWKV_WIDTH)),
                  vec, vec, vec, vec, vec,
                  _full((RWKV_WIDTH, RWKV_WIDTH)), _full((tt, tt)), _full((tt, tt))],
        out_specs=[tblk(RWKV_WIDTH), sblk],
        out_shape=[jax.ShapeDtypeStruct((b, l, RWKV_WIDTH), BF16),
                   jax.ShapeDtypeStruct((b, 2, LANES, LANES), F32)],
        scratch_shapes=[pltpu.VMEM((nb, SUBLANES, CP_PAD), F32)] + [slab(BF16)] * 7 + [slab(F32)] * 3,
        compiler_params=_params(("parallel", "arbitrary")),
        name="rwkv",
    )(cp, g, prev, s0p, rp["mu"], rp["w0"], rp["w2"], rp["a0"], rp["a2"], rp["k_k"], rp["k_a"],
      rp["u"], rp["ln_w"], rp["ln_b"],
      (hl == hc).astype(BF16), (same_chunk & (tc <= tr)).astype(BF16), same_chunk.astype(BF16))


def _rwkv_params(mu, w0, w2, a0, a2, k_k, k_a, u, ln_w, ln_b):
    vec = lambda t: t.astype(F32).reshape(1, RWKV_WIDTH)
    mu_p = jnp.zeros((1, CP_PAD), F32).at[0, :RWKV_SHIFT].set(mu.astype(F32))
    w2_p = jnp.zeros((LANES, RWKV_WIDTH), F32).at[:RWKV_LORA].set(w2.astype(F32)).astype(BF16)
    a2_p = jnp.zeros((LANES, RWKV_WIDTH), F32).at[RWKV_LORA:2 * RWKV_LORA].set(a2.astype(F32)).astype(BF16)
    return dict(mu=mu_p, w0=vec(w0), w2=w2_p, a0=vec(a0), a2=a2_p, k_k=vec(k_k), k_a=vec(k_a),
                u=vec(u), ln_w=vec(ln_w), ln_b=vec(ln_b))


def _pair_state(s):
    b = s.shape[0]
    s = s.astype(F32).reshape(b, 2, 2, HEAD_DIM, HEAD_DIM)
    eye = jnp.eye(2, dtype=F32)
    return jnp.einsum("bphvk,hg->bphvgk", s, eye).reshape(b, 2, LANES, LANES)


def _unpair_state(sp):
    b = sp.shape[0]
    s = sp.reshape(b, 2, 2, HEAD_DIM, 2, HEAD_DIM)
    return jnp.stack([s[:, :, 0, :, 0, :], s[:, :, 1, :, 1, :]], axis=2).reshape(b, RWKV_HEADS, HEAD_DIM, HEAD_DIM)


def _outproj_kernel(x_ref, a_ref, b_ref, c_ref, w_ref, o_ref):
    acc = _dot(a_ref[...], w_ref[0:SB_WIDTH, :])
    acc = acc + _dot(b_ref[...], w_ref[SB_WIDTH:SB_WIDTH + SSM_WIDTH, :])
    acc = acc + _dot(c_ref[...], w_ref[SB_WIDTH + SSM_WIDTH:, :])
    o_ref[...] = x_ref[...] + acc


def _outproj(x2d, ma, mb, mc, w_out):
    n = x2d.shape[0]
    tm = min(512, n)
    row = lambda w: pl.BlockSpec((tm, w), lambda i: (i, 0))
    return pl.pallas_call(
        _outproj_kernel,
        grid=(n // tm,),
        in_specs=[row(D_MODEL), row(SB_WIDTH), row(SSM_WIDTH), row(RWKV_WIDTH), _full((D_MODEL, D_MODEL))],
        out_specs=row(D_MODEL),
        out_shape=jax.ShapeDtypeStruct((n, D_MODEL), F32),
        compiler_params=_params(("parallel",)),
        name="outproj",
    )(x2d, ma, mb, mc, w_out)


def _pad_w_in(w_in):
    w = w_in.astype(F32)
    pad = jnp.zeros((D_MODEL, CP_PAD - RWKV_SHIFT), F32)
    return jnp.concatenate([w[:, :COL_CP + RWKV_SHIFT], pad, w[:, COL_CP + RWKV_SHIFT:]], axis=1).astype(BF16)


def _layer(x, lp, past):
    b, l, _ = x.shape
    n = b * l
    x2d = x.reshape(n, D_MODEL)
    q, k, v, ga, u, gb, cp, gc = _inproj(x2d, lp["norm_w"], lp["w_in"], lp["q_w"], lp["k_w"])
    r3 = lambda t: t.reshape(b, l, t.shape[-1])
    q, k, v, ga, u, gb, cp, gc = map(r3, (q, k, v, ga, u, gb, cp, gc))

    if past is None:
        mix_a = _sb_prompt(q, k, v, ga)
        h0r = h0i = jnp.zeros((b, SSM_FLAT), F32)
        prev = jnp.zeros((b, 1, CP_PAD), F32)
        s0p = jnp.zeros((b, 2, LANES, LANES), F32)
    else:
        mix_a = _sb_sample(q, k, v, past["k"], past["v"], past["layer"], ga)
        h0r = past["ssm_re"].astype(F32).reshape(b, SSM_FLAT)
        h0i = past["ssm_im"].astype(F32).reshape(b, SSM_FLAT)
        prev = jnp.pad(past["shift"].astype(F32), ((0, 0), (0, CP_PAD - RWKV_SHIFT)))[:, None, :]
        s0p = _pair_state(past["wkv"])

    mix_b, hr, hi = _s5(u, gb, h0r, h0i, lp["s5"])
    mix_c, s_pair = _rwkv(cp, gc, prev, s0p, lp["rwkv"])

    x_new = _outproj(x2d, mix_a.reshape(n, SB_WIDTH), mix_b.reshape(n, SSM_WIDTH),
                     mix_c.reshape(n, RWKV_WIDTH), lp["w_out"]).reshape(b, l, D_MODEL)
    heads = lambda t: t.reshape(b, l, SB_WIDTH // HEAD_DIM, HEAD_DIM)
    state = (heads(k), heads(v), hr.reshape(b, SSM_GROUPS, SSM_STATE), hi.reshape(b, SSM_GROUPS, SSM_STATE),
             _unpair_state(s_pair), cp[:, -1, :RWKV_SHIFT])
    return x_new, state


def kernel(x_prompt, x_sample, cache_k, cache_v, state_ssm_re, state_ssm_im, state_wkv, state_shift, norm_w, w_in, q_norm_w, k_norm_w, ssm_lambda_re, ssm_lambda_im, ssm_log_dt, ssm_b_re, ssm_b_im, ssm_c_re, ssm_c_im, ssm_d, ssm_w_glu, ssm_b_glu, rwkv_mu, rwkv_w0, rwkv_w2, rwkv_a0, rwkv_a2, rwkv_k_k, rwkv_k_a, rwkv_u, rwkv_ln_w, rwkv_ln_b, w_out):
    depth = w_in.shape[0]
    xp, xs = x_prompt, x_sample
    ck_all = cache_k.reshape(cache_k.shape[:3] + (SB_WIDTH,))
    cv_all = cache_v.reshape(cache_v.shape[:3] + (SB_WIDTH,))
    p_states, s_states = [], []
    for i in range(depth):
        tile8 = lambda t: jnp.tile(t.astype(F32), SB_WIDTH // HEAD_DIM).reshape(1, SB_WIDTH)
        lp = dict(
            norm_w=norm_w[i].astype(F32).reshape(1, D_MODEL), w_in=_pad_w_in(w_in[i]),
            q_w=tile8(q_norm_w[i]), k_w=tile8(k_norm_w[i]),
            s5=_s5_params(ssm_lambda_re[i], ssm_lambda_im[i], ssm_log_dt[i], ssm_b_re[i], ssm_b_im[i],
                          ssm_c_re[i], ssm_c_im[i], ssm_d[i], ssm_w_glu[i], ssm_b_glu[i]),
            rwkv=_rwkv_params(rwkv_mu[i], rwkv_w0[i], rwkv_w2[i], rwkv_a0[i], rwkv_a2[i], rwkv_k_k[i],
                              rwkv_k_a[i], rwkv_u[i].reshape(-1), rwkv_ln_w[i], rwkv_ln_b[i]),
            w_out=w_out[i].astype(BF16))
        xp, st_p = _layer(xp, lp, None)
        past = dict(k=ck_all, v=cv_all, layer=i,
                    ssm_re=state_ssm_re[i], ssm_im=state_ssm_im[i], wkv=state_wkv[i], shift=state_shift[i])
        xs, st_s = _layer(xs, lp, past)
        p_states.append(st_p)
        s_states.append(st_s)
    stack = lambda states, j: jnp.stack([s[j] for s in states], axis=0)
    return ((xp, xs) + tuple(stack(p_states, j) for j in range(6))
            + tuple(stack(s_states, j) for j in range(6)))
```

```python
import functools
import math

import jax
import jax.numpy as jnp
from jax import lax
from jax.experimental import pallas as pl
from jax.experimental.pallas import tpu as pltpu

F32 = jnp.float32
BF16 = jnp.bfloat16

D_MODEL = 1024
HEAD_DIM = 64
SB_WIDTH = D_MODEL // 2
SSM_WIDTH = D_MODEL // 4
SSM_GROUP = 16
SSM_GROUPS = SSM_WIDTH // SSM_GROUP
SSM_STATE = 64
SSM_FLAT = SSM_GROUPS * SSM_STATE
RWKV_WIDTH = D_MODEL // 4
RWKV_HEADS = RWKV_WIDTH // HEAD_DIM
RWKV_LORA = 32
RWKV_SHIFT = 3 * RWKV_WIDTH + 2 * RWKV_LORA
RMS_EPS = 1e-6
GN_EPS = 64e-5
DECAY_SCALE = math.exp(-0.5)

LANES = 128
SUBLANES = 8
CP_PAD = 7 * LANES
COL_Q, COL_K, COL_V, COL_GA = 0, 512, 1024, 1536
COL_U, COL_GB, COL_CP, COL_GC = 2048, 2304, 2560, 2560 + CP_PAD
D_IN_PAD = COL_GC + RWKV_WIDTH
VMEM_LIMIT = 56 * 1024 * 1024

NN = (((1,), (0,)), ((), ()))
NT = (((1,), (1,)), ((), ()))
TN = (((0,), (0,)), ((), ()))


def _dot(a, b, dims=NN):
    return lax.dot_general(a, b, dims, preferred_element_type=F32)


def _split(a):
    hi = a.astype(BF16)
    lo = (a - hi.astype(F32)).astype(BF16)
    return hi, lo


def _dot3(a, b, dims=NN):
    ah, al = _split(a)
    bh, bl = _split(b)
    return _dot(ah, bh, dims) + (_dot(ah, bl, dims) + _dot(al, bh, dims))


def _dot2l(a, b_exact, dims=NN):
    ah, al = _split(a)
    return _dot(ah, b_exact, dims) + _dot(al, b_exact, dims)


def _dot2r(a_exact, b, dims=NN):
    bh, bl = _split(b)
    return _dot(a_exact, bh, dims) + _dot(a_exact, bl, dims)


def _params(sem):
    return pltpu.CompilerParams(dimension_semantics=sem, vmem_limit_bytes=VMEM_LIMIT)


def _full(shape):
    nd = len(shape)
    return pl.BlockSpec(shape, lambda *_: (0,) * nd)


def _head_rms(t, w):
    lane = lax.broadcasted_iota(jnp.int32, (1, LANES), 1)
    lo = lane < HEAD_DIM
    outs = []
    for j in range(t.shape[1] // LANES):
        blk = t[:, j * LANES:(j + 1) * LANES]
        sq = blk * blk
        s_lo = jnp.sum(jnp.where(lo, sq, 0.0), axis=-1, keepdims=True)
        s_hi = jnp.sum(jnp.where(lo, 0.0, sq), axis=-1, keepdims=True)
        ms = jnp.where(lo, s_lo, s_hi) * (1.0 / HEAD_DIM)
        outs.append(blk * lax.rsqrt(ms + RMS_EPS) * w[:, j * LANES:(j + 1) * LANES])
    return jnp.concatenate(outs, axis=-1)


def _inproj_kernel(x_ref, nw_ref, w_ref, qw_ref, kw_ref,
                   q_o, k_o, v_o, ga_o, u_o, gb_o, cp_o, gc_o):
    x = x_ref[...]
    ms = jnp.mean(x * x, axis=-1, keepdims=True)
    h = (x * lax.rsqrt(ms + RMS_EPS) * nw_ref[...]).astype(BF16)

    def proj(a, b):
        return _dot(h, w_ref[:, a:b])

    q = _head_rms(proj(COL_Q, COL_K), qw_ref[...])
    q_o[...] = (q * (HEAD_DIM ** -0.5)).astype(BF16)
    k_o[...] = _head_rms(proj(COL_K, COL_V), kw_ref[...])
    v_o[...] = proj(COL_V, COL_GA)
    ga_o[...] = proj(COL_GA, COL_U)
    u_o[...] = proj(COL_U, COL_GB)
    gb_o[...] = proj(COL_GB, COL_CP)
    cp_o[...] = proj(COL_CP, COL_GC)
    gc_o[...] = proj(COL_GC, D_IN_PAD)


def _inproj(x2d, nw, w_pad, qw, kw):
    n = x2d.shape[0]
    tm = min(256, n)
    widths = (SB_WIDTH, SB_WIDTH, SB_WIDTH, SB_WIDTH, SSM_WIDTH, SSM_WIDTH, CP_PAD, RWKV_WIDTH)
    dtypes = (BF16,) + (F32,) * 7
    row = lambda w: pl.BlockSpec((tm, w), lambda i: (i, 0))
    return pl.pallas_call(
        _inproj_kernel,
        grid=(n // tm,),
        in_specs=[row(D_MODEL), _full((1, D_MODEL)), _full((D_MODEL, D_IN_PAD)),
                  _full((1, SB_WIDTH)), _full((1, SB_WIDTH))],
        out_specs=[row(w) for w in widths],
        out_shape=[jax.ShapeDtypeStruct((n, w), dt) for w, dt in zip(widths, dtypes)],
        compiler_params=_params(("parallel",)),
        name="inproj",
    )(x2d, nw, w_pad, qw, kw)


SB_TK = 128
SB_PAIRS = SB_WIDTH // LANES


SB_U = 2
LOG2E = math.log2(math.e)


def _sb_weights(qs, kbs, carries, tri, masks):
    tk = tri.shape[1]
    pu = [(p, u) for p in range(len(qs)) for u in range(len(kbs[p]))]
    z = {}
    for p in range(len(qs)):
        zp = _dot(qs[p], jnp.concatenate(kbs[p], axis=0), NT)
        for u in range(len(kbs[p])):
            z[(p, u)] = zp[:, u * tk:(u + 1) * tk]
    hl = {}
    for k in pu:
        sp = jnp.maximum(z[k], 0.0) + jnp.log(1.0 + jnp.exp2(jnp.abs(z[k]) * (-LOG2E)))
        if masks[k[1]] is not None:
            sp = jnp.where(masks[k[1]], sp, 0.0)
        hl[k] = jnp.concatenate(_split(sp), axis=1)
    cs = {k: _dot(hl[k], tri) for k in pu}
    weights, new_carries = [], []
    for p in range(len(qs)):
        carry, row = carries[p], []
        for u in range(len(kbs[p])):
            incl = cs[(p, u)] + carry
            a = jnp.exp2((z[(p, u)] - incl) * LOG2E)
            if masks[u] is not None:
                a = jnp.where(masks[u], a, 0.0)
            row.append(a.astype(BF16))
            carry = jnp.broadcast_to(incl[:, 0:1], incl.shape)
        weights.append(row)
        new_carries.append(carry)
    return weights, new_carries


def _sb_values(a2, vbs, head_lo):
    parts = []
    for vb in vbs:
        zero = jnp.zeros_like(vb)
        parts += [jnp.where(head_lo, vb, zero), jnp.where(head_lo, zero, vb)]
    return _dot(a2, jnp.concatenate(parts, axis=0))


def _sb_all_pairs(q, diag_kv, n_diag, past_k, past_v, n_iter, tri, qs_s, a_s, acc_s, carry_s):
    tq, tk = q.shape[0], SB_TK
    head_lo = lax.broadcasted_iota(jnp.int32, (1, LANES), 1) < HEAD_DIM
    pairs = range(SB_PAIRS)
    qs = []
    for p in pairs:
        qp = q[:, p * LANES:(p + 1) * LANES]
        zero = jnp.zeros_like(qp)
        qs.append(jnp.concatenate([jnp.where(head_lo, qp, zero), jnp.where(head_lo, zero, qp)], axis=0))
        qs_s[p] = qs[p]
    rg = tq // n_diag
    row = lax.broadcasted_iota(jnp.int32, (2 * rg, tk), 0)
    col = lax.broadcasted_iota(jnp.int32, (2 * rg, tk), 1)
    causal = col < jnp.where(row >= rg, row - rg, row)
    diag = [[diag_kv(d, p) for d in range(n_diag)] for p in pairs]
    groups = [(p, r) for p in pairs for r in range(n_diag)]
    rows_of = lambda x, r: jnp.concatenate([x[r * rg:(r + 1) * rg], x[tq + r * rg:tq + (r + 1) * rg]], axis=0)
    ws, carries = _sb_weights([rows_of(qs[p], r) for p, r in groups],
                              [[diag[p][d][0] for d in range(r, -1, -1)] for p, r in groups],
                              [jnp.zeros((2 * rg, tk), F32)] * len(groups), tri, [causal] + [None] * (n_diag - 1))
    for p in pairs:
        accs, c_lo, c_hi = [], [], []
        for r in range(n_diag):
            g = groups.index((p, r))
            a2 = jnp.concatenate([half for a in ws[g] for half in (a[:rg], a[rg:])], axis=1)
            accs.append(_sb_values(a2, [diag[p][d][1] for d in range(r, -1, -1)], head_lo))
            c_lo.append(carries[g][:rg])
            c_hi.append(carries[g][rg:])
        acc_s[p] = jnp.concatenate(accs, axis=0)
        carry_s[p] = jnp.concatenate(c_lo + c_hi, axis=0)
    a_s[...] = jnp.zeros_like(a_s)

    def values(prev_start):
        for p in pairs:
            vbs = [past_v(prev_start + (SB_U - 1 - u) * tk, p) for u in range(SB_U)]
            acc_s[p] += _sb_values(a_s[p], vbs, head_lo)

    def body(it, prev_start):
        start = pl.multiple_of((n_iter - 1 - it) * (SB_U * tk), SB_U * tk)
        values(prev_start)
        kbs = [[past_k(start + (SB_U - 1 - u) * tk, p) for u in range(SB_U)] for p in pairs]
        ws, carries = _sb_weights([qs_s[p] for p in pairs], kbs, [carry_s[p] for p in pairs], tri,
                                  [None] * SB_U)
        for p in pairs:
            for u in range(SB_U):
                a_s[p, :, (2 * u) * tk:(2 * u + 1) * tk] = ws[p][u][:tq]
                a_s[p, :, (2 * u + 1) * tk:(2 * u + 2) * tk] = ws[p][u][tq:]
            carry_s[p] = carries[p]
        return start

    values(lax.fori_loop(0, n_iter, body, 0))
    return jnp.concatenate([acc_s[p] for p in range(SB_PAIRS)], axis=1)


def _sb_tri():
    j = lax.broadcasted_iota(jnp.int32, (2 * SB_TK, SB_TK), 0) % SB_TK
    s = lax.broadcasted_iota(jnp.int32, (2 * SB_TK, SB_TK), 1)
    return (j >= s).astype(BF16)


def _sb_scratch(tq):
    return [pltpu.VMEM((SB_PAIRS, 2 * tq, LANES), BF16), pltpu.VMEM((SB_PAIRS, tq, SB_U * 2 * SB_TK), BF16),
            pltpu.VMEM((SB_PAIRS, tq, LANES), F32), pltpu.VMEM((SB_PAIRS, 2 * tq, SB_TK), F32)]


def _sb_block_reader(ref2d):
    def read(start, p):
        start = pl.multiple_of(start, SB_TK)
        return ref2d[pl.ds(start, SB_TK), p * LANES:(p + 1) * LANES].astype(BF16)
    return read


def _sb_prompt_kernel(q_ref, k_ref, v_ref, g_ref, tri_ref, o_ref, qs_s, a_s, acc_s, carry_s, *, tq):
    qi = pl.program_id(1)
    base = pl.multiple_of(qi * tq, tq)
    read_k, read_v = _sb_block_reader(k_ref.at[0]), _sb_block_reader(v_ref.at[0])

    def diag_kv(d, p):
        return read_k(base + d * SB_TK, p), read_v(base + d * SB_TK, p)

    n_iter = qi * (tq // (SB_U * SB_TK))
    y = _sb_all_pairs(q_ref[0], diag_kv, tq // SB_TK, read_k, read_v, n_iter, tri_ref[...],
                      qs_s, a_s, acc_s, carry_s)
    o_ref[0] = (y * jax.nn.silu(g_ref[0])).astype(BF16)


def _sb_prompt(q, k, v, g):
    b, l, _ = q.shape
    tq = SB_U * SB_TK
    qblk = pl.BlockSpec((1, tq, SB_WIDTH), lambda bi, qi: (bi, qi, 0))
    kvblk = pl.BlockSpec((1, l, SB_WIDTH), lambda bi, qi: (bi, 0, 0))
    return pl.pallas_call(
        functools.partial(_sb_prompt_kernel, tq=tq),
        grid=(b, l // tq),
        in_specs=[qblk, kvblk, kvblk, qblk, _full((2 * SB_TK, SB_TK))],
        out_specs=qblk,
        out_shape=jax.ShapeDtypeStruct((b, l, SB_WIDTH), BF16),
        scratch_shapes=_sb_scratch(tq),
        compiler_params=_params(("parallel", "arbitrary")),
        name="sb_prompt",
    )(q, k, v, g, _sb_tri())


def _sb_sample_kernel(q_ref, k_ref, v_ref, ck_ref, cv_ref, g_ref, tri_ref, o_ref, qs_s, a_s, acc_s, carry_s,
                      *, n_iter):
    tq = q_ref.shape[1]

    def diag_kv(d, p):
        ls = slice(p * LANES, (p + 1) * LANES)
        pad = jnp.zeros((SB_TK - tq, LANES), BF16)
        return (jnp.concatenate([k_ref[0, :, ls].astype(BF16), pad], axis=0),
                jnp.concatenate([v_ref[0, :, ls].astype(BF16), pad], axis=0))

    y = _sb_all_pairs(q_ref[0], diag_kv, 1, _sb_block_reader(ck_ref.at[0, 0]), _sb_block_reader(cv_ref.at[0, 0]),
                      n_iter, tri_ref[...], qs_s, a_s, acc_s, carry_s)
    o_ref[0] = (y * jax.nn.silu(g_ref[0])).astype(BF16)


def _sb_sample(q, k, v, ck, cv, layer, g):
    b, l, _ = q.shape
    past = ck.shape[2]
    qblk = pl.BlockSpec((1, l, SB_WIDTH), lambda bi: (bi, 0, 0))
    cblk = pl.BlockSpec((1, 1, past, SB_WIDTH), lambda bi: (layer, bi, 0, 0))
    return pl.pallas_call(
        functools.partial(_sb_sample_kernel, n_iter=past // (SB_U * SB_TK)),
        grid=(b,),
        in_specs=[qblk, qblk, qblk, cblk, cblk, qblk, _full((2 * SB_TK, SB_TK))],
        out_specs=qblk,
        out_shape=jax.ShapeDtypeStruct((b, l, SB_WIDTH), BF16),
        scratch_shapes=_sb_scratch(l),
        compiler_params=_params(("parallel",)),
        name="sb_sample",
    )(q, k, v, ck, cv, g, _sb_tri())


def _s5_kernel(u_ref, g_ref, h0r_ref, h0i_ref, ar_ref, ai_ref, bh_ref, bl_ref, c_ref, d_ref,
               wg_ref, bg_ref, o_ref, hr_o, hi_o, bu_s, h_s, ut_s, gt_s, ot_s, *, tt):
    ti = pl.program_id(1)
    rows = tt * SUBLANES
    slabs = SSM_WIDTH // LANES

    @pl.when(ti == 0)
    def _():
        hr_o[...] = h0r_ref[...]
        hi_o[...] = h0i_ref[...]

    def time_major(ref, slab_s):
        for s in range(SUBLANES):
            for j in range(slabs):
                slab_s[j, pl.ds(s, tt, stride=SUBLANES), :] = ref[s, :, j * LANES:(j + 1) * LANES]
        return jnp.concatenate([slab_s[j] for j in range(slabs)], axis=1)

    u = time_major(u_ref, ut_s)
    uh, ul = _split(u)
    bu_s[...] = _dot(uh, bh_ref[...]) + (_dot(uh, bl_ref[...]) + _dot(ul, bh_ref[...]))

    ar = jnp.broadcast_to(ar_ref[...], (SUBLANES, SSM_FLAT))
    ai = jnp.broadcast_to(ai_ref[...], (SUBLANES, SSM_FLAT))

    def step(t, carry):
        hr, hi = carry
        r0 = pl.multiple_of(t * SUBLANES, SUBLANES)
        br = bu_s[pl.ds(r0, SUBLANES), 0:SSM_FLAT]
        bi = bu_s[pl.ds(r0, SUBLANES), SSM_FLAT:2 * SSM_FLAT]
        nr = ar * hr - ai * hi + br
        ni = ar * hi + ai * hr + bi
        h_s[pl.ds(r0, SUBLANES), 0:SSM_FLAT] = nr
        h_s[pl.ds(r0, SUBLANES), SSM_FLAT:2 * SSM_FLAT] = ni
        return nr, ni

    hr, hi = lax.fori_loop(0, tt, step, (hr_o[...], hi_o[...]))
    hr_o[...] = hr
    hi_o[...] = hi

    y = _dot(h_s[...].astype(BF16), c_ref[...]) + d_ref[...] * u
    y = jax.nn.gelu(y)
    z = _dot(y.astype(BF16), wg_ref[...]) + bg_ref[...]
    out = z[:, :SSM_WIDTH] * jax.nn.sigmoid(z[:, SSM_WIDTH:])
    out = out * jax.nn.silu(time_major(g_ref, gt_s))
    for j in range(slabs):
        ot_s[j] = out[:, j * LANES:(j + 1) * LANES]
    for s in range(SUBLANES):
        for j in range(slabs):
            o_ref[s, :, j * LANES:(j + 1) * LANES] = ot_s[j, pl.ds(s, tt, stride=SUBLANES), :].astype(BF16)


def _s5(u, g, h0r, h0i, sp):
    b, l, _ = u.shape
    tt = min(64, l)
    rows = tt * SUBLANES
    ublk = pl.BlockSpec((SUBLANES, tt, SSM_WIDTH), lambda bi, ti: (bi, ti, 0))
    sblk = pl.BlockSpec((SUBLANES, SSM_FLAT), lambda bi, ti: (bi, 0))
    slab = pltpu.VMEM((SSM_WIDTH // LANES, rows, LANES), F32)
    return pl.pallas_call(
        functools.partial(_s5_kernel, tt=tt),
        grid=(b // SUBLANES, l // tt),
        in_specs=[ublk, ublk, sblk, sblk,
                  _full((1, SSM_FLAT)), _full((1, SSM_FLAT)),
                  _full((SSM_WIDTH, 2 * SSM_FLAT)), _full((SSM_WIDTH, 2 * SSM_FLAT)),
                  _full((2 * SSM_FLAT, SSM_WIDTH)), _full((1, SSM_WIDTH)),
                  _full((SSM_WIDTH, 2 * SSM_WIDTH)), _full((1, 2 * SSM_WIDTH))],
        out_specs=[ublk, sblk, sblk],
        out_shape=[jax.ShapeDtypeStruct((b, l, SSM_WIDTH), BF16),
                   jax.ShapeDtypeStruct((b, SSM_FLAT), F32),
                   jax.ShapeDtypeStruct((b, SSM_FLAT), F32)],
        scratch_shapes=[pltpu.VMEM((rows, 2 * SSM_FLAT), F32), pltpu.VMEM((rows, 2 * SSM_FLAT), F32),
                        slab, slab, slab],
        compiler_params=_params(("parallel", "arbitrary")),
        name="s5",
    )(u, g, h0r, h0i, sp["ar"], sp["ai"], sp["b_hi"], sp["b_lo"], sp["c"], sp["d"], sp["w_glu"], sp["b_glu"])


def _s5_params(lam_re, lam_im, log_dt, b_re, b_im, c_re, c_im, d, w_glu, b_glu):
    dt = jnp.exp(log_dt.astype(F32))[:, None]
    lr = jnp.minimum(lam_re.astype(F32), -1e-4)
    li = lam_im.astype(F32)
    er = jnp.exp(lr * dt)
    ar, ai = er * jnp.cos(li * dt), er * jnp.sin(li * dt)
    den = lr * lr + li * li
    fr = ((ar - 1.0) * lr + ai * li) / den
    fi = (ai * lr - (ar - 1.0) * li) / den
    br, bi = b_re.astype(F32), b_im.astype(F32)
    bbr = fr[..., None] * br - fi[..., None] * bi
    bbi = fr[..., None] * bi + fi[..., None] * br
    eye = jnp.eye(SSM_GROUPS, dtype=F32)
    bd_in = lambda m: jnp.einsum("gpc,gh->gchp", m, eye).reshape(SSM_WIDTH, SSM_FLAT)
    bd_out = lambda m: jnp.einsum("gcp,gh->gphc", m, eye).reshape(SSM_FLAT, SSM_WIDTH)
    b_all = jnp.concatenate([bd_in(bbr), bd_in(bbi)], axis=1)
    b_hi = b_all.astype(BF16)
    b_lo = (b_all - b_hi.astype(F32)).astype(BF16)
    c_all = jnp.concatenate([bd_out(c_re.astype(F32)), -bd_out(c_im.astype(F32))], axis=0)
    return dict(ar=ar.reshape(1, SSM_FLAT), ai=ai.reshape(1, SSM_FLAT), b_hi=b_hi, b_lo=b_lo,
                c=c_all.astype(BF16), d=d.astype(F32).reshape(1, SSM_WIDTH),
                w_glu=w_glu.astype(BF16), b_glu=b_glu.astype(F32).reshape(1, 2 * SSM_WIDTH))


RWKV_CK = 64
RWKV_NB = 8


def _rwkv_kernel(cp_ref, g_ref, prev_ref, s0_ref, mu_ref, w0_ref, w2_ref, a0_ref, a2_ref, kk_ref, ka_ref,
                 ub_ref, lnw_ref, lnb_ref, ho_ref, tri_ref, ones_ref, o_ref, s_o, prev_s,
                 khw_s, rw_s, ktl_s, btl_s, v_s, kht_s, bht_s, wch_s, bonus_s, y_s, *, tt, ck):
    ti = pl.program_id(1)

    @pl.when(ti == 0)
    def _():
        s_o[...] = s0_ref[...]
        prev_s[...] = jnp.broadcast_to(prev_ref[...], prev_s.shape)

    w = RWKV_WIDTH
    nb_all = cp_ref.shape[0]
    head_ones, chunk_tri, chunk_ones = ho_ref[...], tri_ref[...], ones_ref[...]

    def prep(nb, carry):
        cp = cp_ref[nb]
        trow = lax.broadcasted_iota(jnp.int32, (tt, 1), 0)
        shifted = jnp.where(trow == 0, prev_s[nb, 0:1, :], pltpu.roll(cp, 1, 0))
        prev_s[nb] = jnp.broadcast_to(cp[tt - 1:tt, :], (SUBLANES, CP_PAD))
        xc = cp + mu_ref[...] * (shifted - cp)
        r, k, v, lora = xc[:, 0:w], xc[:, w:2 * w], xc[:, 2 * w:3 * w], xc[:, 3 * w:]
        logw = -DECAY_SCALE * jax.nn.sigmoid(w0_ref[...] + _dot(jnp.tanh(lora).astype(BF16), w2_ref[...]))
        a = jax.nn.sigmoid(a0_ref[...] + _dot(lora.astype(BF16), a2_ref[...]))
        cum = _dot2r(chunk_tri, logw)
        tot = _dot2r(chunk_ones, logw)
        kkv = k * kk_ref[...]
        kh = kkv * lax.rsqrt(_dot2l(kkv * kkv, head_ones) + 1e-12)
        kt = k * (1.0 + (a - 1.0) * ka_ref[...])
        bvec = a * kh
        w_inv = jnp.exp(-cum)
        w_end = jnp.exp(tot - cum)
        khw_s[nb] = (kh * jnp.exp(cum - logw)).astype(BF16)
        rw_s[nb] = (r * jnp.exp(cum)).astype(BF16)
        ktl_s[nb] = (kt * w_inv).astype(BF16)
        btl_s[nb] = (bvec * w_inv).astype(BF16)
        v_s[nb] = v.astype(BF16)
        kht_s[nb] = (kt * w_end).astype(BF16)
        bht_s[nb] = (bvec * w_end).astype(BF16)
        wch_s[nb] = jnp.exp(tot)
        bonus_s[nb] = _dot2l(r * kt * ub_ref[...], head_ones) * v
        return carry

    lax.fori_loop(0, nb_all, prep, 0)

    lane = lax.broadcasted_iota(jnp.int32, (1, LANES), 1)
    m_lo = lane < HEAD_DIM
    ri = lax.broadcasted_iota(jnp.int32, (2 * ck, 2 * ck), 0)
    cj = lax.broadcasted_iota(jnp.int32, (2 * ck, 2 * ck), 1)
    same_head = (ri >= ck) == (cj >= ck)
    strict = same_head & (cj < ri)
    lower = same_head & (cj <= ri)
    eye = (ri == cj).astype(F32)
    rr = lax.broadcasted_iota(jnp.int32, (2 * ck, LANES), 0)
    rl = lax.broadcasted_iota(jnp.int32, (2 * ck, LANES), 1)
    own_lanes = (rr >= ck) == (rl >= HEAD_DIM)
    sr = lax.broadcasted_iota(jnp.int32, (LANES, LANES), 0) // HEAD_DIM
    sc = lax.broadcasted_iota(jnp.int32, (LANES, LANES), 1) // HEAD_DIM
    blockdiag = sr == sc
    dot1 = lambda x, y: _dot(x.astype(BF16), y.astype(BF16))

    cat = lambda xs: jnp.concatenate(xs, axis=0)

    def chunk_step(c, carry):
        r0 = pl.multiple_of(c * ck, ck)
        rows = pl.ds(r0, ck)
        chains = [(nb, hp) for nb in range(nb_all) for hp in range(w // LANES)]
        lanes = lambda hp: slice(hp * LANES, (hp + 1) * LANES)
        rd = lambda ref: [ref[nb, rows, lanes(hp)] for nb, hp in chains]
        khw, rw, ktl, btl, v = rd(khw_s), rd(rw_s), rd(ktl_s), rd(btl_s), rd(v_s)
        z = jnp.zeros_like(khw[0])
        l4 = [cat([jnp.where(m_lo, a, z), jnp.where(m_lo, z, a), jnp.where(m_lo, b, z), jnp.where(m_lo, z, b)])
              for a, b in zip(khw, rw)]
        scb = [_dot(a, cat([b, b]), NT) for a, b in zip(l4, btl)]
        sck = [_dot(a, cat([b, b]), NT) for a, b in zip(l4, ktl)]
        nmat = [jnp.where(strict, a[:2 * ck], 0.0) for a in scb]
        arb = [jnp.where(lower, a[2 * ck:], 0.0) for a in scb]
        akk = [jnp.where(strict, a[:2 * ck], 0.0) for a in sck]
        ark = [jnp.where(lower, a[2 * ck:], 0.0) for a in sck]
        inv = [eye - a for a in nmat]
        pw = nmat
        steps = 1
        while 2 * steps < ck:
            pw = [dot1(a, a) for a in pw]
            inv = [dot1(a, eye + b) for a, b in zip(inv, pw)]
            steps *= 2
        s_pair = [s_o[nb, hp] for nb, hp in chains]
        ks = [_dot(cat([a, b]), s.astype(BF16), NT) for a, b, s in zip(khw, rw, s_pair)]
        av = [dot1(cat([a, b]), cat([x, x])) for a, b, x in zip(akk, ark, v)]
        sa2 = [jnp.where(own_lanes, dot1(t, cat([k[:ck], k[:ck]]) + a[:2 * ck]), 0.0)
               for t, k, a in zip(inv, ks, av)]
        y2 = [jnp.where(own_lanes, a[2 * ck:] - dot1(b, s), 0.0) for a, b, s in zip(av, arb, sa2)]
        for (nb, hp), k, y in zip(chains, ks, y2):
            y_s[nb, rows, lanes(hp)] = k[ck:] + y[:ck] + y[ck:]
        kht, bht = rd(kht_s), rd(bht_s)
        upd = [_dot(cat([x, (-(s[:ck] + s[ck:])).astype(BF16)]), cat([a, b]), TN)
               for x, s, a, b in zip(v, sa2, kht, bht)]
        for (nb, hp), s, u in zip(chains, s_pair, upd):
            s_o[nb, hp] = s * wch_s[nb, pl.ds(r0, 1), lanes(hp)] + jnp.where(blockdiag, u, 0.0)
        return carry

    lax.fori_loop(0, tt // ck, chunk_step, 0)

    fin = 2 if nb_all % 2 == 0 else 1

    def finish(i, carry):
        nbs = [i * fin + j for j in range(fin)]
        ys = [y_s[nb] for nb in nbs]
        mean = [_dot2l(y, head_ones) * (1.0 / HEAD_DIM) for y in ys]
        cen = [y - m for y, m in zip(ys, mean)]
        var = [_dot2l(c * c, head_ones) * (1.0 / HEAD_DIM) for c in cen]
        for nb, c, v in zip(nbs, cen, var):
            y = c * lax.rsqrt(v + GN_EPS) * lnw_ref[...] + lnb_ref[...] + bonus_s[nb]
            o_ref[nb] = (y * jax.nn.silu(g_ref[nb])).astype(BF16)
        return carry

    lax.fori_loop(0, nb_all // fin, finish, 0)


def _rwkv(cp, g, prev, s0p, rp):
    b, l, _ = cp.shape
    tt = min(256, l)
    ck = min(RWKV_CK, tt)
    nb = min(RWKV_NB, b)
    tblk = lambda w: pl.BlockSpec((nb, tt, w), lambda bi, ti: (bi, ti, 0))
    sblk = pl.BlockSpec((nb, 2, LANES, LANES), lambda bi, ti: (bi, 0, 0, 0))
    vec = _full((1, RWKV_WIDTH))
    hl = lax.broadcasted_iota(jnp.int32, (RWKV_WIDTH, RWKV_WIDTH), 0) // HEAD_DIM
    hc = lax.broadcasted_iota(jnp.int32, (RWKV_WIDTH, RWKV_WIDTH), 1) // HEAD_DIM
    tr = lax.broadcasted_iota(jnp.int32, (tt, tt), 0)
    tc = lax.broadcasted_iota(jnp.int32, (tt, tt), 1)
    same_chunk = (tr // ck) == (tc // ck)
    slab = lambda dt: pltpu.VMEM((nb, tt, RWKV_WIDTH), dt)
    return pl.pallas_call(
        functools.partial(_rwkv_kernel, tt=tt, ck=ck),
        grid=(b // nb, l // tt),
        in_specs=[tblk(CP_PAD), tblk(RWKV_WIDTH),
                  pl.BlockSpec((nb, 1, CP_PAD), lambda bi, ti: (bi, 0, 0)), sblk,
                  _full((1, CP_PAD)), vec, _full((LANES, RWKV_WIDTH)), vec, _full((LANES, RWKV_WIDTH)),
                  vec, vec, vec, vec, vec,
                  _full((RWKV_WIDTH, RWKV_WIDTH)), _full((tt, tt)), _full((tt, tt))],
        out_specs=[tblk(RWKV_WIDTH), sblk],
        out_shape=[jax.ShapeDtypeStruct((b, l, RWKV_WIDTH), BF16),
                   jax.ShapeDtypeStruct((b, 2, LANES, LANES), F32)],
        scratch_shapes=[pltpu.VMEM((nb, SUBLANES, CP_PAD), F32)] + [slab(BF16)] * 7 + [slab(F32)] * 3,
        compiler_params=_params(("parallel", "arbitrary")),
        name="rwkv",
    )(cp, g, prev, s0p, rp["mu"], rp["w0"], rp["w2"], rp["a0"], rp["a2"], rp["k_k"], rp["k_a"],
      rp["u"], rp["ln_w"], rp["ln_b"],
      (hl == hc).astype(BF16), (same_chunk & (tc <= tr)).astype(BF16), same_chunk.astype(BF16))


def _rwkv_params(mu, w0, w2, a0, a2, k_k, k_a, u, ln_w, ln_b):
    vec = lambda t: t.astype(F32).reshape(1, RWKV_WIDTH)
    mu_p = jnp.zeros((1, CP_PAD), F32).at[0, :RWKV_SHIFT].set(mu.astype(F32))
    w2_p = jnp.zeros((LANES, RWKV_WIDTH), F32).at[:RWKV_LORA].set(w2.astype(F32)).astype(BF16)
    a2_p = jnp.zeros((LANES, RWKV_WIDTH), F32).at[RWKV_LORA:2 * RWKV_LORA].set(a2.astype(F32)).astype(BF16)
    return dict(mu=mu_p, w0=vec(w0), w2=w2_p, a0=vec(a0), a2=a2_p, k_k=vec(k_k), k_a=vec(k_a),
                u=vec(u), ln_w=vec(ln_w), ln_b=vec(ln_b))


def _pair_state(s):
    b = s.shape[0]
    s = s.astype(F32).reshape(b, 2, 2, HEAD_DIM, HEAD_DIM)
    eye = jnp.eye(2, dtype=F32)
    return jnp.einsum("bphvk,hg->bphvgk", s, eye).reshape(b, 2, LANES, LANES)


def _unpair_state(sp):
    b = sp.shape[0]
    s = sp.reshape(b, 2, 2, HEAD_DIM, 2, HEAD_DIM)
    return jnp.stack([s[:, :, 0, :, 0, :], s[:, :, 1, :, 1, :]], axis=2).reshape(b, RWKV_HEADS, HEAD_DIM, HEAD_DIM)


def _outproj_kernel(x_ref, a_ref, b_ref, c_ref, w_ref, o_ref):
    acc = _dot(a_ref[...], w_ref[0:SB_WIDTH, :])
    acc = acc + _dot(b_ref[...], w_ref[SB_WIDTH:SB_WIDTH + SSM_WIDTH, :])
    acc = acc + _dot(c_ref[...], w_ref[SB_WIDTH + SSM_WIDTH:, :])
    o_ref[...] = x_ref[...] + acc


def _outproj(x2d, ma, mb, mc, w_out):
    n = x2d.shape[0]
    tm = min(512, n)
    row = lambda w: pl.BlockSpec((tm, w), lambda i: (i, 0))
    return pl.pallas_call(
        _outproj_kernel,
        grid=(n // tm,),
        in_specs=[row(D_MODEL), row(SB_WIDTH), row(SSM_WIDTH), row(RWKV_WIDTH), _full((D_MODEL, D_MODEL))],
        out_specs=row(D_MODEL),
        out_shape=jax.ShapeDtypeStruct((n, D_MODEL), F32),
        compiler_params=_params(("parallel",)),
        name="outproj",
    )(x2d, ma, mb, mc, w_out)


def _pad_w_in(w_in):
    w = w_in.astype(F32)
    pad = jnp.zeros((D_MODEL, CP_PAD - RWKV_SHIFT), F32)
    return jnp.concatenate([w[:, :COL_CP + RWKV_SHIFT], pad, w[:, COL_CP + RWKV_SHIFT:]], axis=1).astype(BF16)


def _layer(x, lp, past):
    b, l, _ = x.shape
    n = b * l
    x2d = x.reshape(n, D_MODEL)
    q, k, v, ga, u, gb, cp, gc = _inproj(x2d, lp["norm_w"], lp["w_in"], lp["q_w"], lp["k_w"])
    r3 = lambda t: t.reshape(b, l, t.shape[-1])
    q, k, v, ga, u, gb, cp, gc = map(r3, (q, k, v, ga, u, gb, cp, gc))

    if past is None:
        mix_a = _sb_prompt(q, k, v, ga)
        h0r = h0i = jnp.zeros((b, SSM_FLAT), F32)
        prev = jnp.zeros((b, 1, CP_PAD), F32)
        s0p = jnp.zeros((b, 2, LANES, LANES), F32)
    else:
        mix_a = _sb_sample(q, k, v, past["k"], past["v"], past["layer"], ga)
        h0r = past["ssm_re"].astype(F32).reshape(b, SSM_FLAT)
        h0i = past["ssm_im"].astype(F32).reshape(b, SSM_FLAT)
        prev = jnp.pad(past["shift"].astype(F32), ((0, 0), (0, CP_PAD - RWKV_SHIFT)))[:, None, :]
        s0p = _pair_state(past["wkv"])

    mix_b, hr, hi = _s5(u, gb, h0r, h0i, lp["s5"])
    mix_c, s_pair = _rwkv(cp, gc, prev, s0p, lp["rwkv"])

    x_new = _outproj(x2d, mix_a.reshape(n, SB_WIDTH), mix_b.reshape(n, SSM_WIDTH),
                     mix_c.reshape(n, RWKV_WIDTH), lp["w_out"]).reshape(b, l, D_MODEL)
    heads = lambda t: t.reshape(b, l, SB_WIDTH // HEAD_DIM, HEAD_DIM)
    state = (heads(k), heads(v), hr.reshape(b, SSM_GROUPS, SSM_STATE), hi.reshape(b, SSM_GROUPS, SSM_STATE),
             _unpair_state(s_pair), cp[:, -1, :RWKV_SHIFT])
    return x_new, state


def kernel(x_prompt, x_sample, cache_k, cache_v, state_ssm_re, state_ssm_im, state_wkv, state_shift, norm_w, w_in, q_norm_w, k_norm_w, ssm_lambda_re, ssm_lambda_im, ssm_log_dt, ssm_b_re, ssm_b_im, ssm_c_re, ssm_c_im, ssm_d, ssm_w_glu, ssm_b_glu, rwkv_mu, rwkv_w0, rwkv_w2, rwkv_a0, rwkv_a2, rwkv_k_k, rwkv_k_a, rwkv_u, rwkv_ln_w, rwkv_ln_b, w_out):
    depth = w_in.shape[0]
    xp, xs = x_prompt, x_sample
    ck_all = cache_k.reshape(cache_k.shape[:3] + (SB_WIDTH,))
    cv_all = cache_v.reshape(cache_v.shape[:3] + (SB_WIDTH,))
    p_states, s_states = [], []
    for i in range(depth):
        tile8 = lambda t: jnp.tile(t.astype(F32), SB_WIDTH // HEAD_DIM).reshape(1, SB_WIDTH)
        lp = dict(
            norm_w=norm_w[i].astype(F32).reshape(1, D_MODEL), w_in=_pad_w_in(w_in[i]),
            q_w=tile8(q_norm_w[i]), k_w=tile8(k_norm_w[i]),
            s5=_s5_params(ssm_lambda_re[i], ssm_lambda_im[i], ssm_log_dt[i], ssm_b_re[i], ssm_b_im[i],
                          ssm_c_re[i], ssm_c_im[i], ssm_d[i], ssm_w_glu[i], ssm_b_glu[i]),
            rwkv=_rwkv_params(rwkv_mu[i], rwkv_w0[i], rwkv_w2[i], rwkv_a0[i], rwkv_a2[i], rwkv_k_k[i],
                              rwkv_k_a[i], rwkv_u[i].reshape(-1), rwkv_ln_w[i], rwkv_ln_b[i]),
            w_out=w_out[i].astype(BF16))
        xp, st_p = _layer(xp, lp, None)
        past = dict(k=ck_all, v=cv_all, layer=i,
                    ssm_re=state_ssm_re[i], ssm_im=state_ssm_im[i], wkv=state_wkv[i], shift=state_shift[i])
        xs, st_s = _layer(xs, lp, past)
        p_states.append(st_p)
        s_states.append(st_s)
    stack = lambda states, j: jnp.stack([s[j] for s in states], axis=0)
    return ((xp, xs) + tuple(stack(p_states, j) for j in range(6))
            + tuple(stack(s_states, j) for j in range(6)))
```

```python
import functools
import math

import jax
import jax.numpy as jnp
from jax import lax
from jax.experimental import pallas as pl
from jax.experimental.pallas import tpu as pltpu

F32 = jnp.float32
BF16 = jnp.bfloat16

D_MODEL = 1024
HEAD_DIM = 64
SB_WIDTH = D_MODEL // 2
SSM_WIDTH = D_MODEL // 4
SSM_GROUP = 16
SSM_GROUPS = SSM_WIDTH // SSM_GROUP
SSM_STATE = 64
SSM_FLAT = SSM_GROUPS * SSM_STATE
RWKV_WIDTH = D_MODEL // 4
RWKV_HEADS = RWKV_WIDTH // HEAD_DIM
RWKV_LORA = 32
RWKV_SHIFT = 3 * RWKV_WIDTH + 2 * RWKV_LORA
RMS_EPS = 1e-6
GN_EPS = 64e-5
DECAY_SCALE = math.exp(-0.5)

LANES = 128
SUBLANES = 8
CP_PAD = 7 * LANES
COL_Q, COL_K, COL_V, COL_GA = 0, 512, 1024, 1536
COL_U, COL_GB, COL_CP, COL_GC = 2048, 2304, 2560, 2560 + CP_PAD
D_IN_PAD = COL_GC + RWKV_WIDTH
VMEM_LIMIT = 56 * 1024 * 1024

NN = (((1,), (0,)), ((), ()))
NT = (((1,), (1,)), ((), ()))
TN = (((0,), (0,)), ((), ()))


def _dot(a, b, dims=NN):
    return lax.dot_general(a, b, dims, preferred_element_type=F32)


def _split(a):
    hi = a.astype(BF16)
    lo = (a - hi.astype(F32)).astype(BF16)
    return hi, lo


def _dot3(a, b, dims=NN):
    ah, al = _split(a)
    bh, bl = _split(b)
    return _dot(ah, bh, dims) + (_dot(ah, bl, dims) + _dot(al, bh, dims))


def _dot2l(a, b_exact, dims=NN):
    ah, al = _split(a)
    return _dot(ah, b_exact, dims) + _dot(al, b_exact, dims)


def _dot2r(a_exact, b, dims=NN):
    bh, bl = _split(b)
    return _dot(a_exact, bh, dims) + _dot(a_exact, bl, dims)


def _params(sem):
    return pltpu.CompilerParams(dimension_semantics=sem, vmem_limit_bytes=VMEM_LIMIT)


def _full(shape):
    nd = len(shape)
    return pl.BlockSpec(shape, lambda *_: (0,) * nd)


def _head_rms(t, w):
    lane = lax.broadcasted_iota(jnp.int32, (1, LANES), 1)
    lo = lane < HEAD_DIM
    outs = []
    for j in range(t.shape[1] // LANES):
        blk = t[:, j * LANES:(j + 1) * LANES]
        sq = blk * blk
        s_lo = jnp.sum(jnp.where(lo, sq, 0.0), axis=-1, keepdims=True)
        s_hi = jnp.sum(jnp.where(lo, 0.0, sq), axis=-1, keepdims=True)
        ms = jnp.where(lo, s_lo, s_hi) * (1.0 / HEAD_DIM)
        outs.append(blk * lax.rsqrt(ms + RMS_EPS) * w[:, j * LANES:(j + 1) * LANES])
    return jnp.concatenate(outs, axis=-1)


def _inproj_kernel(x_ref, nw_ref, w_ref, qw_ref, kw_ref,
                   q_o, k_o, v_o, ga_o, u_o, gb_o, cp_o, gc_o):
    x = x_ref[...]
    ms = jnp.mean(x * x, axis=-1, keepdims=True)
    h = (x * lax.rsqrt(ms + RMS_EPS) * nw_ref[...]).astype(BF16)

    def proj(a, b):
        return _dot(h, w_ref[:, a:b])

    q = _head_rms(proj(COL_Q, COL_K), qw_ref[...])
    q_o[...] = (q * (HEAD_DIM ** -0.5)).astype(BF16)
    k_o[...] = _head_rms(proj(COL_K, COL_V), kw_ref[...])
    v_o[...] = proj(COL_V, COL_GA)
    ga_o[...] = proj(COL_GA, COL_U)
    u_o[...] = proj(COL_U, COL_GB)
    gb_o[...] = proj(COL_GB, COL_CP)
    cp_o[...] = proj(COL_CP, COL_GC)
    gc_o[...] = proj(COL_GC, D_IN_PAD)


def _inproj(x2d, nw, w_pad, qw, kw):
    n = x2d.shape[0]
    tm = min(256, n)
    widths = (SB_WIDTH, SB_WIDTH, SB_WIDTH, SB_WIDTH, SSM_WIDTH, SSM_WIDTH, CP_PAD, RWKV_WIDTH)
    dtypes = (BF16,) + (F32,) * 7
    row = lambda w: pl.BlockSpec((tm, w), lambda i: (i, 0))
    return pl.pallas_call(
        _inproj_kernel,
        grid=(n // tm,),
        in_specs=[row(D_MODEL), _full((1, D_MODEL)), _full((D_MODEL, D_IN_PAD)),
                  _full((1, SB_WIDTH)), _full((1, SB_WIDTH))],
        out_specs=[row(w) for w in widths],
        out_shape=[jax.ShapeDtypeStruct((n, w), dt) for w, dt in zip(widths, dtypes)],
        compiler_params=_params(("parallel",)),
        name="inproj",
    )(x2d, nw, w_pad, qw, kw)


SB_TK = 128
SB_PAIRS = SB_WIDTH // LANES


SB_U = 2
LOG2E = math.log2(math.e)


def _sb_weights(qs, kbs, carries, tri, masks):
    return _sb_weights_of(_sb_scores(qs, kbs), carries, tri, masks)


def _sb_scores(qs, kbs):
    z = []
    for q, kb in zip(qs, kbs):
        tk = kb[0].shape[0]
        zp = _dot(q, jnp.concatenate(kb, axis=0), NT)
        z.append([zp[:, u * tk:(u + 1) * tk] for u in range(len(kb))])
    return z


def _sb_weights_of(scores, carries, tri, masks):
    pu = [(p, u) for p in range(len(scores)) for u in range(len(scores[p]))]
    z = {k: scores[k[0]][k[1]] for k in pu}
    hl = {}
    for k in pu:
        sp = jnp.maximum(z[k], 0.0) + jnp.log(1.0 + jnp.exp2(jnp.abs(z[k]) * (-LOG2E)))
        if masks[k[1]] is not None:
            sp = jnp.where(masks[k[1]], sp, 0.0)
        hl[k] = sp.astype(BF16)
    cs = {k: _dot(hl[k], tri) for k in pu}
    weights, new_carries = [], []
    for p in range(len(scores)):
        carry, row = carries[p], []
        for u in range(len(scores[p])):
            incl = cs[(p, u)] + carry
            a = jnp.exp2((z[(p, u)] - incl) * LOG2E)
            if masks[u] is not None:
                a = jnp.where(masks[u], a, 0.0)
            row.append(a.astype(BF16))
            carry = jnp.broadcast_to(incl[:, 0:1], incl.shape)
        weights.append(row)
        new_carries.append(carry)
    return weights, new_carries


def _sb_values(a2, vbs, head_lo):
    parts = []
    for vb in vbs:
        zero = jnp.zeros_like(vb)
        parts += [jnp.where(head_lo, vb, zero), jnp.where(head_lo, zero, vb)]
    return _dot(a2, jnp.concatenate(parts, axis=0))


def _sb_all_pairs(q, diag_kv, n_diag, past_k, past_v, n_iter, tri, qs_s, a_s, acc_s, carry_s):
    tq, tk = q.shape[0], SB_TK
    head_lo = lax.broadcasted_iota(jnp.int32, (1, LANES), 1) < HEAD_DIM
    pairs = range(SB_PAIRS)
    qs = []
    for p in pairs:
        qp = q[:, p * LANES:(p + 1) * LANES]
        zero = jnp.zeros_like(qp)
        qs.append(jnp.concatenate([jnp.where(head_lo, qp, zero), jnp.where(head_lo, zero, qp)], axis=0))
        qs_s[p] = qs[p]
    rg = tq // n_diag
    row = lax.broadcasted_iota(jnp.int32, (2 * rg, tk), 0)
    col = lax.broadcasted_iota(jnp.int32, (2 * rg, tk), 1)
    causal = col < jnp.where(row >= rg, row - rg, row)
    diag = [[diag_kv(d, p) for d in range(n_diag)] for p in pairs]
    groups = [(p, r) for p in pairs for r in range(n_diag)]
    rows_of = lambda x, r: jnp.concatenate([x[r * rg:(r + 1) * rg], x[tq + r * rg:tq + (r + 1) * rg]], axis=0)
    ws, carries = _sb_weights([rows_of(qs[p], r) for p, r in groups],
                              [[diag[p][d][0] for d in range(r, -1, -1)] for p, r in groups],
                              [jnp.zeros((2 * rg, tk), F32)] * len(groups), tri, [causal] + [None] * (n_diag - 1))
    for p in pairs:
        accs, c_lo, c_hi = [], [], []
        for r in range(n_diag):
            g = groups.index((p, r))
            a2 = jnp.concatenate([half for a in ws[g] for half in (a[:rg], a[rg:])], axis=1)
            accs.append(_sb_values(a2, [diag[p][d][1] for d in range(r, -1, -1)], head_lo))
            c_lo.append(carries[g][:rg])
            c_hi.append(carries[g][rg:])
        acc_s[p] = jnp.concatenate(accs, axis=0)
        carry_s[p] = jnp.concatenate(c_lo + c_hi, axis=0)
    a_s[...] = jnp.zeros_like(a_s)

    def values(prev_start):
        for p in pairs:
            vbs = [past_v(prev_start + (SB_U - 1 - u) * tk, p) for u in range(SB_U)]
            acc_s[p] += _sb_values(a_s[p], vbs, head_lo)

    span = SB_U * tk

    def step(it):
        start = (n_iter - 1 - it) * span
        prev_start = jnp.where(it == 0, 0, (n_iter - it) * span)
        kbs = [[past_k(start + (SB_U - 1 - u) * tk, p) for u in range(SB_U)] for p in pairs]
        scores = _sb_scores([qs_s[p] for p in pairs], kbs)
        values(prev_start)
        ws, carries = _sb_weights_of(scores, [carry_s[p] for p in pairs], tri, [None] * SB_U)
        for p in pairs:
            for u in range(SB_U):
                a_s[p, :, (2 * u) * tk:(2 * u + 1) * tk] = ws[p][u][:tq]
                a_s[p, :, (2 * u + 1) * tk:(2 * u + 2) * tk] = ws[p][u][tq:]
            carry_s[p] = carries[p]

    odd = n_iter % 2

    @pl.when(odd == 1)
    def _():
        step(jnp.int32(0))

    def two_steps(i, c):
        step(odd + 2 * i)
        step(odd + 2 * i + 1)
        return c

    lax.fori_loop(0, n_iter // 2, two_steps, 0)
    values(0)
    return jnp.concatenate([acc_s[p] for p in range(SB_PAIRS)], axis=1)


def _sb_tri():
    j = lax.broadcasted_iota(jnp.int32, (SB_TK, SB_TK), 0)
    s = lax.broadcasted_iota(jnp.int32, (SB_TK, SB_TK), 1)
    return (j >= s).astype(BF16)


def _sb_scratch(tq):
    return [pltpu.VMEM((SB_PAIRS, 2 * tq, LANES), BF16), pltpu.VMEM((SB_PAIRS, tq, SB_U * 2 * SB_TK), BF16),
            pltpu.VMEM((SB_PAIRS, tq, LANES), F32), pltpu.VMEM((SB_PAIRS, 2 * tq, SB_TK), F32)]


def _sb_block_reader(ref2d):
    def read(start, p):
        if not isinstance(start, int):
            start = pl.multiple_of(start, SB_TK)
        return ref2d[pl.ds(start, SB_TK), p * LANES:(p + 1) * LANES].astype(BF16)
    return read


def _sb_prompt_kernel(q_ref, k_ref, v_ref, g_ref, tri_ref, o_ref, qs_s, a_s, acc_s, carry_s, *, tq):
    qi = pl.program_id(1)
    base = pl.multiple_of(qi * tq, tq)
    read_k, read_v = _sb_block_reader(k_ref.at[0]), _sb_block_reader(v_ref.at[0])

    def diag_kv(d, p):
        return read_k(base + d * SB_TK, p), read_v(base + d * SB_TK, p)

    n_iter = qi * (tq // (SB_U * SB_TK))
    y = _sb_all_pairs(q_ref[0], diag_kv, tq // SB_TK, read_k, read_v, n_iter, tri_ref[...],
                      qs_s, a_s, acc_s, carry_s)
    o_ref[0] = (y * jax.nn.silu(g_ref[0])).astype(BF16)


def _sb_prompt(q, k, v, g):
    b, l, _ = q.shape
    tq = SB_U * SB_TK
    qblk = pl.BlockSpec((1, tq, SB_WIDTH), lambda bi, qi: (bi, qi, 0))
    kvblk = pl.BlockSpec((1, l, SB_WIDTH), lambda bi, qi: (bi, 0, 0))
    return pl.pallas_call(
        functools.partial(_sb_prompt_kernel, tq=tq),
        grid=(b, l // tq),
        in_specs=[qblk, kvblk, kvblk, qblk, _full((SB_TK, SB_TK))],
        out_specs=qblk,
        out_shape=jax.ShapeDtypeStruct((b, l, SB_WIDTH), BF16),
        scratch_shapes=_sb_scratch(tq),
        compiler_params=_params(("parallel", "arbitrary")),
        name="sb_prompt",
    )(q, k, v, g, _sb_tri())


def _sb_sample_kernel(q_ref, k_ref, v_ref, ck_ref, cv_ref, g_ref, tri_ref, o_ref, qs_s, a_s, acc_s, carry_s,
                      *, n_iter):
    tq = q_ref.shape[1]

    def diag_kv(d, p):
        ls = slice(p * LANES, (p + 1) * LANES)
        pad = jnp.zeros((SB_TK - tq, LANES), BF16)
        return (jnp.concatenate([k_ref[0, :, ls].astype(BF16), pad], axis=0),
                jnp.concatenate([v_ref[0, :, ls].astype(BF16), pad], axis=0))

    y = _sb_all_pairs(q_ref[0], diag_kv, 1, _sb_block_reader(ck_ref.at[0, 0]), _sb_block_reader(cv_ref.at[0, 0]),
                      n_iter, tri_ref[...], qs_s, a_s, acc_s, carry_s)
    o_ref[0] = (y * jax.nn.silu(g_ref[0])).astype(BF16)


def _sb_sample(q, k, v, ck, cv, layer, g):
    b, l, _ = q.shape
    past = ck.shape[2]
    qblk = pl.BlockSpec((1, l, SB_WIDTH), lambda bi: (bi, 0, 0))
    cblk = pl.BlockSpec((1, 1, past, SB_WIDTH), lambda bi: (layer, bi, 0, 0))
    return pl.pallas_call(
        functools.partial(_sb_sample_kernel, n_iter=past // (SB_U * SB_TK)),
        grid=(b,),
        in_specs=[qblk, qblk, qblk, cblk, cblk, qblk, _full((SB_TK, SB_TK))],
        out_specs=qblk,
        out_shape=jax.ShapeDtypeStruct((b, l, SB_WIDTH), BF16),
        scratch_shapes=_sb_scratch(l),
        compiler_params=_params(("parallel",)),
        name="sb_sample",
    )(q, k, v, ck, cv, g, _sb_tri())


def _s5_kernel(u_ref, g_ref, h0r_ref, h0i_ref, ar_ref, ai_ref, bh_ref, bl_ref, c_ref, d_ref,
               wg_ref, bg_ref, o_ref, hr_o, hi_o, bu_s, h_s, ut_s, gt_s, ot_s, *, tt):
    ti = pl.program_id(1)
    rows = tt * SUBLANES
    slabs = SSM_WIDTH // LANES

    @pl.when(ti == 0)
    def _():
        hr_o[...] = h0r_ref[...]
        hi_o[...] = h0i_ref[...]

    def time_major(ref, slab_s):
        for s in range(SUBLANES):
            for j in range(slabs):
                slab_s[j, pl.ds(s, tt, stride=SUBLANES), :] = ref[s, :, j * LANES:(j + 1) * LANES]
        return jnp.concatenate([slab_s[j] for j in range(slabs)], axis=1)

    u = time_major(u_ref, ut_s)
    uh, ul = _split(u)
    bu_s[...] = _dot(uh, bh_ref[...]) + (_dot(uh, bl_ref[...]) + _dot(ul, bh_ref[...]))

    ar = jnp.broadcast_to(ar_ref[...], (SUBLANES, SSM_FLAT))
    ai = jnp.broadcast_to(ai_ref[...], (SUBLANES, SSM_FLAT))

    def step(t, carry):
        hr, hi = carry
        r0 = pl.multiple_of(t * SUBLANES, SUBLANES)
        br = bu_s[pl.ds(r0, SUBLANES), 0:SSM_FLAT]
        bi = bu_s[pl.ds(r0, SUBLANES), SSM_FLAT:2 * SSM_FLAT]
        nr = ar * hr - ai * hi + br
        ni = ar * hi + ai * hr + bi
        h_s[pl.ds(r0, SUBLANES), 0:SSM_FLAT] = nr
        h_s[pl.ds(r0, SUBLANES), SSM_FLAT:2 * SSM_FLAT] = ni
        return nr, ni

    hr, hi = lax.fori_loop(0, tt, step, (hr_o[...], hi_o[...]))
    hr_o[...] = hr
    hi_o[...] = hi

    y = _dot(h_s[...].astype(BF16), c_ref[...]) + d_ref[...] * u
    y = jax.nn.gelu(y)
    z = _dot(y.astype(BF16), wg_ref[...]) + bg_ref[...]
    out = z[:, :SSM_WIDTH] * jax.nn.sigmoid(z[:, SSM_WIDTH:])
    out = out * jax.nn.silu(time_major(g_ref, gt_s))
    for j in range(slabs):
        ot_s[j] = out[:, j * LANES:(j + 1) * LANES]
    for s in range(SUBLANES):
        for j in range(slabs):
            o_ref[s, :, j * LANES:(j + 1) * LANES] = ot_s[j, pl.ds(s, tt, stride=SUBLANES), :].astype(BF16)


def _s5(u, g, h0r, h0i, sp):
    b, l, _ = u.shape
    tt = min(64, l)
    rows = tt * SUBLANES
    ublk = pl.BlockSpec((SUBLANES, tt, SSM_WIDTH), lambda bi, ti: (bi, ti, 0))
    sblk = pl.BlockSpec((SUBLANES, SSM_FLAT), lambda bi, ti: (bi, 0))
    slab = pltpu.VMEM((SSM_WIDTH // LANES, rows, LANES), F32)
    return pl.pallas_call(
        functools.partial(_s5_kernel, tt=tt),
        grid=(b // SUBLANES, l // tt),
        in_specs=[ublk, ublk, sblk, sblk,
                  _full((1, SSM_FLAT)), _full((1, SSM_FLAT)),
                  _full((SSM_WIDTH, 2 * SSM_FLAT)), _full((SSM_WIDTH, 2 * SSM_FLAT)),
                  _full((2 * SSM_FLAT, SSM_WIDTH)), _full((1, SSM_WIDTH)),
                  _full((SSM_WIDTH, 2 * SSM_WIDTH)), _full((1, 2 * SSM_WIDTH))],
        out_specs=[ublk, sblk, sblk],
        out_shape=[jax.ShapeDtypeStruct((b, l, SSM_WIDTH), BF16),
                   jax.ShapeDtypeStruct((b, SSM_FLAT), F32),
                   jax.ShapeDtypeStruct((b, SSM_FLAT), F32)],
        scratch_shapes=[pltpu.VMEM((rows, 2 * SSM_FLAT), F32), pltpu.VMEM((rows, 2 * SSM_FLAT), F32),
                        slab, slab, slab],
        compiler_params=_params(("parallel", "arbitrary")),
        name="s5",
    )(u, g, h0r, h0i, sp["ar"], sp["ai"], sp["b_hi"], sp["b_lo"], sp["c"], sp["d"], sp["w_glu"], sp["b_glu"])


def _s5_params(lam_re, lam_im, log_dt, b_re, b_im, c_re, c_im, d, w_glu, b_glu):
    dt = jnp.exp(log_dt.astype(F32))[:, None]
    lr = jnp.minimum(lam_re.astype(F32), -1e-4)
    li = lam_im.astype(F32)
    er = jnp.exp(lr * dt)
    ar, ai = er * jnp.cos(li * dt), er * jnp.sin(li * dt)
    den = lr * lr + li * li
    fr = ((ar - 1.0) * lr + ai * li) / den
    fi = (ai * lr - (ar - 1.0) * li) / den
    br, bi = b_re.astype(F32), b_im.astype(F32)
    bbr = fr[..., None] * br - fi[..., None] * bi
    bbi = fr[..., None] * bi + fi[..., None] * br
    eye = jnp.eye(SSM_GROUPS, dtype=F32)
    bd_in = lambda m: jnp.einsum("gpc,gh->gchp", m, eye).reshape(SSM_WIDTH, SSM_FLAT)
    bd_out = lambda m: jnp.einsum("gcp,gh->gphc", m, eye).reshape(SSM_FLAT, SSM_WIDTH)
    b_all = jnp.concatenate([bd_in(bbr), bd_in(bbi)], axis=1)
    b_hi = b_all.astype(BF16)
    b_lo = (b_all - b_hi.astype(F32)).astype(BF16)
    c_all = jnp.concatenate([bd_out(c_re.astype(F32)), -bd_out(c_im.astype(F32))], axis=0)
    return dict(ar=ar.reshape(1, SSM_FLAT), ai=ai.reshape(1, SSM_FLAT), b_hi=b_hi, b_lo=b_lo,
                c=c_all.astype(BF16), d=d.astype(F32).reshape(1, SSM_WIDTH),
                w_glu=w_glu.astype(BF16), b_glu=b_glu.astype(F32).reshape(1, 2 * SSM_WIDTH))


RWKV_CK = 64
RWKV_NB = 8


def _rwkv_kernel(cp_ref, g_ref, prev_ref, s0_ref, mu_ref, w0_ref, w2_ref, a0_ref, a2_ref, kk_ref, ka_ref,
                 ub_ref, lnw_ref, lnb_ref, ho_ref, tri_ref, ones_ref, o_ref, s_o, prev_s,
                 khw_s, rw_s, ktl_s, btl_s, v_s, kht_s, bht_s, wch_s, bonus_s, y_s, *, tt, ck):
    ti = pl.program_id(1)

    @pl.when(ti == 0)
    def _():
        s_o[...] = s0_ref[...]
        prev_s[...] = jnp.broadcast_to(prev_ref[...], prev_s.shape)

    w = RWKV_WIDTH
    nb_all = cp_ref.shape[0]
    head_ones, chunk_tri, chunk_ones = ho_ref[...], tri_ref[...], ones_ref[...]

    def prep(nb, carry):
        cp = cp_ref[nb]
        trow = lax.broadcasted_iota(jnp.int32, (tt, 1), 0)
        shifted = jnp.where(trow == 0, prev_s[nb, 0:1, :], pltpu.roll(cp, 1, 0))
        prev_s[nb] = jnp.broadcast_to(cp[tt - 1:tt, :], (SUBLANES, CP_PAD))
        xc = cp + mu_ref[...] * (shifted - cp)
        r, k, v, lora = xc[:, 0:w], xc[:, w:2 * w], xc[:, 2 * w:3 * w], xc[:, 3 * w:]
        logw = -DECAY_SCALE * jax.nn.sigmoid(w0_ref[...] + _dot(jnp.tanh(lora).astype(BF16), w2_ref[...]))
        a = jax.nn.sigmoid(a0_ref[...] + _dot(lora.astype(BF16), a2_ref[...]))
        cum = _dot2r(chunk_tri, logw)
        tot = _dot2r(chunk_ones, logw)
        kkv = k * kk_ref[...]
        kh = kkv * lax.rsqrt(_dot2l(kkv * kkv, head_ones) + 1e-12)
        kt = k * (1.0 + (a - 1.0) * ka_ref[...])
        bvec = a * kh
        w_inv = jnp.exp(-cum)
        w_end = jnp.exp(tot - cum)
        khw_s[nb] = (kh * jnp.exp(cum - logw)).astype(BF16)
        rw_s[nb] = (r * jnp.exp(cum)).astype(BF16)
        ktl_s[nb] = (kt * w_inv).astype(BF16)
        btl_s[nb] = (bvec * w_inv).astype(BF16)
        v_s[nb] = v.astype(BF16)
        kht_s[nb] = (kt * w_end).astype(BF16)
        bht_s[nb] = (bvec * w_end).astype(BF16)
        wch_s[nb] = jnp.exp(tot)
        bonus_s[nb] = _dot2l(r * kt * ub_ref[...], head_ones) * v
        return carry

    lax.fori_loop(0, nb_all, prep, 0)

    lane = lax.broadcasted_iota(jnp.int32, (1, LANES), 1)
    m_lo = lane < HEAD_DIM
    ri = lax.broadcasted_iota(jnp.int32, (2 * ck, 2 * ck), 0)
    cj = lax.broadcasted_iota(jnp.int32, (2 * ck, 2 * ck), 1)
    same_head = (ri >= ck) == (cj >= ck)
    strict = same_head & (cj < ri)
    lower = same_head & (cj <= ri)
    eye = (ri == cj).astype(F32)
    rr = lax.broadcasted_iota(jnp.int32, (2 * ck, LANES), 0)
    rl = lax.broadcasted_iota(jnp.int32, (2 * ck, LANES), 1)
    own_lanes = (rr >= ck) == (rl >= HEAD_DIM)
    sr = lax.broadcasted_iota(jnp.int32, (LANES, LANES), 0) // HEAD_DIM
    sc = lax.broadcasted_iota(jnp.int32, (LANES, LANES), 1) // HEAD_DIM
    blockdiag = sr == sc
    dot1 = lambda x, y: _dot(x.astype(BF16), y.astype(BF16))

    cat = lambda xs: jnp.concatenate(xs, axis=0)

    def chunk_step(c, carry):
        r0 = pl.multiple_of(c * ck, ck)
        rows = pl.ds(r0, ck)
        chains = [(nb, hp) for nb in range(nb_all) for hp in range(w // LANES)]
        lanes = lambda hp: slice(hp * LANES, (hp + 1) * LANES)
        rd = lambda ref: [ref[nb, rows, lanes(hp)] for nb, hp in chains]
        khw, rw, ktl, btl, v = rd(khw_s), rd(rw_s), rd(ktl_s), rd(btl_s), rd(v_s)
        z = jnp.zeros_like(khw[0])
        l4 = [cat([jnp.where(m_lo, a, z), jnp.where(m_lo, z, a), jnp.where(m_lo, b, z), jnp.where(m_lo, z, b)])
              for a, b in zip(khw, rw)]
        scb = [_dot(a, cat([b, b]), NT) for a, b in zip(l4, btl)]
        sck = [_dot(a, cat([b, b]), NT) for a, b in zip(l4, ktl)]
        nmat = [jnp.where(strict, a[:2 * ck], 0.0) for a in scb]
        arb = [jnp.where(lower, a[2 * ck:], 0.0) for a in scb]
        akk = [jnp.where(strict, a[:2 * ck], 0.0) for a in sck]
        ark = [jnp.where(lower, a[2 * ck:], 0.0) for a in sck]
        inv = [eye - a for a in nmat]
        pw = nmat
        steps = 1
        while 2 * steps < ck:
            pw = [dot1(a, a) for a in pw]
            inv = [dot1(a, eye + b) for a, b in zip(inv, pw)]
            steps *= 2
        s_pair = [s_o[nb, hp] for nb, hp in chains]
        ks = [_dot(cat([a, b]), s.astype(BF16), NT) for a, b, s in zip(khw, rw, s_pair)]
        av = [dot1(cat([a, b]), cat([x, x])) for a, b, x in zip(akk, ark, v)]
        sa2 = [jnp.where(own_lanes, dot1(t, cat([k[:ck], k[:ck]]) + a[:2 * ck]), 0.0)
               for t, k, a in zip(inv, ks, av)]
        y2 = [jnp.where(own_lanes, a[2 * ck:] - dot1(b, s), 0.0) for a, b, s in zip(av, arb, sa2)]
        for (nb, hp), k, y in zip(chains, ks, y2):
            y_s[nb, rows, lanes(hp)] = k[ck:] + y[:ck] + y[ck:]
        kht, bht = rd(kht_s), rd(bht_s)
        upd = [_dot(cat([x, (-(s[:ck] + s[ck:])).astype(BF16)]), cat([a, b]), TN)
               for x, s, a, b in zip(v, sa2, kht, bht)]
        for (nb, hp), s, u in zip(chains, s_pair, upd):
            s_o[nb, hp] = s * wch_s[nb, pl.ds(r0, 1), lanes(hp)] + jnp.where(blockdiag, u, 0.0)
        return carry

    lax.fori_loop(0, tt // ck, chunk_step, 0)

    fin = 2 if nb_all % 2 == 0 else 1

    def finish(i, carry):
        nbs = [i * fin + j for j in range(fin)]
        ys = [y_s[nb] for nb in nbs]
        mean = [_dot2l(y, head_ones) * (1.0 / HEAD_DIM) for y in ys]
        cen = [y - m for y, m in zip(ys, mean)]
        var = [_dot2l(c * c, head_ones) * (1.0 / HEAD_DIM) for c in cen]
        for nb, c, v in zip(nbs, cen, var):
            y = c * lax.rsqrt(v + GN_EPS) * lnw_ref[...] + lnb_ref[...] + bonus_s[nb]
            o_ref[nb] = (y * jax.nn.silu(g_ref[nb])).astype(BF16)
        return carry

    lax.fori_loop(0, nb_all // fin, finish, 0)


def _rwkv(cp, g, prev, s0p, rp):
    b, l, _ = cp.shape
    tt = min(256, l)
    ck = min(RWKV_CK, tt)
    nb = min(RWKV_NB, b)
    tblk = lambda w: pl.BlockSpec((nb, tt, w), lambda bi, ti: (bi, ti, 0))
    sblk = pl.BlockSpec((nb, 2, LANES, LANES), lambda bi, ti: (bi, 0, 0, 0))
    vec = _full((1, RWKV_WIDTH))
    hl = lax.broadcasted_iota(jnp.int32, (RWKV_WIDTH, RWKV_WIDTH), 0) // HEAD_DIM
    hc = lax.broadcasted_iota(jnp.int32, (RWKV_WIDTH, RWKV_WIDTH), 1) // HEAD_DIM
    tr = lax.broadcasted_iota(jnp.int32, (tt, tt), 0)
    tc = lax.broadcasted_iota(jnp.int32, (tt, tt), 1)
    same_chunk = (tr // ck) == (tc // ck)
    slab = lambda dt: pltpu.VMEM((nb, tt, RWKV_WIDTH), dt)
    return pl.pallas_call(
        functools.partial(_rwkv_kernel, tt=tt, ck=ck),
        grid=(b // nb, l // tt),
        in_specs=[tblk(CP_PAD), tblk(RWKV_WIDTH),
                  pl.BlockSpec((nb, 1, CP_PAD), lambda bi, ti: (bi, 0, 0)), sblk,
                  _full((1, CP_PAD)), vec, _full((LANES, RWKV_WIDTH)), vec, _full((LANES, RWKV_WIDTH)),
                  vec, vec, vec, vec, vec,
                  _full((RWKV_WIDTH, RWKV_WIDTH)), _full((tt, tt)), _full((tt, tt))],
        out_specs=[tblk(RWKV_WIDTH), sblk],
        out_shape=[jax.ShapeDtypeStruct((b, l, RWKV_WIDTH), BF16),
                   jax.ShapeDtypeStruct((b, 2, LANES, LANES), F32)],
        scratch_shapes=[pltpu.VMEM((nb, SUBLANES, CP_PAD), F32)] + [slab(BF16)] * 7 + [slab(F32)] * 3,
        compiler_params=_params(("parallel", "arbitrary")),
        name="rwkv",
    )(cp, g, prev, s0p, rp["mu"], rp["w0"], rp["w2"], rp["a0"], rp["a2"], rp["k_k"], rp["k_a"],
      rp["u"], rp["ln_w"], rp["ln_b"],
      (hl == hc).astype(BF16), (same_chunk & (tc <= tr)).astype(BF16), same_chunk.astype(BF16))


def _rwkv_params(mu, w0, w2, a0, a2, k_k, k_a, u, ln_w, ln_b):
    vec = lambda t: t.astype(F32).reshape(1, RWKV_WIDTH)
    mu_p = jnp.zeros((1, CP_PAD), F32).at[0, :RWKV_SHIFT].set(mu.astype(F32))
    w2_p = jnp.zeros((LANES, RWKV_WIDTH), F32).at[:RWKV_LORA].set(w2.astype(F32)).astype(BF16)
    a2_p = jnp.zeros((LANES, RWKV_WIDTH), F32).at[RWKV_LORA:2 * RWKV_LORA].set(a2.astype(F32)).astype(BF16)
    return dict(mu=mu_p, w0=vec(w0), w2=w2_p, a0=vec(a0), a2=a2_p, k_k=vec(k_k), k_a=vec(k_a),
                u=vec(u), ln_w=vec(ln_w), ln_b=vec(ln_b))


def _pair_state(s):
    b = s.shape[0]
    s = s.astype(F32).reshape(b, 2, 2, HEAD_DIM, HEAD_DIM)
    eye = jnp.eye(2, dtype=F32)
    return jnp.einsum("bphvk,hg->bphvgk", s, eye).reshape(b, 2, LANES, LANES)


def _unpair_state(sp):
    b = sp.shape[0]
    s = sp.reshape(b, 2, 2, HEAD_DIM, 2, HEAD_DIM)
    return jnp.stack([s[:, :, 0, :, 0, :], s[:, :, 1, :, 1, :]], axis=2).reshape(b, RWKV_HEADS, HEAD_DIM, HEAD_DIM)


def _outproj_kernel(x_ref, a_ref, b_ref, c_ref, w_ref, o_ref):
    acc = _dot(a_ref[...], w_ref[0:SB_WIDTH, :])
    acc = acc + _dot(b_ref[...], w_ref[SB_WIDTH:SB_WIDTH + SSM_WIDTH, :])
    acc = acc + _dot(c_ref[...], w_ref[SB_WIDTH + SSM_WIDTH:, :])
    o_ref[...] = x_ref[...] + acc


def _outproj(x2d, ma, mb, mc, w_out):
    n = x2d.shape[0]
    tm = min(512, n)
    row = lambda w: pl.BlockSpec((tm, w), lambda i: (i, 0))
    return pl.pallas_call(
        _outproj_kernel,
        grid=(n // tm,),
        in_specs=[row(D_MODEL), row(SB_WIDTH), row(SSM_WIDTH), row(RWKV_WIDTH), _full((D_MODEL, D_MODEL))],
        out_specs=row(D_MODEL),
        out_shape=jax.ShapeDtypeStruct((n, D_MODEL), F32),
        compiler_params=_params(("parallel",)),
        name="outproj",
    )(x2d, ma, mb, mc, w_out)


def _pad_w_in(w_in):
    w = w_in.astype(F32)
    pad = jnp.zeros((D_MODEL, CP_PAD - RWKV_SHIFT), F32)
    return jnp.concatenate([w[:, :COL_CP + RWKV_SHIFT], pad, w[:, COL_CP + RWKV_SHIFT:]], axis=1).astype(BF16)


def _layer(x, lp, past):
    b, l, _ = x.shape
    n = b * l
    x2d = x.reshape(n, D_MODEL)
    q, k, v, ga, u, gb, cp, gc = _inproj(x2d, lp["norm_w"], lp["w_in"], lp["q_w"], lp["k_w"])
    r3 = lambda t: t.reshape(b, l, t.shape[-1])
    q, k, v, ga, u, gb, cp, gc = map(r3, (q, k, v, ga, u, gb, cp, gc))

    if past is None:
        mix_a = _sb_prompt(q, k, v, ga)
        h0r = h0i = jnp.zeros((b, SSM_FLAT), F32)
        prev = jnp.zeros((b, 1, CP_PAD), F32)
        s0p = jnp.zeros((b, 2, LANES, LANES), F32)
    else:
        mix_a = _sb_sample(q, k, v, past["k"], past["v"], past["layer"], ga)
        h0r = past["ssm_re"].astype(F32).reshape(b, SSM_FLAT)
        h0i = past["ssm_im"].astype(F32).reshape(b, SSM_FLAT)
        prev = jnp.pad(past["shift"].astype(F32), ((0, 0), (0, CP_PAD - RWKV_SHIFT)))[:, None, :]
        s0p = _pair_state(past["wkv"])

    mix_b, hr, hi = _s5(u, gb, h0r, h0i, lp["s5"])
    mix_c, s_pair = _rwkv(cp, gc, prev, s0p, lp["rwkv"])

    x_new = _outproj(x2d, mix_a.reshape(n, SB_WIDTH), mix_b.reshape(n, SSM_WIDTH),
                     mix_c.reshape(n, RWKV_WIDTH), lp["w_out"]).reshape(b, l, D_MODEL)
    heads = lambda t: t.reshape(b, l, SB_WIDTH // HEAD_DIM, HEAD_DIM)
    state = (heads(k), heads(v), hr.reshape(b, SSM_GROUPS, SSM_STATE), hi.reshape(b, SSM_GROUPS, SSM_STATE),
             _unpair_state(s_pair), cp[:, -1, :RWKV_SHIFT])
    return x_new, state


def kernel(x_prompt, x_sample, cache_k, cache_v, state_ssm_re, state_ssm_im, state_wkv, state_shift, norm_w, w_in, q_norm_w, k_norm_w, ssm_lambda_re, ssm_lambda_im, ssm_log_dt, ssm_b_re, ssm_b_im, ssm_c_re, ssm_c_im, ssm_d, ssm_w_glu, ssm_b_glu, rwkv_mu, rwkv_w0, rwkv_w2, rwkv_a0, rwkv_a2, rwkv_k_k, rwkv_k_a, rwkv_u, rwkv_ln_w, rwkv_ln_b, w_out):
    depth = w_in.shape[0]
    xp, xs = x_prompt, x_sample
    ck_all = cache_k.reshape(cache_k.shape[:3] + (SB_WIDTH,))
    cv_all = cache_v.reshape(cache_v.shape[:3] + (SB_WIDTH,))
    p_states, s_states = [], []
    for i in range(depth):
        tile8 = lambda t: jnp.tile(t.astype(F32), SB_WIDTH // HEAD_DIM).reshape(1, SB_WIDTH)
        lp = dict(
            norm_w=norm_w[i].astype(F32).reshape(1, D_MODEL), w_in=_pad_w_in(w_in[i]),
            q_w=tile8(q_norm_w[i]), k_w=tile8(k_norm_w[i]),
            s5=_s5_params(ssm_lambda_re[i], ssm_lambda_im[i], ssm_log_dt[i], ssm_b_re[i], ssm_b_im[i],
                          ssm_c_re[i], ssm_c_im[i], ssm_d[i], ssm_w_glu[i], ssm_b_glu[i]),
            rwkv=_rwkv_params(rwkv_mu[i], rwkv_w0[i], rwkv_w2[i], rwkv_a0[i], rwkv_a2[i], rwkv_k_k[i],
                              rwkv_k_a[i], rwkv_u[i].reshape(-1), rwkv_ln_w[i], rwkv_ln_b[i]),
            w_out=w_out[i].astype(BF16))
        xp, st_p = _layer(xp, lp, None)
        past = dict(k=ck_all, v=cv_all, layer=i,
                    ssm_re=state_ssm_re[i], ssm_im=state_ssm_im[i], wkv=state_wkv[i], shift=state_shift[i])
        xs, st_s = _layer(xs, lp, past)
        p_states.append(st_p)
        s_states.append(st_s)
    stack = lambda states, j: jnp.stack([s[j] for s in states], axis=0)
    return ((xp, xs) + tuple(stack(p_states, j) for j in range(6))
            + tuple(stack(s_states, j) for j in range(6)))
```

```python
import functools
import math

import jax
import jax.numpy as jnp
from jax import lax
from jax.experimental import pallas as pl
from jax.experimental.pallas import tpu as pltpu

F32 = jnp.float32
BF16 = jnp.bfloat16

D_MODEL = 1024
HEAD_DIM = 64
SB_WIDTH = D_MODEL // 2
SSM_WIDTH = D_MODEL // 4
SSM_GROUP = 16
SSM_GROUPS = SSM_WIDTH // SSM_GROUP
SSM_STATE = 64
SSM_FLAT = SSM_GROUPS * SSM_STATE
RWKV_WIDTH = D_MODEL // 4
RWKV_HEADS = RWKV_WIDTH // HEAD_DIM
RWKV_LORA = 32
RWKV_SHIFT = 3 * RWKV_WIDTH + 2 * RWKV_LORA
RMS_EPS = 1e-6
GN_EPS = 64e-5
DECAY_SCALE = math.exp(-0.5)

LANES = 128
SUBLANES = 8
CP_PAD = 7 * LANES
COL_Q, COL_K, COL_V, COL_GA = 0, 512, 1024, 1536
COL_U, COL_GB, COL_CP, COL_GC = 2048, 2304, 2560, 2560 + CP_PAD
D_IN_PAD = COL_GC + RWKV_WIDTH
VMEM_LIMIT = 56 * 1024 * 1024

NN = (((1,), (0,)), ((), ()))
NT = (((1,), (1,)), ((), ()))
TN = (((0,), (0,)), ((), ()))


def _dot(a, b, dims=NN):
    return lax.dot_general(a, b, dims, preferred_element_type=F32)


def _split(a):
    hi = a.astype(BF16)
    lo = (a - hi.astype(F32)).astype(BF16)
    return hi, lo


def _dot2l(a, b_exact, dims=NN):
    ah, al = _split(a)
    return _dot(ah, b_exact, dims) + _dot(al, b_exact, dims)


def _dot2r(a_exact, b, dims=NN):
    bh, bl = _split(b)
    return _dot(a_exact, bh, dims) + _dot(a_exact, bl, dims)


def _params(sem):
    return pltpu.CompilerParams(dimension_semantics=sem, vmem_limit_bytes=VMEM_LIMIT)


def _full(shape):
    nd = len(shape)
    return pl.BlockSpec(shape, lambda *_: (0,) * nd)


def _head_rms(t, w):
    lane = lax.broadcasted_iota(jnp.int32, (1, LANES), 1)
    lo = lane < HEAD_DIM
    outs = []
    for j in range(t.shape[1] // LANES):
        blk = t[:, j * LANES:(j + 1) * LANES]
        sq = blk * blk
        s_lo = jnp.sum(jnp.where(lo, sq, 0.0), axis=-1, keepdims=True)
        s_hi = jnp.sum(jnp.where(lo, 0.0, sq), axis=-1, keepdims=True)
        ms = jnp.where(lo, s_lo, s_hi) * (1.0 / HEAD_DIM)
        outs.append(blk * lax.rsqrt(ms + RMS_EPS) * w[:, j * LANES:(j + 1) * LANES])
    return jnp.concatenate(outs, axis=-1)


def _inproj_kernel(x_ref, nw_ref, w_ref, qw_ref, kw_ref,
                   q_o, k_o, v_o, ga_o, u_o, gb_o, cp_o, gc_o):
    x = x_ref[...]
    ms = jnp.mean(x * x, axis=-1, keepdims=True)
    h = (x * lax.rsqrt(ms + RMS_EPS) * nw_ref[...]).astype(BF16)

    def proj(a, b):
        return _dot(h, w_ref[:, a:b])

    q = _head_rms(proj(COL_Q, COL_K), qw_ref[...])
    q_o[...] = (q * (HEAD_DIM ** -0.5)).astype(BF16)
    k_o[...] = _head_rms(proj(COL_K, COL_V), kw_ref[...])
    v_o[...] = proj(COL_V, COL_GA)
    ga_o[...] = proj(COL_GA, COL_U)
    u_o[...] = proj(COL_U, COL_GB)
    gb_o[...] = proj(COL_GB, COL_CP)
    cp_o[...] = proj(COL_CP, COL_GC)
    gc_o[...] = proj(COL_GC, D_IN_PAD)


def _inproj(x2d, nw, w_pad, qw, kw):
    n = x2d.shape[0]
    tm = min(256, n)
    widths = (SB_WIDTH, SB_WIDTH, SB_WIDTH, SB_WIDTH, SSM_WIDTH, SSM_WIDTH, CP_PAD, RWKV_WIDTH)
    dtypes = (BF16,) + (F32,) * 7
    row = lambda w: pl.BlockSpec((tm, w), lambda i: (i, 0))
    return pl.pallas_call(
        _inproj_kernel,
        grid=(n // tm,),
        in_specs=[row(D_MODEL), _full((1, D_MODEL)), _full((D_MODEL, D_IN_PAD)),
                  _full((1, SB_WIDTH)), _full((1, SB_WIDTH))],
        out_specs=[row(w) for w in widths],
        out_shape=[jax.ShapeDtypeStruct((n, w), dt) for w, dt in zip(widths, dtypes)],
        compiler_params=_params(("parallel",)),
        name="inproj",
    )(x2d, nw, w_pad, qw, kw)


SB_TK = 128
SB_PAIRS = SB_WIDTH // LANES


SB_U = 2
LOG2E = math.log2(math.e)


def _sb_weights(qs, kbs, carries, tri, masks):
    return _sb_weights_of(_sb_scores(qs, kbs), carries, tri, masks)


def _sb_scores(qs, kbs):
    z = []
    for q, kb in zip(qs, kbs):
        tk = kb[0].shape[0]
        zp = _dot(q, jnp.concatenate(kb, axis=0), NT)
        z.append([zp[:, u * tk:(u + 1) * tk] for u in range(len(kb))])
    return z


def _sb_weights_of(scores, carries, tri, masks):
    pu = [(p, u) for p in range(len(scores)) for u in range(len(scores[p]))]
    z = {k: scores[k[0]][k[1]] for k in pu}
    hl = {}
    for k in pu:
        sp = jnp.maximum(z[k], 0.0) + jnp.log(1.0 + jnp.exp2(jnp.abs(z[k]) * (-LOG2E)))
        if masks[k[1]] is not None:
            sp = jnp.where(masks[k[1]], sp, 0.0)
        hl[k] = sp.astype(BF16)
    cs = {k: _dot(hl[k], tri) for k in pu}
    weights, new_carries = [], []
    for p in range(len(scores)):
        carry, row = carries[p], []
        for u in range(len(scores[p])):
            incl = cs[(p, u)] + carry
            a = jnp.exp2((z[(p, u)] - incl) * LOG2E)
            if masks[u] is not None:
                a = jnp.where(masks[u], a, 0.0)
            row.append(a.astype(BF16))
            carry = jnp.broadcast_to(incl[:, 0:1], incl.shape)
        weights.append(row)
        new_carries.append(carry)
    return weights, new_carries


def _sb_values(a2, vbs, head_lo):
    parts = []
    for vb in vbs:
        zero = jnp.zeros_like(vb)
        parts += [jnp.where(head_lo, vb, zero), jnp.where(head_lo, zero, vb)]
    return _dot(a2, jnp.concatenate(parts, axis=0))


def _sb_all_pairs(q, diag_kv, n_diag, past_k, past_v, n_iter, tri, qs_s, a_s, acc_s, carry_s):
    tq, tk = q.shape[0], SB_TK
    head_lo = lax.broadcasted_iota(jnp.int32, (1, LANES), 1) < HEAD_DIM
    pairs = range(SB_PAIRS)
    qs = []
    for p in pairs:
        qp = q[:, p * LANES:(p + 1) * LANES]
        zero = jnp.zeros_like(qp)
        qs.append(jnp.concatenate([jnp.where(head_lo, qp, zero), jnp.where(head_lo, zero, qp)], axis=0))
        qs_s[p] = qs[p]
    rg = tq // n_diag
    row = lax.broadcasted_iota(jnp.int32, (2 * rg, tk), 0)
    col = lax.broadcasted_iota(jnp.int32, (2 * rg, tk), 1)
    causal = col < jnp.where(row >= rg, row - rg, row)
    diag = [[diag_kv(d, p) for d in range(n_diag)] for p in pairs]
    groups = [(p, r) for p in pairs for r in range(n_diag)]
    rows_of = lambda x, r: jnp.concatenate([x[r * rg:(r + 1) * rg], x[tq + r * rg:tq + (r + 1) * rg]], axis=0)
    ws, carries = _sb_weights([rows_of(qs[p], r) for p, r in groups],
                              [[diag[p][d][0] for d in range(r, -1, -1)] for p, r in groups],
                              [jnp.zeros((2 * rg, tk), F32)] * len(groups), tri, [causal] + [None] * (n_diag - 1))
    for p in pairs:
        accs, c_lo, c_hi = [], [], []
        for r in range(n_diag):
            g = groups.index((p, r))
            a2 = jnp.concatenate([half for a in ws[g] for half in (a[:rg], a[rg:])], axis=1)
            accs.append(_sb_values(a2, [diag[p][d][1] for d in range(r, -1, -1)], head_lo))
            c_lo.append(carries[g][:rg])
            c_hi.append(carries[g][rg:])
        acc_s[p] = jnp.concatenate(accs, axis=0)
        carry_s[p] = jnp.concatenate(c_lo + c_hi, axis=0)
    a_s[...] = jnp.zeros_like(a_s)

    def values(prev_start):
        for p in pairs:
            vbs = [past_v(prev_start + (SB_U - 1 - u) * tk, p) for u in range(SB_U)]
            acc_s[p] += _sb_values(a_s[p], vbs, head_lo)

    span = SB_U * tk

    def step(it):
        start = (n_iter - 1 - it) * span
        prev_start = jnp.where(it == 0, 0, (n_iter - it) * span)
        kbs = [[past_k(start + (SB_U - 1 - u) * tk, p) for u in range(SB_U)] for p in pairs]
        scores = _sb_scores([qs_s[p] for p in pairs], kbs)
        values(prev_start)
        ws, carries = _sb_weights_of(scores, [carry_s[p] for p in pairs], tri, [None] * SB_U)
        for p in pairs:
            for u in range(SB_U):
                a_s[p, :, (2 * u) * tk:(2 * u + 1) * tk] = ws[p][u][:tq]
                a_s[p, :, (2 * u + 1) * tk:(2 * u + 2) * tk] = ws[p][u][tq:]
            carry_s[p] = carries[p]

    odd = n_iter % 2

    @pl.when(odd == 1)
    def _():
        step(jnp.int32(0))

    def two_steps(i, c):
        step(odd + 2 * i)
        step(odd + 2 * i + 1)
        return c

    lax.fori_loop(0, n_iter // 2, two_steps, 0)
    values(0)
    return jnp.concatenate([acc_s[p] for p in range(SB_PAIRS)], axis=1)


def _sb_tri():
    j = lax.broadcasted_iota(jnp.int32, (SB_TK, SB_TK), 0)
    s = lax.broadcasted_iota(jnp.int32, (SB_TK, SB_TK), 1)
    return (j >= s).astype(BF16)


def _sb_scratch(tq):
    return [pltpu.VMEM((SB_PAIRS, 2 * tq, LANES), BF16), pltpu.VMEM((SB_PAIRS, tq, SB_U * 2 * SB_TK), BF16),
            pltpu.VMEM((SB_PAIRS, tq, LANES), F32), pltpu.VMEM((SB_PAIRS, 2 * tq, SB_TK), F32)]


def _sb_block_reader(ref2d):
    def read(start, p):
        if not isinstance(start, int):
            start = pl.multiple_of(start, SB_TK)
        return ref2d[pl.ds(start, SB_TK), p * LANES:(p + 1) * LANES].astype(BF16)
    return read


def _sb_prompt_kernel(q_ref, k_ref, v_ref, g_ref, tri_ref, o_ref, qs_s, a_s, acc_s, carry_s, *, tq):
    qi = pl.program_id(1)
    base = pl.multiple_of(qi * tq, tq)
    read_k, read_v = _sb_block_reader(k_ref.at[0]), _sb_block_reader(v_ref.at[0])

    def diag_kv(d, p):
        return read_k(base + d * SB_TK, p), read_v(base + d * SB_TK, p)

    n_iter = qi * (tq // (SB_U * SB_TK))
    y = _sb_all_pairs(q_ref[0], diag_kv, tq // SB_TK, read_k, read_v, n_iter, tri_ref[...],
                      qs_s, a_s, acc_s, carry_s)
    o_ref[0] = (y * jax.nn.silu(g_ref[0])).astype(BF16)


def _sb_prompt(q, k, v, g):
    b, l, _ = q.shape
    tq = SB_U * SB_TK
    qblk = pl.BlockSpec((1, tq, SB_WIDTH), lambda bi, qi: (bi, qi, 0))
    kvblk = pl.BlockSpec((1, l, SB_WIDTH), lambda bi, qi: (bi, 0, 0))
    return pl.pallas_call(
        functools.partial(_sb_prompt_kernel, tq=tq),
        grid=(b, l // tq),
        in_specs=[qblk, kvblk, kvblk, qblk, _full((SB_TK, SB_TK))],
        out_specs=qblk,
        out_shape=jax.ShapeDtypeStruct((b, l, SB_WIDTH), BF16),
        scratch_shapes=_sb_scratch(tq),
        compiler_params=_params(("parallel", "arbitrary")),
        name="sb_prompt",
    )(q, k, v, g, _sb_tri())


def _sb_sample_kernel(q_ref, k_ref, v_ref, ck_ref, cv_ref, g_ref, tri_ref, o_ref, qs_s, a_s, acc_s, carry_s,
                      *, n_iter):
    tq = q_ref.shape[1]

    def diag_kv(d, p):
        ls = slice(p * LANES, (p + 1) * LANES)
        pad = jnp.zeros((SB_TK - tq, LANES), BF16)
        return (jnp.concatenate([k_ref[0, :, ls].astype(BF16), pad], axis=0),
                jnp.concatenate([v_ref[0, :, ls].astype(BF16), pad], axis=0))

    y = _sb_all_pairs(q_ref[0], diag_kv, 1, _sb_block_reader(ck_ref.at[0, 0]), _sb_block_reader(cv_ref.at[0, 0]),
                      n_iter, tri_ref[...], qs_s, a_s, acc_s, carry_s)
    o_ref[0] = (y * jax.nn.silu(g_ref[0])).astype(BF16)


def _sb_sample(q, k, v, ck, cv, layer, g):
    b, l, _ = q.shape
    past = ck.shape[2]
    qblk = pl.BlockSpec((1, l, SB_WIDTH), lambda bi: (bi, 0, 0))
    cblk = pl.BlockSpec((1, 1, past, SB_WIDTH), lambda bi: (layer, bi, 0, 0))
    return pl.pallas_call(
        functools.partial(_sb_sample_kernel, n_iter=past // (SB_U * SB_TK)),
        grid=(b,),
        in_specs=[qblk, qblk, qblk, cblk, cblk, qblk, _full((SB_TK, SB_TK))],
        out_specs=qblk,
        out_shape=jax.ShapeDtypeStruct((b, l, SB_WIDTH), BF16),
        scratch_shapes=_sb_scratch(l),
        compiler_params=_params(("parallel",)),
        name="sb_sample",
    )(q, k, v, ck, cv, g, _sb_tri())


def _s5_kernel(u_ref, g_ref, h0r_ref, h0i_ref, ar_ref, ai_ref, b_ref, c_ref, d_ref,
               wg_ref, bg_ref, o_ref, hr_o, hi_o, bu_s, h_s, ut_s, gt_s, ot_s, *, tt):
    ti = pl.program_id(1)
    rows = tt * SUBLANES
    slabs = SSM_WIDTH // LANES

    @pl.when(ti == 0)
    def _():
        hr_o[...] = h0r_ref[...]
        hi_o[...] = h0i_ref[...]

    def time_major(ref, slab_s):
        for s in range(SUBLANES):
            for j in range(slabs):
                slab_s[j, pl.ds(s, tt, stride=SUBLANES), :] = ref[s, :, j * LANES:(j + 1) * LANES]
        return jnp.concatenate([slab_s[j] for j in range(slabs)], axis=1)

    u = time_major(u_ref, ut_s)
    bu_s[...] = _dot(u.astype(BF16), b_ref[...])

    ar = jnp.broadcast_to(ar_ref[...], (SUBLANES, SSM_FLAT))
    ai = jnp.broadcast_to(ai_ref[...], (SUBLANES, SSM_FLAT))

    def step(t, carry):
        hr, hi = carry
        r0 = pl.multiple_of(t * SUBLANES, SUBLANES)
        br = bu_s[pl.ds(r0, SUBLANES), 0:SSM_FLAT]
        bi = bu_s[pl.ds(r0, SUBLANES), SSM_FLAT:2 * SSM_FLAT]
        nr = ar * hr - ai * hi + br
        ni = ar * hi + ai * hr + bi
        h_s[pl.ds(r0, SUBLANES), 0:SSM_FLAT] = nr
        h_s[pl.ds(r0, SUBLANES), SSM_FLAT:2 * SSM_FLAT] = ni
        return nr, ni

    hr, hi = lax.fori_loop(0, tt, step, (hr_o[...], hi_o[...]))
    hr_o[...] = hr
    hi_o[...] = hi

    y = _dot(h_s[...].astype(BF16), c_ref[...]) + d_ref[...] * u
    y = jax.nn.gelu(y)
    z = _dot(y.astype(BF16), wg_ref[...]) + bg_ref[...]
    out = z[:, :SSM_WIDTH] * jax.nn.sigmoid(z[:, SSM_WIDTH:])
    out = out * jax.nn.silu(time_major(g_ref, gt_s))
    for j in range(slabs):
        ot_s[j] = out[:, j * LANES:(j + 1) * LANES]
    for s in range(SUBLANES):
        for j in range(slabs):
            o_ref[s, :, j * LANES:(j + 1) * LANES] = ot_s[j, pl.ds(s, tt, stride=SUBLANES), :].astype(BF16)


def _s5(u, g, h0r, h0i, sp):
    b, l, _ = u.shape
    tt = min(128, l)
    rows = tt * SUBLANES
    ublk = pl.BlockSpec((SUBLANES, tt, SSM_WIDTH), lambda bi, ti: (bi, ti, 0))
    sblk = pl.BlockSpec((SUBLANES, SSM_FLAT), lambda bi, ti: (bi, 0))
    slab = pltpu.VMEM((SSM_WIDTH // LANES, rows, LANES), F32)
    return pl.pallas_call(
        functools.partial(_s5_kernel, tt=tt),
        grid=(b // SUBLANES, l // tt),
        in_specs=[ublk, ublk, sblk, sblk,
                  _full((1, SSM_FLAT)), _full((1, SSM_FLAT)),
                  _full((SSM_WIDTH, 2 * SSM_FLAT)),
                  _full((2 * SSM_FLAT, SSM_WIDTH)), _full((1, SSM_WIDTH)),
                  _full((SSM_WIDTH, 2 * SSM_WIDTH)), _full((1, 2 * SSM_WIDTH))],
        out_specs=[ublk, sblk, sblk],
        out_shape=[jax.ShapeDtypeStruct((b, l, SSM_WIDTH), BF16),
                   jax.ShapeDtypeStruct((b, SSM_FLAT), F32),
                   jax.ShapeDtypeStruct((b, SSM_FLAT), F32)],
        scratch_shapes=[pltpu.VMEM((rows, 2 * SSM_FLAT), F32), pltpu.VMEM((rows, 2 * SSM_FLAT), F32),
                        slab, slab, slab],
        compiler_params=_params(("parallel", "arbitrary")),
        name="s5",
    )(u, g, h0r, h0i, sp["ar"], sp["ai"], sp["b"], sp["c"], sp["d"], sp["w_glu"], sp["b_glu"])


def _s5_params(lam_re, lam_im, log_dt, b_re, b_im, c_re, c_im, d, w_glu, b_glu):
    dt = jnp.exp(log_dt.astype(F32))[:, None]
    lr = jnp.minimum(lam_re.astype(F32), -1e-4)
    li = lam_im.astype(F32)
    er = jnp.exp(lr * dt)
    ar, ai = er * jnp.cos(li * dt), er * jnp.sin(li * dt)
    den = lr * lr + li * li
    fr = ((ar - 1.0) * lr + ai * li) / den
    fi = (ai * lr - (ar - 1.0) * li) / den
    br, bi = b_re.astype(F32), b_im.astype(F32)
    bbr = fr[..., None] * br - fi[..., None] * bi
    bbi = fr[..., None] * bi + fi[..., None] * br
    eye = jnp.eye(SSM_GROUPS, dtype=F32)
    bd_in = lambda m: jnp.einsum("gpc,gh->gchp", m, eye).reshape(SSM_WIDTH, SSM_FLAT)
    bd_out = lambda m: jnp.einsum("gcp,gh->gphc", m, eye).reshape(SSM_FLAT, SSM_WIDTH)
    b_all = jnp.concatenate([bd_in(bbr), bd_in(bbi)], axis=1)
    c_all = jnp.concatenate([bd_out(c_re.astype(F32)), -bd_out(c_im.astype(F32))], axis=0)
    return dict(ar=ar.reshape(1, SSM_FLAT), ai=ai.reshape(1, SSM_FLAT), b=b_all.astype(BF16),
                c=c_all.astype(BF16), d=d.astype(F32).reshape(1, SSM_WIDTH),
                w_glu=w_glu.astype(BF16), b_glu=b_glu.astype(F32).reshape(1, 2 * SSM_WIDTH))


RWKV_CK = 64
RWKV_NB = 8


def _rwkv_kernel(cp_ref, g_ref, prev_ref, s0_ref, mu_ref, w0_ref, w2_ref, a0_ref, a2_ref, kk_ref, ka_ref,
                 ub_ref, lnw_ref, lnb_ref, ho_ref, tri_ref, ones_ref, o_ref, s_o, prev_s,
                 khw_s, rw_s, ktl_s, btl_s, v_s, kht_s, bht_s, wch_s, bonus_s, y_s, *, tt, ck):
    ti = pl.program_id(1)

    @pl.when(ti == 0)
    def _():
        s_o[...] = s0_ref[...]
        prev_s[...] = jnp.broadcast_to(prev_ref[...], prev_s.shape)

    w = RWKV_WIDTH
    nb_all = cp_ref.shape[0]
    head_ones, chunk_tri, chunk_ones = ho_ref[...], tri_ref[...], ones_ref[...]

    def prep(nb, carry):
        cp = cp_ref[nb]
        trow = lax.broadcasted_iota(jnp.int32, (tt, 1), 0)
        shifted = jnp.where(trow == 0, prev_s[nb, 0:1, :], pltpu.roll(cp, 1, 0))
        prev_s[nb] = jnp.broadcast_to(cp[tt - 1:tt, :], (SUBLANES, CP_PAD))
        xc = cp + mu_ref[...] * (shifted - cp)
        r, k, v, lora = xc[:, 0:w], xc[:, w:2 * w], xc[:, 2 * w:3 * w], xc[:, 3 * w:]
        logw = -DECAY_SCALE * jax.nn.sigmoid(w0_ref[...] + _dot(jnp.tanh(lora).astype(BF16), w2_ref[...]))
        a = jax.nn.sigmoid(a0_ref[...] + _dot(lora.astype(BF16), a2_ref[...]))
        cum = _dot2r(chunk_tri, logw)
        tot = _dot2r(chunk_ones, logw)
        kkv = k * kk_ref[...]
        kh = kkv * lax.rsqrt(_dot2l(kkv * kkv, head_ones) + 1e-12)
        kt = k * (1.0 + (a - 1.0) * ka_ref[...])
        bvec = a * kh
        w_inv = jnp.exp(-cum)
        w_end = jnp.exp(tot - cum)
        khw_s[nb] = (kh * jnp.exp(cum - logw)).astype(BF16)
        rw_s[nb] = (r * jnp.exp(cum)).astype(BF16)
        ktl_s[nb] = (kt * w_inv).astype(BF16)
        btl_s[nb] = (bvec * w_inv).astype(BF16)
        v_s[nb] = v.astype(BF16)
        kht_s[nb] = (kt * w_end).astype(BF16)
        bht_s[nb] = (bvec * w_end).astype(BF16)
        wch_s[nb] = jnp.exp(tot)
        bonus_s[nb] = _dot2l(r * kt * ub_ref[...], head_ones) * v
        return carry

    lax.fori_loop(0, nb_all, prep, 0)

    lane = lax.broadcasted_iota(jnp.int32, (1, LANES), 1)
    m_lo = lane < HEAD_DIM
    ri = lax.broadcasted_iota(jnp.int32, (2 * ck, 2 * ck), 0)
    cj = lax.broadcasted_iota(jnp.int32, (2 * ck, 2 * ck), 1)
    same_head = (ri >= ck) == (cj >= ck)
    strict = same_head & (cj < ri)
    lower = same_head & (cj <= ri)
    eye = (ri == cj).astype(F32)
    rr = lax.broadcasted_iota(jnp.int32, (2 * ck, LANES), 0)
    rl = lax.broadcasted_iota(jnp.int32, (2 * ck, LANES), 1)
    own_lanes = (rr >= ck) == (rl >= HEAD_DIM)
    sr = lax.broadcasted_iota(jnp.int32, (LANES, LANES), 0) // HEAD_DIM
    sc = lax.broadcasted_iota(jnp.int32, (LANES, LANES), 1) // HEAD_DIM
    blockdiag = sr == sc
    dot1 = lambda x, y: _dot(x.astype(BF16), y.astype(BF16))

    cat = lambda xs: jnp.concatenate(xs, axis=0)

    def chunk_step(c, carry):
        r0 = pl.multiple_of(c * ck, ck)
        rows = pl.ds(r0, ck)
        chains = [(nb, hp) for nb in range(nb_all) for hp in range(w // LANES)]
        lanes = lambda hp: slice(hp * LANES, (hp + 1) * LANES)
        rd = lambda ref: [ref[nb, rows, lanes(hp)] for nb, hp in chains]
        khw, rw, ktl, btl, v = rd(khw_s), rd(rw_s), rd(ktl_s), rd(btl_s), rd(v_s)
        z = jnp.zeros_like(khw[0])
        l4 = [cat([jnp.where(m_lo, a, z), jnp.where(m_lo, z, a), jnp.where(m_lo, b, z), jnp.where(m_lo, z, b)])
              for a, b in zip(khw, rw)]
        scb = [_dot(a, cat([b, b]), NT) for a, b in zip(l4, btl)]
        sck = [_dot(a, cat([b, b]), NT) for a, b in zip(l4, ktl)]
        nmat = [jnp.where(strict, a[:2 * ck], 0.0) for a in scb]
        arb = [jnp.where(lower, a[2 * ck:], 0.0) for a in scb]
        akk = [jnp.where(strict, a[:2 * ck], 0.0) for a in sck]
        ark = [jnp.where(lower, a[2 * ck:], 0.0) for a in sck]
        inv = [eye - a for a in nmat]
        pw = nmat
        steps = 1
        while 2 * steps < ck:
            pw = [dot1(a, a) for a in pw]
            inv = [dot1(a, eye + b) for a, b in zip(inv, pw)]
            steps *= 2
        s_pair = [s_o[nb, hp] for nb, hp in chains]
        ks = [_dot(cat([a, b]), s.astype(BF16), NT) for a, b, s in zip(khw, rw, s_pair)]
        av = [dot1(cat([a, b]), cat([x, x])) for a, b, x in zip(akk, ark, v)]
        sa2 = [jnp.where(own_lanes, dot1(t, cat([k[:ck], k[:ck]]) + a[:2 * ck]), 0.0)
               for t, k, a in zip(inv, ks, av)]
        y2 = [jnp.where(own_lanes, a[2 * ck:] - dot1(b, s), 0.0) for a, b, s in zip(av, arb, sa2)]
        for (nb, hp), k, y in zip(chains, ks, y2):
            y_s[nb, rows, lanes(hp)] = k[ck:] + y[:ck] + y[ck:]
        kht, bht = rd(kht_s), rd(bht_s)
        upd = [_dot(cat([x, (-(s[:ck] + s[ck:])).astype(BF16)]), cat([a, b]), TN)
               for x, s, a, b in zip(v, sa2, kht, bht)]
        for (nb, hp), s, u in zip(chains, s_pair, upd):
            s_o[nb, hp] = s * wch_s[nb, pl.ds(r0, 1), lanes(hp)] + jnp.where(blockdiag, u, 0.0)
        return carry

    lax.fori_loop(0, tt // ck, chunk_step, 0)

    fin = 2 if nb_all % 2 == 0 else 1

    def finish(i, carry):
        nbs = [i * fin + j for j in range(fin)]
        ys = [y_s[nb] for nb in nbs]
        mean = [_dot2l(y, head_ones) * (1.0 / HEAD_DIM) for y in ys]
        cen = [y - m for y, m in zip(ys, mean)]
        var = [_dot2l(c * c, head_ones) * (1.0 / HEAD_DIM) for c in cen]
        for nb, c, v in zip(nbs, cen, var):
            y = c * lax.rsqrt(v + GN_EPS) * lnw_ref[...] + lnb_ref[...] + bonus_s[nb]
            o_ref[nb] = (y * jax.nn.silu(g_ref[nb])).astype(BF16)
        return carry

    lax.fori_loop(0, nb_all // fin, finish, 0)


def _rwkv(cp, g, prev, s0p, rp):
    b, l, _ = cp.shape
    tt = min(256, l)
    ck = min(RWKV_CK, tt)
    nb = min(RWKV_NB, b)
    tblk = lambda w: pl.BlockSpec((nb, tt, w), lambda bi, ti: (bi, ti, 0))
    sblk = pl.BlockSpec((nb, 2, LANES, LANES), lambda bi, ti: (bi, 0, 0, 0))
    vec = _full((1, RWKV_WIDTH))
    hl = lax.broadcasted_iota(jnp.int32, (RWKV_WIDTH, RWKV_WIDTH), 0) // HEAD_DIM
    hc = lax.broadcasted_iota(jnp.int32, (RWKV_WIDTH, RWKV_WIDTH), 1) // HEAD_DIM
    tr = lax.broadcasted_iota(jnp.int32, (tt, tt), 0)
    tc = lax.broadcasted_iota(jnp.int32, (tt, tt), 1)
    same_chunk = (tr // ck) == (tc // ck)
    slab = lambda dt: pltpu.VMEM((nb, tt, RWKV_WIDTH), dt)
    return pl.pallas_call(
        functools.partial(_rwkv_kernel, tt=tt, ck=ck),
        grid=(b // nb, l // tt),
        in_specs=[tblk(CP_PAD), tblk(RWKV_WIDTH),
                  pl.BlockSpec((nb, 1, CP_PAD), lambda bi, ti: (bi, 0, 0)), sblk,
                  _full((1, CP_PAD)), vec, _full((LANES, RWKV_WIDTH)), vec, _full((LANES, RWKV_WIDTH)),
                  vec, vec, vec, vec, vec,
                  _full((RWKV_WIDTH, RWKV_WIDTH)), _full((tt, tt)), _full((tt, tt))],
        out_specs=[tblk(RWKV_WIDTH), sblk],
        out_shape=[jax.ShapeDtypeStruct((b, l, RWKV_WIDTH), BF16),
                   jax.ShapeDtypeStruct((b, 2, LANES, LANES), F32)],
        scratch_shapes=[pltpu.VMEM((nb, SUBLANES, CP_PAD), F32)] + [slab(BF16)] * 7 + [slab(F32)] * 3,
        compiler_params=_params(("parallel", "arbitrary")),
        name="rwkv",
    )(cp, g, prev, s0p, rp["mu"], rp["w0"], rp["w2"], rp["a0"], rp["a2"], rp["k_k"], rp["k_a"],
      rp["u"], rp["ln_w"], rp["ln_b"],
      (hl == hc).astype(BF16), (same_chunk & (tc <= tr)).astype(BF16), same_chunk.astype(BF16))


def _rwkv_params(mu, w0, w2, a0, a2, k_k, k_a, u, ln_w, ln_b):
    vec = lambda t: t.astype(F32).reshape(1, RWKV_WIDTH)
    mu_p = jnp.zeros((1, CP_PAD), F32).at[0, :RWKV_SHIFT].set(mu.astype(F32))
    w2_p = jnp.zeros((LANES, RWKV_WIDTH), F32).at[:RWKV_LORA].set(w2.astype(F32)).astype(BF16)
    a2_p = jnp.zeros((LANES, RWKV_WIDTH), F32).at[RWKV_LORA:2 * RWKV_LORA].set(a2.astype(F32)).astype(BF16)
    return dict(mu=mu_p, w0=vec(w0), w2=w2_p, a0=vec(a0), a2=a2_p, k_k=vec(k_k), k_a=vec(k_a),
                u=vec(u), ln_w=vec(ln_w), ln_b=vec(ln_b))


def _pair_state(s):
    b = s.shape[0]
    s = s.astype(F32).reshape(b, 2, 2, HEAD_DIM, HEAD_DIM)
    eye = jnp.eye(2, dtype=F32)
    return jnp.einsum("bphvk,hg->bphvgk", s, eye).reshape(b, 2, LANES, LANES)


def _unpair_state(sp):
    b = sp.shape[0]
    s = sp.reshape(b, 2, 2, HEAD_DIM, 2, HEAD_DIM)
    return jnp.stack([s[:, :, 0, :, 0, :], s[:, :, 1, :, 1, :]], axis=2).reshape(b, RWKV_HEADS, HEAD_DIM, HEAD_DIM)


def _outproj_kernel(x_ref, a_ref, b_ref, c_ref, w_ref, o_ref):
    acc = _dot(a_ref[...], w_ref[0:SB_WIDTH, :])
    acc = acc + _dot(b_ref[...], w_ref[SB_WIDTH:SB_WIDTH + SSM_WIDTH, :])
    acc = acc + _dot(c_ref[...], w_ref[SB_WIDTH + SSM_WIDTH:, :])
    o_ref[...] = x_ref[...] + acc


def _outproj(x2d, ma, mb, mc, w_out):
    n = x2d.shape[0]
    tm = min(512, n)
    row = lambda w: pl.BlockSpec((tm, w), lambda i: (i, 0))
    return pl.pallas_call(
        _outproj_kernel,
        grid=(n // tm,),
        in_specs=[row(D_MODEL), row(SB_WIDTH), row(SSM_WIDTH), row(RWKV_WIDTH), _full((D_MODEL, D_MODEL))],
        out_specs=row(D_MODEL),
        out_shape=jax.ShapeDtypeStruct((n, D_MODEL), F32),
        compiler_params=_params(("parallel",)),
        name="outproj",
    )(x2d, ma, mb, mc, w_out)


def _pad_w_in(w_in):
    w = w_in.astype(F32)
    pad = jnp.zeros((D_MODEL, CP_PAD - RWKV_SHIFT), F32)
    return jnp.concatenate([w[:, :COL_CP + RWKV_SHIFT], pad, w[:, COL_CP + RWKV_SHIFT:]], axis=1).astype(BF16)


def _layer(x, lp, past):
    b, l, _ = x.shape
    n = b * l
    x2d = x.reshape(n, D_MODEL)
    q, k, v, ga, u, gb, cp, gc = _inproj(x2d, lp["norm_w"], lp["w_in"], lp["q_w"], lp["k_w"])
    r3 = lambda t: t.reshape(b, l, t.shape[-1])
    q, k, v, ga, u, gb, cp, gc = map(r3, (q, k, v, ga, u, gb, cp, gc))

    if past is None:
        mix_a = _sb_prompt(q, k, v, ga)
        h0r = h0i = jnp.zeros((b, SSM_FLAT), F32)
        prev = jnp.zeros((b, 1, CP_PAD), F32)
        s0p = jnp.zeros((b, 2, LANES, LANES), F32)
    else:
        mix_a = _sb_sample(q, k, v, past["k"], past["v"], past["layer"], ga)
        h0r = past["ssm_re"].astype(F32).reshape(b, SSM_FLAT)
        h0i = past["ssm_im"].astype(F32).reshape(b, SSM_FLAT)
        prev = jnp.pad(past["shift"].astype(F32), ((0, 0), (0, CP_PAD - RWKV_SHIFT)))[:, None, :]
        s0p = _pair_state(past["wkv"])

    mix_b, hr, hi = _s5(u, gb, h0r, h0i, lp["s5"])
    mix_c, s_pair = _rwkv(cp, gc, prev, s0p, lp["rwkv"])

    x_new = _outproj(x2d, mix_a.reshape(n, SB_WIDTH), mix_b.reshape(n, SSM_WIDTH),
                     mix_c.reshape(n, RWKV_WIDTH), lp["w_out"]).reshape(b, l, D_MODEL)
    heads = lambda t: t.reshape(b, l, SB_WIDTH // HEAD_DIM, HEAD_DIM)
    state = (heads(k), heads(v), hr.reshape(b, SSM_GROUPS, SSM_STATE), hi.reshape(b, SSM_GROUPS, SSM_STATE),
             _unpair_state(s_pair), cp[:, -1, :RWKV_SHIFT])
    return x_new, state


def kernel(x_prompt, x_sample, cache_k, cache_v, state_ssm_re, state_ssm_im, state_wkv, state_shift, norm_w, w_in, q_norm_w, k_norm_w, ssm_lambda_re, ssm_lambda_im, ssm_log_dt, ssm_b_re, ssm_b_im, ssm_c_re, ssm_c_im, ssm_d, ssm_w_glu, ssm_b_glu, rwkv_mu, rwkv_w0, rwkv_w2, rwkv_a0, rwkv_a2, rwkv_k_k, rwkv_k_a, rwkv_u, rwkv_ln_w, rwkv_ln_b, w_out):
    depth = w_in.shape[0]
    xp, xs = x_prompt, x_sample
    ck_all = cache_k.reshape(cache_k.shape[:3] + (SB_WIDTH,))
    cv_all = cache_v.reshape(cache_v.shape[:3] + (SB_WIDTH,))
    p_states, s_states = [], []
    for i in range(depth):
        tile8 = lambda t: jnp.tile(t.astype(F32), SB_WIDTH // HEAD_DIM).reshape(1, SB_WIDTH)
        lp = dict(
            norm_w=norm_w[i].astype(F32).reshape(1, D_MODEL), w_in=_pad_w_in(w_in[i]),
            q_w=tile8(q_norm_w[i]), k_w=tile8(k_norm_w[i]),
            s5=_s5_params(ssm_lambda_re[i], ssm_lambda_im[i], ssm_log_dt[i], ssm_b_re[i], ssm_b_im[i],
                          ssm_c_re[i], ssm_c_im[i], ssm_d[i], ssm_w_glu[i], ssm_b_glu[i]),
            rwkv=_rwkv_params(rwkv_mu[i], rwkv_w0[i], rwkv_w2[i], rwkv_a0[i], rwkv_a2[i], rwkv_k_k[i],
                              rwkv_k_a[i], rwkv_u[i].reshape(-1), rwkv_ln_w[i], rwkv_ln_b[i]),
            w_out=w_out[i].astype(BF16))
        xp, st_p = _layer(xp, lp, None)
        past = dict(k=ck_all, v=cv_all, layer=i,
                    ssm_re=state_ssm_re[i], ssm_im=state_ssm_im[i], wkv=state_wkv[i], shift=state_shift[i])
        xs, st_s = _layer(xs, lp, past)
        p_states.append(st_p)
        s_states.append(st_s)
    stack = lambda states, j: jnp.stack([s[j] for s in states], axis=0)
    return ((xp, xs) + tuple(stack(p_states, j) for j in range(6))
            + tuple(stack(s_states, j) for j in range(6)))
```

```python
import functools
import math

import jax
import jax.numpy as jnp
from jax import lax
from jax.experimental import pallas as pl
from jax.experimental.pallas import tpu as pltpu

F32 = jnp.float32
BF16 = jnp.bfloat16

D_MODEL = 1024
HEAD_DIM = 64
SB_WIDTH = D_MODEL // 2
SSM_WIDTH = D_MODEL // 4
SSM_GROUP = 16
SSM_GROUPS = SSM_WIDTH // SSM_GROUP
SSM_STATE = 64
SSM_FLAT = SSM_GROUPS * SSM_STATE
RWKV_WIDTH = D_MODEL // 4
RWKV_HEADS = RWKV_WIDTH // HEAD_DIM
RWKV_LORA = 32
RWKV_SHIFT = 3 * RWKV_WIDTH + 2 * RWKV_LORA
RMS_EPS = 1e-6
GN_EPS = 64e-5
DECAY_SCALE = math.exp(-0.5)

LANES = 128
SUBLANES = 8
CP_PAD = 7 * LANES
COL_Q, COL_K, COL_V, COL_GA = 0, 512, 1024, 1536
COL_U, COL_GB, COL_CP, COL_GC = 2048, 2304, 2560, 2560 + CP_PAD
D_IN_PAD = COL_GC + RWKV_WIDTH
VMEM_LIMIT = 56 * 1024 * 1024

NN = (((1,), (0,)), ((), ()))
NT = (((1,), (1,)), ((), ()))
TN = (((0,), (0,)), ((), ()))


def _dot(a, b, dims=NN):
    return lax.dot_general(a, b, dims, preferred_element_type=F32)


def _split(a):
    hi = a.astype(BF16)
    lo = (a - hi.astype(F32)).astype(BF16)
    return hi, lo


def _dot2l(a, b_exact, dims=NN):
    ah, al = _split(a)
    return _dot(ah, b_exact, dims) + _dot(al, b_exact, dims)


def _dot2r(a_exact, b, dims=NN):
    bh, bl = _split(b)
    return _dot(a_exact, bh, dims) + _dot(a_exact, bl, dims)


def _params(sem):
    return pltpu.CompilerParams(dimension_semantics=sem, vmem_limit_bytes=VMEM_LIMIT)


def _full(shape):
    nd = len(shape)
    return pl.BlockSpec(shape, lambda *_: (0,) * nd)


def _head_rms(t, w):
    lane = lax.broadcasted_iota(jnp.int32, (1, LANES), 1)
    lo = lane < HEAD_DIM
    outs = []
    for j in range(t.shape[1] // LANES):
        blk = t[:, j * LANES:(j + 1) * LANES]
        sq = blk * blk
        s_lo = jnp.sum(jnp.where(lo, sq, 0.0), axis=-1, keepdims=True)
        s_hi = jnp.sum(jnp.where(lo, 0.0, sq), axis=-1, keepdims=True)
        ms = jnp.where(lo, s_lo, s_hi) * (1.0 / HEAD_DIM)
        outs.append(blk * lax.rsqrt(ms + RMS_EPS) * w[:, j * LANES:(j + 1) * LANES])
    return jnp.concatenate(outs, axis=-1)


def _mix_out(a_ref, b_ref, c_ref, w_ref):
    acc = _dot(a_ref[...], w_ref[0:SB_WIDTH, :])
    acc = acc + _dot(b_ref[...], w_ref[SB_WIDTH:SB_WIDTH + SSM_WIDTH, :])
    return acc + _dot(c_ref[...], w_ref[SB_WIDTH + SSM_WIDTH:, :])


def _inproj_kernel(*refs, prev_out):
    if prev_out:
        x_ref, a_ref, b_ref, c_ref, wo_ref = refs[:5]
        nw_ref, w_ref, qw_ref, kw_ref, xo_ref = refs[5:10]
        q_o, k_o, v_o, ga_o, u_o, gb_o, cp_o, gc_o = refs[10:]
        x = x_ref[...] + _mix_out(a_ref, b_ref, c_ref, wo_ref)
        xo_ref[...] = x
    else:
        x_ref, nw_ref, w_ref, qw_ref, kw_ref, q_o, k_o, v_o, ga_o, u_o, gb_o, cp_o, gc_o = refs
        x = x_ref[...]
    ms = jnp.mean(x * x, axis=-1, keepdims=True)
    h = (x * lax.rsqrt(ms + RMS_EPS) * nw_ref[...]).astype(BF16)

    def proj(a, b):
        return _dot(h, w_ref[:, a:b])

    q = _head_rms(proj(COL_Q, COL_K), qw_ref[...])
    q_o[...] = (q * (HEAD_DIM ** -0.5)).astype(BF16)
    k_o[...] = _head_rms(proj(COL_K, COL_V), kw_ref[...])
    v_o[...] = proj(COL_V, COL_GA)
    ga_o[...] = proj(COL_GA, COL_U)
    u_o[...] = proj(COL_U, COL_GB)
    gb_o[...] = proj(COL_GB, COL_CP)
    cp_o[...] = proj(COL_CP, COL_GC)
    gc_o[...] = proj(COL_GC, D_IN_PAD)


def _inproj(x2d, nw, w_pad, qw, kw, prev_out=None):
    n = x2d.shape[0]
    tm = min(512, n)
    widths = (SB_WIDTH, SB_WIDTH, SB_WIDTH, SB_WIDTH, SSM_WIDTH, SSM_WIDTH, CP_PAD, RWKV_WIDTH)
    dtypes = (BF16,) + (F32,) * 7
    row = lambda w: pl.BlockSpec((tm, w), lambda i: (i, 0))
    ins, in_specs = [x2d], [row(D_MODEL)]
    if prev_out is not None:
        ins += list(prev_out)
        in_specs += [row(SB_WIDTH), row(SSM_WIDTH), row(RWKV_WIDTH), _full((D_MODEL, D_MODEL))]
        widths, dtypes = (D_MODEL,) + widths, (F32,) + dtypes
    return pl.pallas_call(
        functools.partial(_inproj_kernel, prev_out=prev_out is not None),
        grid=(n // tm,),
        in_specs=in_specs + [_full((1, D_MODEL)), _full((D_MODEL, D_IN_PAD)),
                             _full((1, SB_WIDTH)), _full((1, SB_WIDTH))],
        out_specs=[row(w) for w in widths],
        out_shape=[jax.ShapeDtypeStruct((n, w), dt) for w, dt in zip(widths, dtypes)],
        compiler_params=_params(("parallel",)),
        name="inproj",
    )(*ins, nw, w_pad, qw, kw)


SB_TK = 128
SB_PAIRS = SB_WIDTH // LANES


SB_U = 2
LOG2E = math.log2(math.e)


def _sb_weights(qs, kbs, carries, tri, masks):
    return _sb_weights_of(_sb_scores(qs, kbs), carries, tri, masks)


def _sb_scores(qs, kbs):
    z = []
    for q, kb in zip(qs, kbs):
        tk = kb[0].shape[0]
        zp = _dot(q, jnp.concatenate(kb, axis=0), NT)
        z.append([zp[:, u * tk:(u + 1) * tk] for u in range(len(kb))])
    return z


def _sb_weights_of(scores, carries, tri, masks):
    pu = [(p, u) for p in range(len(scores)) for u in range(len(scores[p]))]
    z = {k: scores[k[0]][k[1]] for k in pu}
    hl = {}
    for k in pu:
        sp = jnp.maximum(z[k], 0.0) + jnp.log(1.0 + jnp.exp2(jnp.abs(z[k]) * (-LOG2E)))
        if masks[k[1]] is not None:
            sp = jnp.where(masks[k[1]], sp, 0.0)
        hl[k] = sp.astype(BF16)
    cs = {k: _dot(hl[k], tri) for k in pu}
    weights, new_carries = [], []
    for p in range(len(scores)):
        carry, row = carries[p], []
        for u in range(len(scores[p])):
            incl = cs[(p, u)] + carry
            a = jnp.exp2((z[(p, u)] - incl) * LOG2E)
            if masks[u] is not None:
                a = jnp.where(masks[u], a, 0.0)
            row.append(a.astype(BF16))
            carry = jnp.broadcast_to(incl[:, 0:1], incl.shape)
        weights.append(row)
        new_carries.append(carry)
    return weights, new_carries


def _sb_values(a2, vbs, head_lo):
    parts = []
    for vb in vbs:
        zero = jnp.zeros_like(vb)
        parts += [jnp.where(head_lo, vb, zero), jnp.where(head_lo, zero, vb)]
    return _dot(a2, jnp.concatenate(parts, axis=0))


def _sb_all_pairs(q, diag_kv, n_diag, past_k, past_v, n_iter, tri, qs_s, a_s, acc_s, carry_s):
    tq, tk = q.shape[0], SB_TK
    head_lo = lax.broadcasted_iota(jnp.int32, (1, LANES), 1) < HEAD_DIM
    pairs = range(SB_PAIRS)
    qs = []
    for p in pairs:
        qp = q[:, p * LANES:(p + 1) * LANES]
        zero = jnp.zeros_like(qp)
        qs.append(jnp.concatenate([jnp.where(head_lo, qp, zero), jnp.where(head_lo, zero, qp)], axis=0))
        qs_s[p] = qs[p]
    rg = tq // n_diag
    row = lax.broadcasted_iota(jnp.int32, (2 * rg, tk), 0)
    col = lax.broadcasted_iota(jnp.int32, (2 * rg, tk), 1)
    causal = col < jnp.where(row >= rg, row - rg, row)
    diag = [[diag_kv(d, p) for d in range(n_diag)] for p in pairs]
    groups = [(p, r) for p in pairs for r in range(n_diag)]
    rows_of = lambda x, r: jnp.concatenate([x[r * rg:(r + 1) * rg], x[tq + r * rg:tq + (r + 1) * rg]], axis=0)
    ws, carries = _sb_weights([rows_of(qs[p], r) for p, r in groups],
                              [[diag[p][d][0] for d in range(r, -1, -1)] for p, r in groups],
                              [jnp.zeros((2 * rg, tk), F32)] * len(groups), tri, [causal] + [None] * (n_diag - 1))
    for p in pairs:
        accs, c_lo, c_hi = [], [], []
        for r in range(n_diag):
            g = groups.index((p, r))
            a2 = jnp.concatenate([half for a in ws[g] for half in (a[:rg], a[rg:])], axis=1)
            accs.append(_sb_values(a2, [diag[p][d][1] for d in range(r, -1, -1)], head_lo))
            c_lo.append(carries[g][:rg])
            c_hi.append(carries[g][rg:])
        acc_s[p] = jnp.concatenate(accs, axis=0)
        carry_s[p] = jnp.concatenate(c_lo + c_hi, axis=0)
    a_s[...] = jnp.zeros_like(a_s)

    def values(prev_start):
        for p in pairs:
            vbs = [past_v(prev_start + (SB_U - 1 - u) * tk, p) for u in range(SB_U)]
            acc_s[p] += _sb_values(a_s[p], vbs, head_lo)

    span = SB_U * tk

    def step(it):
        start = (n_iter - 1 - it) * span
        prev_start = jnp.where(it == 0, 0, (n_iter - it) * span)
        kbs = [[past_k(start + (SB_U - 1 - u) * tk, p) for u in range(SB_U)] for p in pairs]
        scores = _sb_scores([qs_s[p] for p in pairs], kbs)
        values(prev_start)
        ws, carries = _sb_weights_of(scores, [carry_s[p] for p in pairs], tri, [None] * SB_U)
        for p in pairs:
            for u in range(SB_U):
                a_s[p, :, (2 * u) * tk:(2 * u + 1) * tk] = ws[p][u][:tq]
                a_s[p, :, (2 * u + 1) * tk:(2 * u + 2) * tk] = ws[p][u][tq:]
            carry_s[p] = carries[p]

    odd = n_iter % 2

    @pl.when(odd == 1)
    def _():
        step(jnp.int32(0))

    def two_steps(i, c):
        step(odd + 2 * i)
        step(odd + 2 * i + 1)
        return c

    lax.fori_loop(0, n_iter // 2, two_steps, 0)
    values(0)
    return jnp.concatenate([acc_s[p] for p in range(SB_PAIRS)], axis=1)


def _sb_tri():
    j = lax.broadcasted_iota(jnp.int32, (SB_TK, SB_TK), 0)
    s = lax.broadcasted_iota(jnp.int32, (SB_TK, SB_TK), 1)
    return (j >= s).astype(BF16)


def _sb_scratch(tq):
    return [pltpu.VMEM((SB_PAIRS, 2 * tq, LANES), BF16), pltpu.VMEM((SB_PAIRS, tq, SB_U * 2 * SB_TK), BF16),
            pltpu.VMEM((SB_PAIRS, tq, LANES), F32), pltpu.VMEM((SB_PAIRS, 2 * tq, SB_TK), F32)]


def _sb_block_reader(ref2d):
    def read(start, p):
        if not isinstance(start, int):
            start = pl.multiple_of(start, SB_TK)
        return ref2d[pl.ds(start, SB_TK), p * LANES:(p + 1) * LANES].astype(BF16)
    return read


def _sb_prompt_kernel(q_ref, k_ref, v_ref, g_ref, tri_ref, o_ref, qs_s, a_s, acc_s, carry_s, *, tq):
    qi = pl.program_id(1)
    base = pl.multiple_of(qi * tq, tq)
    read_k, read_v = _sb_block_reader(k_ref.at[0]), _sb_block_reader(v_ref.at[0])

    def diag_kv(d, p):
        return read_k(base + d * SB_TK, p), read_v(base + d * SB_TK, p)

    n_iter = qi * (tq // (SB_U * SB_TK))
    y = _sb_all_pairs(q_ref[0], diag_kv, tq // SB_TK, read_k, read_v, n_iter, tri_ref[...],
                      qs_s, a_s, acc_s, carry_s)
    o_ref[0] = (y * jax.nn.silu(g_ref[0])).astype(BF16)


def _sb_prompt(q, k, v, g):
    b, l, _ = q.shape
    tq = SB_U * SB_TK
    qblk = pl.BlockSpec((1, tq, SB_WIDTH), lambda bi, qi: (bi, qi, 0))
    kvblk = pl.BlockSpec((1, l, SB_WIDTH), lambda bi, qi: (bi, 0, 0))
    return pl.pallas_call(
        functools.partial(_sb_prompt_kernel, tq=tq),
        grid=(b, l // tq),
        in_specs=[qblk, kvblk, kvblk, qblk, _full((SB_TK, SB_TK))],
        out_specs=qblk,
        out_shape=jax.ShapeDtypeStruct((b, l, SB_WIDTH), BF16),
        scratch_shapes=_sb_scratch(tq),
        compiler_params=_params(("parallel", "arbitrary")),
        name="sb_prompt",
    )(q, k, v, g, _sb_tri())


def _sb_sample_kernel(q_ref, k_ref, v_ref, ck_ref, cv_ref, g_ref, tri_ref, o_ref, qs_s, a_s, acc_s, carry_s,
                      *, n_iter):
    tq = q_ref.shape[1]

    def diag_kv(d, p):
        ls = slice(p * LANES, (p + 1) * LANES)
        pad = jnp.zeros((SB_TK - tq, LANES), BF16)
        return (jnp.concatenate([k_ref[0, :, ls].astype(BF16), pad], axis=0),
                jnp.concatenate([v_ref[0, :, ls].astype(BF16), pad], axis=0))

    y = _sb_all_pairs(q_ref[0], diag_kv, 1, _sb_block_reader(ck_ref.at[0, 0]), _sb_block_reader(cv_ref.at[0, 0]),
                      n_iter, tri_ref[...], qs_s, a_s, acc_s, carry_s)
    o_ref[0] = (y * jax.nn.silu(g_ref[0])).astype(BF16)


def _sb_sample(q, k, v, ck, cv, layer, g):
    b, l, _ = q.shape
    past = ck.shape[2]
    qblk = pl.BlockSpec((1, l, SB_WIDTH), lambda bi: (bi, 0, 0))
    cblk = pl.BlockSpec((1, 1, past, SB_WIDTH), lambda bi: (layer, bi, 0, 0))
    return pl.pallas_call(
        functools.partial(_sb_sample_kernel, n_iter=past // (SB_U * SB_TK)),
        grid=(b,),
        in_specs=[qblk, qblk, qblk, cblk, cblk, qblk, _full((SB_TK, SB_TK))],
        out_specs=qblk,
        out_shape=jax.ShapeDtypeStruct((b, l, SB_WIDTH), BF16),
        scratch_shapes=_sb_scratch(l),
        compiler_params=_params(("parallel",)),
        name="sb_sample",
    )(q, k, v, ck, cv, g, _sb_tri())


def _s5_kernel(u_ref, g_ref, h0r_ref, h0i_ref, ar_ref, ai_ref, b_ref, c_ref, d_ref,
               wg_ref, bg_ref, o_ref, hr_o, hi_o, bu_s, h_s, ut_s, gt_s, ot_s, *, tt):
    ti = pl.program_id(1)
    rows = tt * SUBLANES
    slabs = SSM_WIDTH // LANES

    @pl.when(ti == 0)
    def _():
        hr_o[...] = h0r_ref[...]
        hi_o[...] = h0i_ref[...]

    def time_major(ref, slab_s):
        for s in range(SUBLANES):
            for j in range(slabs):
                slab_s[j, pl.ds(s, tt, stride=SUBLANES), :] = ref[s, :, j * LANES:(j + 1) * LANES]
        return jnp.concatenate([slab_s[j] for j in range(slabs)], axis=1)

    u = time_major(u_ref, ut_s)
    bu_s[...] = _dot(u.astype(BF16), b_ref[...])

    ar = jnp.broadcast_to(ar_ref[...], (SUBLANES, SSM_FLAT))
    ai = jnp.broadcast_to(ai_ref[...], (SUBLANES, SSM_FLAT))

    def step(t, carry):
        hr, hi = carry
        r0 = pl.multiple_of(t * SUBLANES, SUBLANES)
        br = bu_s[pl.ds(r0, SUBLANES), 0:SSM_FLAT]
        bi = bu_s[pl.ds(r0, SUBLANES), SSM_FLAT:2 * SSM_FLAT]
        nr = ar * hr - ai * hi + br
        ni = ar * hi + ai * hr + bi
        h_s[pl.ds(r0, SUBLANES), 0:SSM_FLAT] = nr
        h_s[pl.ds(r0, SUBLANES), SSM_FLAT:2 * SSM_FLAT] = ni
        return nr, ni

    hr, hi = lax.fori_loop(0, tt, step, (hr_o[...], hi_o[...]))
    hr_o[...] = hr
    hi_o[...] = hi

    y = _dot(h_s[...].astype(BF16), c_ref[...]) + d_ref[...] * u
    y = jax.nn.gelu(y)
    z = _dot(y.astype(BF16), wg_ref[...]) + bg_ref[...]
    out = z[:, :SSM_WIDTH] * jax.nn.sigmoid(z[:, SSM_WIDTH:])
    out = out * jax.nn.silu(time_major(g_ref, gt_s))
    for j in range(slabs):
        ot_s[j] = out[:, j * LANES:(j + 1) * LANES]
    for s in range(SUBLANES):
        for j in range(slabs):
            o_ref[s, :, j * LANES:(j + 1) * LANES] = ot_s[j, pl.ds(s, tt, stride=SUBLANES), :].astype(BF16)


def _s5(u, g, h0r, h0i, sp):
    b, l, _ = u.shape
    tt = min(128, l)
    rows = tt * SUBLANES
    ublk = pl.BlockSpec((SUBLANES, tt, SSM_WIDTH), lambda bi, ti: (bi, ti, 0))
    sblk = pl.BlockSpec((SUBLANES, SSM_FLAT), lambda bi, ti: (bi, 0))
    slab = pltpu.VMEM((SSM_WIDTH // LANES, rows, LANES), F32)
    return pl.pallas_call(
        functools.partial(_s5_kernel, tt=tt),
        grid=(b // SUBLANES, l // tt),
        in_specs=[ublk, ublk, sblk, sblk,
                  _full((1, SSM_FLAT)), _full((1, SSM_FLAT)),
                  _full((SSM_WIDTH, 2 * SSM_FLAT)),
                  _full((2 * SSM_FLAT, SSM_WIDTH)), _full((1, SSM_WIDTH)),
                  _full((SSM_WIDTH, 2 * SSM_WIDTH)), _full((1, 2 * SSM_WIDTH))],
        out_specs=[ublk, sblk, sblk],
        out_shape=[jax.ShapeDtypeStruct((b, l, SSM_WIDTH), BF16),
                   jax.ShapeDtypeStruct((b, SSM_FLAT), F32),
                   jax.ShapeDtypeStruct((b, SSM_FLAT), F32)],
        scratch_shapes=[pltpu.VMEM((rows, 2 * SSM_FLAT), F32), pltpu.VMEM((rows, 2 * SSM_FLAT), F32),
                        slab, slab, slab],
        compiler_params=_params(("parallel", "arbitrary")),
        name="s5",
    )(u, g, h0r, h0i, sp["ar"], sp["ai"], sp["b"], sp["c"], sp["d"], sp["w_glu"], sp["b_glu"])


def _s5_params(lam_re, lam_im, log_dt, b_re, b_im, c_re, c_im, d, w_glu, b_glu):
    dt = jnp.exp(log_dt.astype(F32))[:, None]
    lr = jnp.minimum(lam_re.astype(F32), -1e-4)
    li = lam_im.astype(F32)
    er = jnp.exp(lr * dt)
    ar, ai = er * jnp.cos(li * dt), er * jnp.sin(li * dt)
    den = lr * lr + li * li
    fr = ((ar - 1.0) * lr + ai * li) / den
    fi = (ai * lr - (ar - 1.0) * li) / den
    br, bi = b_re.astype(F32), b_im.astype(F32)
    bbr = fr[..., None] * br - fi[..., None] * bi
    bbi = fr[..., None] * bi + fi[..., None] * br
    eye = jnp.eye(SSM_GROUPS, dtype=F32)
    bd_in = lambda m: jnp.einsum("gpc,gh->gchp", m, eye).reshape(SSM_WIDTH, SSM_FLAT)
    bd_out = lambda m: jnp.einsum("gcp,gh->gphc", m, eye).reshape(SSM_FLAT, SSM_WIDTH)
    b_all = jnp.concatenate([bd_in(bbr), bd_in(bbi)], axis=1)
    c_all = jnp.concatenate([bd_out(c_re.astype(F32)), -bd_out(c_im.astype(F32))], axis=0)
    return dict(ar=ar.reshape(1, SSM_FLAT), ai=ai.reshape(1, SSM_FLAT), b=b_all.astype(BF16),
                c=c_all.astype(BF16), d=d.astype(F32).reshape(1, SSM_WIDTH),
                w_glu=w_glu.astype(BF16), b_glu=b_glu.astype(F32).reshape(1, 2 * SSM_WIDTH))


RWKV_CK = 64
RWKV_NB = 8


def _rwkv_kernel(cp_ref, g_ref, prev_ref, s0_ref, mu_ref, w0_ref, w2_ref, a0_ref, a2_ref, kk_ref, ka_ref,
                 ub_ref, lnw_ref, lnb_ref, ho_ref, tri_ref, ones_ref, o_ref, s_o, prev_s,
                 khw_s, rw_s, ktl_s, btl_s, v_s, kht_s, bht_s, wch_s, bonus_s, y_s, *, tt, ck):
    ti = pl.program_id(1)

    @pl.when(ti == 0)
    def _():
        s_o[...] = s0_ref[...]
        prev_s[...] = jnp.broadcast_to(prev_ref[...], prev_s.shape)

    w = RWKV_WIDTH
    nb_all = cp_ref.shape[0]
    head_ones, chunk_tri, chunk_ones = ho_ref[...], tri_ref[...], ones_ref[...]

    def prep(nb, carry):
        cp = cp_ref[nb]
        trow = lax.broadcasted_iota(jnp.int32, (tt, 1), 0)
        shifted = jnp.where(trow == 0, prev_s[nb, 0:1, :], pltpu.roll(cp, 1, 0))
        prev_s[nb] = jnp.broadcast_to(cp[tt - 1:tt, :], (SUBLANES, CP_PAD))
        xc = cp + mu_ref[...] * (shifted - cp)
        r, k, v, lora = xc[:, 0:w], xc[:, w:2 * w], xc[:, 2 * w:3 * w], xc[:, 3 * w:]
        logw = -DECAY_SCALE * jax.nn.sigmoid(w0_ref[...] + _dot(jnp.tanh(lora).astype(BF16), w2_ref[...]))
        a = jax.nn.sigmoid(a0_ref[...] + _dot(lora.astype(BF16), a2_ref[...]))
        cum = _dot2r(chunk_tri, logw)
        tot = _dot2r(chunk_ones, logw)
        kkv = k * kk_ref[...]
        kh = kkv * lax.rsqrt(_dot2l(kkv * kkv, head_ones) + 1e-12)
        kt = k * (1.0 + (a - 1.0) * ka_ref[...])
        bvec = a * kh
        w_inv = jnp.exp(-cum)
        w_end = jnp.exp(tot - cum)
        khw_s[nb] = (kh * jnp.exp(cum - logw)).astype(BF16)
        rw_s[nb] = (r * jnp.exp(cum)).astype(BF16)
        ktl_s[nb] = (kt * w_inv).astype(BF16)
        btl_s[nb] = (bvec * w_inv).astype(BF16)
        v_s[nb] = v.astype(BF16)
        kht_s[nb] = (kt * w_end).astype(BF16)
        bht_s[nb] = (bvec * w_end).astype(BF16)
        wch_s[nb] = jnp.exp(tot)
        bonus_s[nb] = _dot2l(r * kt * ub_ref[...], head_ones) * v
        return carry

    lax.fori_loop(0, nb_all, prep, 0)

    lane = lax.broadcasted_iota(jnp.int32, (1, LANES), 1)
    m_lo = lane < HEAD_DIM
    ri = lax.broadcasted_iota(jnp.int32, (2 * ck, 2 * ck), 0)
    cj = lax.broadcasted_iota(jnp.int32, (2 * ck, 2 * ck), 1)
    same_head = (ri >= ck) == (cj >= ck)
    strict = same_head & (cj < ri)
    lower = same_head & (cj <= ri)
    eye = (ri == cj).astype(F32)
    rr = lax.broadcasted_iota(jnp.int32, (2 * ck, LANES), 0)
    rl = lax.broadcasted_iota(jnp.int32, (2 * ck, LANES), 1)
    own_lanes = (rr >= ck) == (rl >= HEAD_DIM)
    sr = lax.broadcasted_iota(jnp.int32, (LANES, LANES), 0) // HEAD_DIM
    sc = lax.broadcasted_iota(jnp.int32, (LANES, LANES), 1) // HEAD_DIM
    blockdiag = sr == sc
    dot1 = lambda x, y: _dot(x.astype(BF16), y.astype(BF16))

    cat = lambda xs: jnp.concatenate(xs, axis=0)

    def chunk_step(c, carry):
        r0 = pl.multiple_of(c * ck, ck)
        rows = pl.ds(r0, ck)
        chains = [(nb, hp) for nb in range(nb_all) for hp in range(w // LANES)]
        lanes = lambda hp: slice(hp * LANES, (hp + 1) * LANES)
        rd = lambda ref: [ref[nb, rows, lanes(hp)] for nb, hp in chains]
        khw, rw, ktl, btl, v = rd(khw_s), rd(rw_s), rd(ktl_s), rd(btl_s), rd(v_s)
        z = jnp.zeros_like(khw[0])
        l4 = [cat([jnp.where(m_lo, a, z), jnp.where(m_lo, z, a), jnp.where(m_lo, b, z), jnp.where(m_lo, z, b)])
              for a, b in zip(khw, rw)]
        scb = [_dot(a, cat([b, b]), NT) for a, b in zip(l4, btl)]
        sck = [_dot(a, cat([b, b]), NT) for a, b in zip(l4, ktl)]
        nmat = [jnp.where(strict, a[:2 * ck], 0.0) for a in scb]
        arb = [jnp.where(lower, a[2 * ck:], 0.0) for a in scb]
        akk = [jnp.where(strict, a[:2 * ck], 0.0) for a in sck]
        ark = [jnp.where(lower, a[2 * ck:], 0.0) for a in sck]
        inv = [eye - a for a in nmat]
        pw = nmat
        steps = 1
        while 2 * steps < ck:
            pw = [dot1(a, a) for a in pw]
            inv = [dot1(a, eye + b) for a, b in zip(inv, pw)]
            steps *= 2
        s_pair = [s_o[nb, hp] for nb, hp in chains]
        ks = [_dot(cat([a, b]), s.astype(BF16), NT) for a, b, s in zip(khw, rw, s_pair)]
        av = [dot1(cat([a, b]), cat([x, x])) for a, b, x in zip(akk, ark, v)]
        sa2 = [jnp.where(own_lanes, dot1(t, cat([k[:ck], k[:ck]]) + a[:2 * ck]), 0.0)
               for t, k, a in zip(inv, ks, av)]
        y2 = [jnp.where(own_lanes, a[2 * ck:] - dot1(b, s), 0.0) for a, b, s in zip(av, arb, sa2)]
        for (nb, hp), k, y in zip(chains, ks, y2):
            y_s[nb, rows, lanes(hp)] = k[ck:] + y[:ck] + y[ck:]
        kht, bht = rd(kht_s), rd(bht_s)
        upd = [_dot(cat([x, (-(s[:ck] + s[ck:])).astype(BF16)]), cat([a, b]), TN)
               for x, s, a, b in zip(v, sa2, kht, bht)]
        for (nb, hp), s, u in zip(chains, s_pair, upd):
            s_o[nb, hp] = s * wch_s[nb, pl.ds(r0, 1), lanes(hp)] + jnp.where(blockdiag, u, 0.0)
        return carry

    lax.fori_loop(0, tt // ck, chunk_step, 0)

    fin = 2 if nb_all % 2 == 0 else 1

    def finish(i, carry):
        nbs = [i * fin + j for j in range(fin)]
        ys = [y_s[nb] for nb in nbs]
        mean = [_dot2l(y, head_ones) * (1.0 / HEAD_DIM) for y in ys]
        cen = [y - m for y, m in zip(ys, mean)]
        var = [_dot2l(c * c, head_ones) * (1.0 / HEAD_DIM) for c in cen]
        for nb, c, v in zip(nbs, cen, var):
            y = c * lax.rsqrt(v + GN_EPS) * lnw_ref[...] + lnb_ref[...] + bonus_s[nb]
            o_ref[nb] = (y * jax.nn.silu(g_ref[nb])).astype(BF16)
        return carry

    lax.fori_loop(0, nb_all // fin, finish, 0)


def _rwkv(cp, g, prev, s0p, rp):
    b, l, _ = cp.shape
    tt = min(256, l)
    ck = min(RWKV_CK, tt)
    nb = min(RWKV_NB, b)
    tblk = lambda w: pl.BlockSpec((nb, tt, w), lambda bi, ti: (bi, ti, 0))
    sblk = pl.BlockSpec((nb, 2, LANES, LANES), lambda bi, ti: (bi, 0, 0, 0))
    vec = _full((1, RWKV_WIDTH))
    hl = lax.broadcasted_iota(jnp.int32, (RWKV_WIDTH, RWKV_WIDTH), 0) // HEAD_DIM
    hc = lax.broadcasted_iota(jnp.int32, (RWKV_WIDTH, RWKV_WIDTH), 1) // HEAD_DIM
    tr = lax.broadcasted_iota(jnp.int32, (tt, tt), 0)
    tc = lax.broadcasted_iota(jnp.int32, (tt, tt), 1)
    same_chunk = (tr // ck) == (tc // ck)
    slab = lambda dt: pltpu.VMEM((nb, tt, RWKV_WIDTH), dt)
    return pl.pallas_call(
        functools.partial(_rwkv_kernel, tt=tt, ck=ck),
        grid=(b // nb, l // tt),
        in_specs=[tblk(CP_PAD), tblk(RWKV_WIDTH),
                  pl.BlockSpec((nb, 1, CP_PAD), lambda bi, ti: (bi, 0, 0)), sblk,
                  _full((1, CP_PAD)), vec, _full((LANES, RWKV_WIDTH)), vec, _full((LANES, RWKV_WIDTH)),
                  vec, vec, vec, vec, vec,
                  _full((RWKV_WIDTH, RWKV_WIDTH)), _full((tt, tt)), _full((tt, tt))],
        out_specs=[tblk(RWKV_WIDTH), sblk],
        out_shape=[jax.ShapeDtypeStruct((b, l, RWKV_WIDTH), BF16),
                   jax.ShapeDtypeStruct((b, 2, LANES, LANES), F32)],
        scratch_shapes=[pltpu.VMEM((nb, SUBLANES, CP_PAD), F32)] + [slab(BF16)] * 7 + [slab(F32)] * 3,
        compiler_params=_params(("parallel", "arbitrary")),
        name="rwkv",
    )(cp, g, prev, s0p, rp["mu"], rp["w0"], rp["w2"], rp["a0"], rp["a2"], rp["k_k"], rp["k_a"],
      rp["u"], rp["ln_w"], rp["ln_b"],
      (hl == hc).astype(BF16), (same_chunk & (tc <= tr)).astype(BF16), same_chunk.astype(BF16))


def _rwkv_params(mu, w0, w2, a0, a2, k_k, k_a, u, ln_w, ln_b):
    vec = lambda t: t.astype(F32).reshape(1, RWKV_WIDTH)
    mu_p = jnp.zeros((1, CP_PAD), F32).at[0, :RWKV_SHIFT].set(mu.astype(F32))
    w2_p = jnp.zeros((LANES, RWKV_WIDTH), F32).at[:RWKV_LORA].set(w2.astype(F32)).astype(BF16)
    a2_p = jnp.zeros((LANES, RWKV_WIDTH), F32).at[RWKV_LORA:2 * RWKV_LORA].set(a2.astype(F32)).astype(BF16)
    return dict(mu=mu_p, w0=vec(w0), w2=w2_p, a0=vec(a0), a2=a2_p, k_k=vec(k_k), k_a=vec(k_a),
                u=vec(u), ln_w=vec(ln_w), ln_b=vec(ln_b))


def _pair_state(s):
    b = s.shape[0]
    s = s.astype(F32).reshape(b, 2, 2, HEAD_DIM, HEAD_DIM)
    eye = jnp.eye(2, dtype=F32)
    return jnp.einsum("bphvk,hg->bphvgk", s, eye).reshape(b, 2, LANES, LANES)


def _unpair_state(sp):
    b = sp.shape[0]
    s = sp.reshape(b, 2, 2, HEAD_DIM, 2, HEAD_DIM)
    return jnp.stack([s[:, :, 0, :, 0, :], s[:, :, 1, :, 1, :]], axis=2).reshape(b, RWKV_HEADS, HEAD_DIM, HEAD_DIM)


def _outproj_kernel(x_ref, a_ref, b_ref, c_ref, w_ref, o_ref):
    o_ref[...] = x_ref[...] + _mix_out(a_ref, b_ref, c_ref, w_ref)


def _outproj(x2d, ma, mb, mc, w_out):
    n = x2d.shape[0]
    tm = min(512, n)
    row = lambda w: pl.BlockSpec((tm, w), lambda i: (i, 0))
    return pl.pallas_call(
        _outproj_kernel,
        grid=(n // tm,),
        in_specs=[row(D_MODEL), row(SB_WIDTH), row(SSM_WIDTH), row(RWKV_WIDTH), _full((D_MODEL, D_MODEL))],
        out_specs=row(D_MODEL),
        out_shape=jax.ShapeDtypeStruct((n, D_MODEL), F32),
        compiler_params=_params(("parallel",)),
        name="outproj",
    )(x2d, ma, mb, mc, w_out)


def _pad_w_in(w_in):
    w = w_in.astype(F32)
    pad = jnp.zeros((D_MODEL, CP_PAD - RWKV_SHIFT), F32)
    return jnp.concatenate([w[:, :COL_CP + RWKV_SHIFT], pad, w[:, COL_CP + RWKV_SHIFT:]], axis=1).astype(BF16)


def _layer(x2d, b, l, lp, past, prev_out):
    n = b * l
    outs = _inproj(x2d, lp["norm_w"], lp["w_in"], lp["q_w"], lp["k_w"], prev_out)
    if prev_out is not None:
        x2d, outs = outs[0], outs[1:]
    q, k, v, ga, u, gb, cp, gc = outs
    r3 = lambda t: t.reshape(b, l, t.shape[-1])
    q, k, v, ga, u, gb, cp, gc = map(r3, (q, k, v, ga, u, gb, cp, gc))

    if past is None:
        mix_a = _sb_prompt(q, k, v, ga)
        h0r = h0i = jnp.zeros((b, SSM_FLAT), F32)
        prev = jnp.zeros((b, 1, CP_PAD), F32)
        s0p = jnp.zeros((b, 2, LANES, LANES), F32)
    else:
        mix_a = _sb_sample(q, k, v, past["k"], past["v"], past["layer"], ga)
        h0r = past["ssm_re"].astype(F32).reshape(b, SSM_FLAT)
        h0i = past["ssm_im"].astype(F32).reshape(b, SSM_FLAT)
        prev = jnp.pad(past["shift"].astype(F32), ((0, 0), (0, CP_PAD - RWKV_SHIFT)))[:, None, :]
        s0p = _pair_state(past["wkv"])

    mix_b, hr, hi = _s5(u, gb, h0r, h0i, lp["s5"])
    mix_c, s_pair = _rwkv(cp, gc, prev, s0p, lp["rwkv"])

    pending = (mix_a.reshape(n, SB_WIDTH), mix_b.reshape(n, SSM_WIDTH), mix_c.reshape(n, RWKV_WIDTH), lp["w_out"])
    heads = lambda t: t.reshape(b, l, SB_WIDTH // HEAD_DIM, HEAD_DIM)
    state = (heads(k), heads(v), hr.reshape(b, SSM_GROUPS, SSM_STATE), hi.reshape(b, SSM_GROUPS, SSM_STATE),
             _unpair_state(s_pair), cp[:, -1, :RWKV_SHIFT])
    return x2d, pending, state


def kernel(x_prompt, x_sample, cache_k, cache_v, state_ssm_re, state_ssm_im, state_wkv, state_shift, norm_w, w_in, q_norm_w, k_norm_w, ssm_lambda_re, ssm_lambda_im, ssm_log_dt, ssm_b_re, ssm_b_im, ssm_c_re, ssm_c_im, ssm_d, ssm_w_glu, ssm_b_glu, rwkv_mu, rwkv_w0, rwkv_w2, rwkv_a0, rwkv_a2, rwkv_k_k, rwkv_k_a, rwkv_u, rwkv_ln_w, rwkv_ln_b, w_out):
    depth = w_in.shape[0]
    (bp, lp_len, _), (bs, ls_len, _) = x_prompt.shape, x_sample.shape
    xp, xs = x_prompt.reshape(bp * lp_len, D_MODEL), x_sample.reshape(bs * ls_len, D_MODEL)
    out_p = out_s = None
    ck_all = cache_k.reshape(cache_k.shape[:3] + (SB_WIDTH,))
    cv_all = cache_v.reshape(cache_v.shape[:3] + (SB_WIDTH,))
    p_states, s_states = [], []
    for i in range(depth):
        tile8 = lambda t: jnp.tile(t.astype(F32), SB_WIDTH // HEAD_DIM).reshape(1, SB_WIDTH)
        lp = dict(
            norm_w=norm_w[i].astype(F32).reshape(1, D_MODEL), w_in=_pad_w_in(w_in[i]),
            q_w=tile8(q_norm_w[i]), k_w=tile8(k_norm_w[i]),
            s5=_s5_params(ssm_lambda_re[i], ssm_lambda_im[i], ssm_log_dt[i], ssm_b_re[i], ssm_b_im[i],
                          ssm_c_re[i], ssm_c_im[i], ssm_d[i], ssm_w_glu[i], ssm_b_glu[i]),
            rwkv=_rwkv_params(rwkv_mu[i], rwkv_w0[i], rwkv_w2[i], rwkv_a0[i], rwkv_a2[i], rwkv_k_k[i],
                              rwkv_k_a[i], rwkv_u[i].reshape(-1), rwkv_ln_w[i], rwkv_ln_b[i]),
            w_out=w_out[i].astype(BF16))
        xp, out_p, st_p = _layer(xp, bp, lp_len, lp, None, out_p)
        past = dict(k=ck_all, v=cv_all, layer=i,
                    ssm_re=state_ssm_re[i], ssm_im=state_ssm_im[i], wkv=state_wkv[i], shift=state_shift[i])
        xs, out_s, st_s = _layer(xs, bs, ls_len, lp, past, out_s)
        p_states.append(st_p)
        s_states.append(st_s)
    xp = _outproj(xp, *out_p).reshape(x_prompt.shape)
    xs = _outproj(xs, *out_s).reshape(x_sample.shape)
    stack = lambda states, j: jnp.stack([s[j] for s in states], axis=0)
    return ((xp, xs) + tuple(stack(p_states, j) for j in range(6))
            + tuple(stack(s_states, j) for j in range(6)))
```

```python
import functools
import math

import jax
import jax.numpy as jnp
from jax import lax
from jax.experimental import pallas as pl
from jax.experimental.pallas import tpu as pltpu

F32 = jnp.float32
BF16 = jnp.bfloat16

D_MODEL = 1024
HEAD_DIM = 64
SB_WIDTH = D_MODEL // 2
SSM_WIDTH = D_MODEL // 4
SSM_GROUP = 16
SSM_GROUPS = SSM_WIDTH // SSM_GROUP
SSM_STATE = 64
SSM_FLAT = SSM_GROUPS * SSM_STATE
RWKV_WIDTH = D_MODEL // 4
RWKV_HEADS = RWKV_WIDTH // HEAD_DIM
RWKV_LORA = 32
RWKV_SHIFT = 3 * RWKV_WIDTH + 2 * RWKV_LORA
RMS_EPS = 1e-6
GN_EPS = 64e-5
DECAY_SCALE = math.exp(-0.5)

LANES = 128
SUBLANES = 8
CP_PAD = 7 * LANES
COL_Q, COL_K, COL_V, COL_GA = 0, 512, 1024, 1536
COL_U, COL_GB, COL_CP, COL_GC = 2048, 2304, 2560, 2560 + CP_PAD
D_IN_PAD = COL_GC + RWKV_WIDTH
VMEM_LIMIT = 56 * 1024 * 1024

NN = (((1,), (0,)), ((), ()))
NT = (((1,), (1,)), ((), ()))
TN = (((0,), (0,)), ((), ()))


def _dot(a, b, dims=NN):
    return lax.dot_general(a, b, dims, preferred_element_type=F32)


def _split(a):
    hi = a.astype(BF16)
    lo = (a - hi.astype(F32)).astype(BF16)
    return hi, lo


def _dot2l(a, b_exact, dims=NN):
    ah, al = _split(a)
    return _dot(ah, b_exact, dims) + _dot(al, b_exact, dims)


def _dot2r(a_exact, b, dims=NN):
    bh, bl = _split(b)
    return _dot(a_exact, bh, dims) + _dot(a_exact, bl, dims)


def _params(sem):
    return pltpu.CompilerParams(dimension_semantics=sem, vmem_limit_bytes=VMEM_LIMIT)


def _full(shape):
    nd = len(shape)
    return pl.BlockSpec(shape, lambda *_: (0,) * nd)


def _head_rms(t, w):
    lane = lax.broadcasted_iota(jnp.int32, (1, LANES), 1)
    lo = lane < HEAD_DIM
    outs = []
    for j in range(t.shape[1] // LANES):
        blk = t[:, j * LANES:(j + 1) * LANES]
        sq = blk * blk
        s_lo = jnp.sum(jnp.where(lo, sq, 0.0), axis=-1, keepdims=True)
        s_hi = jnp.sum(jnp.where(lo, 0.0, sq), axis=-1, keepdims=True)
        ms = jnp.where(lo, s_lo, s_hi) * (1.0 / HEAD_DIM)
        outs.append(blk * lax.rsqrt(ms + RMS_EPS) * w[:, j * LANES:(j + 1) * LANES])
    return jnp.concatenate(outs, axis=-1)


def _mix_out(a_ref, b_ref, c_ref, w_ref):
    acc = _dot(a_ref[...], w_ref[0:SB_WIDTH, :])
    acc = acc + _dot(b_ref[...], w_ref[SB_WIDTH:SB_WIDTH + SSM_WIDTH, :])
    return acc + _dot(c_ref[...], w_ref[SB_WIDTH + SSM_WIDTH:, :])


def _inproj_kernel(*refs, prev_out):
    if prev_out:
        x_ref, a_ref, b_ref, c_ref, wo_ref = refs[:5]
        nw_ref, w_ref, qw_ref, kw_ref, xo_ref = refs[5:10]
        q_o, k_o, v_o, ga_o, u_o, gb_o, cp_o, gc_o, kt_o, vt_o = refs[10:]
        x = x_ref[...] + _mix_out(a_ref, b_ref, c_ref, wo_ref)
        xo_ref[...] = x
    else:
        x_ref, nw_ref, w_ref, qw_ref, kw_ref, q_o, k_o, v_o, ga_o, u_o, gb_o, cp_o, gc_o, kt_o, vt_o = refs
        x = x_ref[...]
    ms = jnp.mean(x * x, axis=-1, keepdims=True)
    h = (x * lax.rsqrt(ms + RMS_EPS) * nw_ref[...]).astype(BF16)

    def proj(a, b):
        return _dot(h, w_ref[:, a:b])

    q = _head_rms(proj(COL_Q, COL_K), qw_ref[...])
    q_o[...] = (q * (HEAD_DIM ** -0.5)).astype(BF16)
    k = _head_rms(proj(COL_K, COL_V), kw_ref[...])
    v = proj(COL_V, COL_GA)
    k_o[...] = k
    v_o[...] = v
    kt_o[0] = k.T
    vt_o[0] = v.T
    ga_o[...] = proj(COL_GA, COL_U)
    u_o[...] = proj(COL_U, COL_GB)
    gb_o[...] = proj(COL_GB, COL_CP)
    cp_o[...] = proj(COL_CP, COL_GC)
    gc_o[...] = proj(COL_GC, D_IN_PAD)


def _inproj(x2d, seq, nw, w_pad, qw, kw, prev_out=None):
    n = x2d.shape[0]
    tm = min(512, seq)
    per = seq // tm
    widths = (SB_WIDTH, SB_WIDTH, SB_WIDTH, SB_WIDTH, SSM_WIDTH, SSM_WIDTH, CP_PAD, RWKV_WIDTH)
    dtypes = (BF16,) + (F32,) * 7
    row = lambda w: pl.BlockSpec((tm, w), lambda i: (i, 0))
    ins, in_specs = [x2d], [row(D_MODEL)]
    if prev_out is not None:
        ins += list(prev_out)
        in_specs += [row(SB_WIDTH), row(SSM_WIDTH), row(RWKV_WIDTH), _full((D_MODEL, D_MODEL))]
        widths, dtypes = (D_MODEL,) + widths, (F32,) + dtypes
    return pl.pallas_call(
        functools.partial(_inproj_kernel, prev_out=prev_out is not None),
        grid=(n // tm,),
        in_specs=in_specs + [_full((1, D_MODEL)), _full((D_MODEL, D_IN_PAD)),
                             _full((1, SB_WIDTH)), _full((1, SB_WIDTH))],
        out_specs=([row(w) for w in widths]
                   + [pl.BlockSpec((1, SB_WIDTH, tm), lambda i: (i // per, 0, i % per))] * 2),
        out_shape=([jax.ShapeDtypeStruct((n, w), dt) for w, dt in zip(widths, dtypes)]
                   + [jax.ShapeDtypeStruct((n // seq, SB_WIDTH, seq), F32)] * 2),
        compiler_params=_params(("parallel",)),
        name="inproj",
    )(*ins, nw, w_pad, qw, kw)


SB_TK = 128
SB_PAIRS = SB_WIDTH // LANES


SB_U = 2
LOG2E = math.log2(math.e)


def _sb_weights(qs, kbs, carries, tri, masks):
    return _sb_weights_of(_sb_scores(qs, kbs), carries, tri, masks)


def _sb_scores(qs, kbs):
    z = []
    for q, kb in zip(qs, kbs):
        tk = kb[0].shape[0]
        zp = _dot(q, jnp.concatenate(kb, axis=0), NT)
        z.append([zp[:, u * tk:(u + 1) * tk] for u in range(len(kb))])
    return z


def _sb_weights_of(scores, carries, tri, masks):
    pu = [(p, u) for p in range(len(scores)) for u in range(len(scores[p]))]
    z = {k: scores[k[0]][k[1]] for k in pu}
    hl = {}
    for k in pu:
        sp = jnp.maximum(z[k], 0.0) + jnp.log(1.0 + jnp.exp2(jnp.abs(z[k]) * (-LOG2E)))
        if masks[k[1]] is not None:
            sp = jnp.where(masks[k[1]], sp, 0.0)
        hl[k] = sp.astype(BF16)
    cs = {k: _dot(hl[k], tri) for k in pu}
    weights, new_carries = [], []
    for p in range(len(scores)):
        carry, row = carries[p], []
        for u in range(len(scores[p])):
            incl = cs[(p, u)] + carry
            a = jnp.exp2((z[(p, u)] - incl) * LOG2E)
            if masks[u] is not None:
                a = jnp.where(masks[u], a, 0.0)
            row.append(a.astype(BF16))
            carry = jnp.broadcast_to(incl[:, 0:1], incl.shape)
        weights.append(row)
        new_carries.append(carry)
    return weights, new_carries


def _sb_values(a2, vbs, head_lo):
    parts = []
    for vb in vbs:
        zero = jnp.zeros_like(vb)
        parts += [jnp.where(head_lo, vb, zero), jnp.where(head_lo, zero, vb)]
    return _dot(a2, jnp.concatenate(parts, axis=0))


def _sb_all_pairs(q, diag_kv, n_diag, past_k, past_v, n_iter, tri, qs_s, a_s, acc_s, carry_s):
    tq, tk = q.shape[0], SB_TK
    head_lo = lax.broadcasted_iota(jnp.int32, (1, LANES), 1) < HEAD_DIM
    pairs = range(SB_PAIRS)
    qs = []
    for p in pairs:
        qp = q[:, p * LANES:(p + 1) * LANES]
        zero = jnp.zeros_like(qp)
        qs.append(jnp.concatenate([jnp.where(head_lo, qp, zero), jnp.where(head_lo, zero, qp)], axis=0))
        qs_s[p] = qs[p]
    rg = tq // n_diag
    row = lax.broadcasted_iota(jnp.int32, (2 * rg, tk), 0)
    col = lax.broadcasted_iota(jnp.int32, (2 * rg, tk), 1)
    causal = col < jnp.where(row >= rg, row - rg, row)
    diag = [[diag_kv(d, p) for d in range(n_diag)] for p in pairs]
    groups = [(p, r) for p in pairs for r in range(n_diag)]
    rows_of = lambda x, r: jnp.concatenate([x[r * rg:(r + 1) * rg], x[tq + r * rg:tq + (r + 1) * rg]], axis=0)
    ws, carries = _sb_weights([rows_of(qs[p], r) for p, r in groups],
                              [[diag[p][d][0] for d in range(r, -1, -1)] for p, r in groups],
                              [jnp.zeros((2 * rg, tk), F32)] * len(groups), tri, [causal] + [None] * (n_diag - 1))
    for p in pairs:
        accs, c_lo, c_hi = [], [], []
        for r in range(n_diag):
            g = groups.index((p, r))
            a2 = jnp.concatenate([half for a in ws[g] for half in (a[:rg], a[rg:])], axis=1)
            accs.append(_sb_values(a2, [diag[p][d][1] for d in range(r, -1, -1)], head_lo))
            c_lo.append(carries[g][:rg])
            c_hi.append(carries[g][rg:])
        acc_s[p] = jnp.concatenate(accs, axis=0)
        carry_s[p] = jnp.concatenate(c_lo + c_hi, axis=0)
    a_s[...] = jnp.zeros_like(a_s)

    def values(prev_start):
        for p in pairs:
            vbs = [past_v(prev_start + (SB_U - 1 - u) * tk, p) for u in range(SB_U)]
            acc_s[p] += _sb_values(a_s[p], vbs, head_lo)

    span = SB_U * tk

    def step(it):
        start = (n_iter - 1 - it) * span
        prev_start = jnp.where(it == 0, 0, (n_iter - it) * span)
        kbs = [[past_k(start + (SB_U - 1 - u) * tk, p) for u in range(SB_U)] for p in pairs]
        scores = _sb_scores([qs_s[p] for p in pairs], kbs)
        values(prev_start)
        ws, carries = _sb_weights_of(scores, [carry_s[p] for p in pairs], tri, [None] * SB_U)
        for p in pairs:
            for u in range(SB_U):
                a_s[p, :, (2 * u) * tk:(2 * u + 1) * tk] = ws[p][u][:tq]
                a_s[p, :, (2 * u + 1) * tk:(2 * u + 2) * tk] = ws[p][u][tq:]
            carry_s[p] = carries[p]

    odd = n_iter % 2

    @pl.when(odd == 1)
    def _():
        step(jnp.int32(0))

    def two_steps(i, c):
        step(odd + 2 * i)
        step(odd + 2 * i + 1)
        return c

    lax.fori_loop(0, n_iter // 2, two_steps, 0)
    values(0)
    return jnp.concatenate([acc_s[p] for p in range(SB_PAIRS)], axis=1)


def _sb_tri():
    j = lax.broadcasted_iota(jnp.int32, (SB_TK, SB_TK), 0)
    s = lax.broadcasted_iota(jnp.int32, (SB_TK, SB_TK), 1)
    return (j >= s).astype(BF16)


def _sb_scratch(tq):
    return [pltpu.VMEM((SB_PAIRS, 2 * tq, LANES), BF16), pltpu.VMEM((SB_PAIRS, tq, SB_U * 2 * SB_TK), BF16),
            pltpu.VMEM((SB_PAIRS, tq, LANES), F32), pltpu.VMEM((SB_PAIRS, 2 * tq, SB_TK), F32)]


def _sb_block_reader(ref2d):
    def read(start, p):
        if not isinstance(start, int):
            start = pl.multiple_of(start, SB_TK)
        return ref2d[pl.ds(start, SB_TK), p * LANES:(p + 1) * LANES].astype(BF16)
    return read


def _sb_prompt_kernel(q_ref, k_ref, v_ref, g_ref, tri_ref, o_ref, qs_s, a_s, acc_s, carry_s, *, tq):
    qi = pl.program_id(1)
    base = pl.multiple_of(qi * tq, tq)
    read_k, read_v = _sb_block_reader(k_ref.at[0]), _sb_block_reader(v_ref.at[0])

    def diag_kv(d, p):
        return read_k(base + d * SB_TK, p), read_v(base + d * SB_TK, p)

    n_iter = qi * (tq // (SB_U * SB_TK))
    y = _sb_all_pairs(q_ref[0], diag_kv, tq // SB_TK, read_k, read_v, n_iter, tri_ref[...],
                      qs_s, a_s, acc_s, carry_s)
    o_ref[0] = (y * jax.nn.silu(g_ref[0])).astype(BF16)


def _sb_prompt(q, k, v, g):
    b, l, _ = q.shape
    tq = SB_U * SB_TK
    qblk = pl.BlockSpec((1, tq, SB_WIDTH), lambda bi, qi: (bi, qi, 0))
    kvblk = pl.BlockSpec((1, l, SB_WIDTH), lambda bi, qi: (bi, 0, 0))
    return pl.pallas_call(
        functools.partial(_sb_prompt_kernel, tq=tq),
        grid=(b, l // tq),
        in_specs=[qblk, kvblk, kvblk, qblk, _full((SB_TK, SB_TK))],
        out_specs=qblk,
        out_shape=jax.ShapeDtypeStruct((b, l, SB_WIDTH), BF16),
        scratch_shapes=_sb_scratch(tq),
        compiler_params=_params(("parallel", "arbitrary")),
        name="sb_prompt",
    )(q, k, v, g, _sb_tri())


def _sb_sample_kernel(q_ref, k_ref, v_ref, ck_ref, cv_ref, g_ref, tri_ref, o_ref, qs_s, a_s, acc_s, carry_s,
                      *, n_iter):
    tq = q_ref.shape[1]

    def diag_kv(d, p):
        ls = slice(p * LANES, (p + 1) * LANES)
        pad = jnp.zeros((SB_TK - tq, LANES), BF16)
        return (jnp.concatenate([k_ref[0, :, ls].astype(BF16), pad], axis=0),
                jnp.concatenate([v_ref[0, :, ls].astype(BF16), pad], axis=0))

    y = _sb_all_pairs(q_ref[0], diag_kv, 1, _sb_block_reader(ck_ref.at[0, 0]), _sb_block_reader(cv_ref.at[0, 0]),
                      n_iter, tri_ref[...], qs_s, a_s, acc_s, carry_s)
    o_ref[0] = (y * jax.nn.silu(g_ref[0])).astype(BF16)


def _sb_sample(q, k, v, ck, cv, layer, g):
    b, l, _ = q.shape
    past = ck.shape[2]
    qblk = pl.BlockSpec((1, l, SB_WIDTH), lambda bi: (bi, 0, 0))
    cblk = pl.BlockSpec((1, 1, past, SB_WIDTH), lambda bi: (layer, bi, 0, 0))
    return pl.pallas_call(
        functools.partial(_sb_sample_kernel, n_iter=past // (SB_U * SB_TK)),
        grid=(b,),
        in_specs=[qblk, qblk, qblk, cblk, cblk, qblk, _full((SB_TK, SB_TK))],
        out_specs=qblk,
        out_shape=jax.ShapeDtypeStruct((b, l, SB_WIDTH), BF16),
        scratch_shapes=_sb_scratch(l),
        compiler_params=_params(("parallel",)),
        name="sb_sample",
    )(q, k, v, ck, cv, g, _sb_tri())


def _s5_kernel(u_ref, g_ref, h0r_ref, h0i_ref, ar_ref, ai_ref, b_ref, c_ref, d_ref,
               wg_ref, bg_ref, o_ref, hr_o, hi_o, bu_s, h_s, ut_s, gt_s, ot_s, *, tt):
    ti = pl.program_id(1)
    rows = tt * SUBLANES
    slabs = SSM_WIDTH // LANES

    @pl.when(ti == 0)
    def _():
        hr_o[...] = h0r_ref[...]
        hi_o[...] = h0i_ref[...]

    def time_major(ref, slab_s):
        for s in range(SUBLANES):
            for j in range(slabs):
                slab_s[j, pl.ds(s, tt, stride=SUBLANES), :] = ref[s, :, j * LANES:(j + 1) * LANES]
        return jnp.concatenate([slab_s[j] for j in range(slabs)], axis=1)

    u = time_major(u_ref, ut_s)
    bu_s[...] = _dot(u.astype(BF16), b_ref[...])

    ar = jnp.broadcast_to(ar_ref[...], (SUBLANES, SSM_FLAT))
    ai = jnp.broadcast_to(ai_ref[...], (SUBLANES, SSM_FLAT))

    def step(t, carry):
        hr, hi = carry
        r0 = pl.multiple_of(t * SUBLANES, SUBLANES)
        br = bu_s[pl.ds(r0, SUBLANES), 0:SSM_FLAT]
        bi = bu_s[pl.ds(r0, SUBLANES), SSM_FLAT:2 * SSM_FLAT]
        nr = ar * hr - ai * hi + br
        ni = ar * hi + ai * hr + bi
        h_s[pl.ds(r0, SUBLANES), 0:SSM_FLAT] = nr
        h_s[pl.ds(r0, SUBLANES), SSM_FLAT:2 * SSM_FLAT] = ni
        return nr, ni

    hr, hi = lax.fori_loop(0, tt, step, (hr_o[...], hi_o[...]))
    hr_o[...] = hr
    hi_o[...] = hi

    y = _dot(h_s[...].astype(BF16), c_ref[...]) + d_ref[...] * u
    y = jax.nn.gelu(y)
    z = _dot(y.astype(BF16), wg_ref[...]) + bg_ref[...]
    out = z[:, :SSM_WIDTH] * jax.nn.sigmoid(z[:, SSM_WIDTH:])
    out = out * jax.nn.silu(time_major(g_ref, gt_s))
    for j in range(slabs):
        ot_s[j] = out[:, j * LANES:(j + 1) * LANES]
    for s in range(SUBLANES):
        for j in range(slabs):
            o_ref[s, :, j * LANES:(j + 1) * LANES] = ot_s[j, pl.ds(s, tt, stride=SUBLANES), :].astype(BF16)


def _s5(u, g, h0r, h0i, sp):
    b, l, _ = u.shape
    tt = min(128, l)
    rows = tt * SUBLANES
    ublk = pl.BlockSpec((SUBLANES, tt, SSM_WIDTH), lambda bi, ti: (bi, ti, 0))
    sblk = pl.BlockSpec((SUBLANES, SSM_FLAT), lambda bi, ti: (bi, 0))
    slab = pltpu.VMEM((SSM_WIDTH // LANES, rows, LANES), F32)
    return pl.pallas_call(
        functools.partial(_s5_kernel, tt=tt),
        grid=(b // SUBLANES, l // tt),
        in_specs=[ublk, ublk, sblk, sblk,
                  _full((1, SSM_FLAT)), _full((1, SSM_FLAT)),
                  _full((SSM_WIDTH, 2 * SSM_FLAT)),
                  _full((2 * SSM_FLAT, SSM_WIDTH)), _full((1, SSM_WIDTH)),
                  _full((SSM_WIDTH, 2 * SSM_WIDTH)), _full((1, 2 * SSM_WIDTH))],
        out_specs=[ublk, sblk, sblk],
        out_shape=[jax.ShapeDtypeStruct((b, l, SSM_WIDTH), BF16),
                   jax.ShapeDtypeStruct((b, SSM_FLAT), F32),
                   jax.ShapeDtypeStruct((b, SSM_FLAT), F32)],
        scratch_shapes=[pltpu.VMEM((rows, 2 * SSM_FLAT), F32), pltpu.VMEM((rows, 2 * SSM_FLAT), F32),
                        slab, slab, slab],
        compiler_params=_params(("parallel", "arbitrary")),
        name="s5",
    )(u, g, h0r, h0i, sp["ar"], sp["ai"], sp["b"], sp["c"], sp["d"], sp["w_glu"], sp["b_glu"])


def _s5_params(lam_re, lam_im, log_dt, b_re, b_im, c_re, c_im, d, w_glu, b_glu):
    dt = jnp.exp(log_dt.astype(F32))[:, None]
    lr = jnp.minimum(lam_re.astype(F32), -1e-4)
    li = lam_im.astype(F32)
    er = jnp.exp(lr * dt)
    ar, ai = er * jnp.cos(li * dt), er * jnp.sin(li * dt)
    den = lr * lr + li * li
    fr = ((ar - 1.0) * lr + ai * li) / den
    fi = (ai * lr - (ar - 1.0) * li) / den
    br, bi = b_re.astype(F32), b_im.astype(F32)
    bbr = fr[..., None] * br - fi[..., None] * bi
    bbi = fr[..., None] * bi + fi[..., None] * br
    eye = jnp.eye(SSM_GROUPS, dtype=F32)
    bd_in = lambda m: jnp.einsum("gpc,gh->gchp", m, eye).reshape(SSM_WIDTH, SSM_FLAT)
    bd_out = lambda m: jnp.einsum("gcp,gh->gphc", m, eye).reshape(SSM_FLAT, SSM_WIDTH)
    b_all = jnp.concatenate([bd_in(bbr), bd_in(bbi)], axis=1)
    c_all = jnp.concatenate([bd_out(c_re.astype(F32)), -bd_out(c_im.astype(F32))], axis=0)
    return dict(ar=ar.reshape(1, SSM_FLAT), ai=ai.reshape(1, SSM_FLAT), b=b_all.astype(BF16),
                c=c_all.astype(BF16), d=d.astype(F32).reshape(1, SSM_WIDTH),
                w_glu=w_glu.astype(BF16), b_glu=b_glu.astype(F32).reshape(1, 2 * SSM_WIDTH))


RWKV_CK = 64
RWKV_NB = 8


def _rwkv_kernel(cp_ref, g_ref, prev_ref, s0_ref, mu_ref, w0_ref, w2_ref, a0_ref, a2_ref, kk_ref, ka_ref,
                 ub_ref, lnw_ref, lnb_ref, ho_ref, tri_ref, ones_ref, o_ref, s_o, prev_s,
                 khw_s, rw_s, ktl_s, btl_s, v_s, kht_s, bht_s, wch_s, bonus_s, y_s, *, tt, ck):
    ti = pl.program_id(1)

    @pl.when(ti == 0)
    def _():
        s_o[...] = s0_ref[...]
        prev_s[...] = jnp.broadcast_to(prev_ref[...], prev_s.shape)

    w = RWKV_WIDTH
    nb_all = cp_ref.shape[0]
    head_ones, chunk_tri, chunk_ones = ho_ref[...], tri_ref[...], ones_ref[...]

    def prep(nb, carry):
        cp = cp_ref[nb]
        trow = lax.broadcasted_iota(jnp.int32, (tt, 1), 0)
        shifted = jnp.where(trow == 0, prev_s[nb, 0:1, :], pltpu.roll(cp, 1, 0))
        prev_s[nb] = jnp.broadcast_to(cp[tt - 1:tt, :], (SUBLANES, CP_PAD))
        xc = cp + mu_ref[...] * (shifted - cp)
        r, k, v, lora = xc[:, 0:w], xc[:, w:2 * w], xc[:, 2 * w:3 * w], xc[:, 3 * w:]
        logw = -DECAY_SCALE * jax.nn.sigmoid(w0_ref[...] + _dot(jnp.tanh(lora).astype(BF16), w2_ref[...]))
        a = jax.nn.sigmoid(a0_ref[...] + _dot(lora.astype(BF16), a2_ref[...]))
        cum = _dot2r(chunk_tri, logw)
        tot = _dot2r(chunk_ones, logw)
        kkv = k * kk_ref[...]
        kh = kkv * lax.rsqrt(_dot2l(kkv * kkv, head_ones) + 1e-12)
        kt = k * (1.0 + (a - 1.0) * ka_ref[...])
        bvec = a * kh
        w_inv = jnp.exp(-cum)
        w_end = jnp.exp(tot - cum)
        khw_s[nb] = (kh * jnp.exp(cum - logw)).astype(BF16)
        rw_s[nb] = (r * jnp.exp(cum)).astype(BF16)
        ktl_s[nb] = (kt * w_inv).astype(BF16)
        btl_s[nb] = (bvec * w_inv).astype(BF16)
        v_s[nb] = v.astype(BF16)
        kht_s[nb] = (kt * w_end).astype(BF16)
        bht_s[nb] = (bvec * w_end).astype(BF16)
        wch_s[nb] = jnp.exp(tot)
        bonus_s[nb] = _dot2l(r * kt * ub_ref[...], head_ones) * v
        return carry

    lax.fori_loop(0, nb_all, prep, 0)

    lane = lax.broadcasted_iota(jnp.int32, (1, LANES), 1)
    m_lo = lane < HEAD_DIM
    ri = lax.broadcasted_iota(jnp.int32, (2 * ck, 2 * ck), 0)
    cj = lax.broadcasted_iota(jnp.int32, (2 * ck, 2 * ck), 1)
    same_head = (ri >= ck) == (cj >= ck)
    strict = same_head & (cj < ri)
    lower = same_head & (cj <= ri)
    eye = (ri == cj).astype(F32)
    rr = lax.broadcasted_iota(jnp.int32, (2 * ck, LANES), 0)
    rl = lax.broadcasted_iota(jnp.int32, (2 * ck, LANES), 1)
    own_lanes = (rr >= ck) == (rl >= HEAD_DIM)
    sr = lax.broadcasted_iota(jnp.int32, (LANES, LANES), 0) // HEAD_DIM
    sc = lax.broadcasted_iota(jnp.int32, (LANES, LANES), 1) // HEAD_DIM
    blockdiag = sr == sc
    dot1 = lambda x, y: _dot(x.astype(BF16), y.astype(BF16))

    cat = lambda xs: jnp.concatenate(xs, axis=0)

    def chunk_step(c, carry):
        r0 = pl.multiple_of(c * ck, ck)
        rows = pl.ds(r0, ck)
        chains = [(nb, hp) for nb in range(nb_all) for hp in range(w // LANES)]
        lanes = lambda hp: slice(hp * LANES, (hp + 1) * LANES)
        rd = lambda ref: [ref[nb, rows, lanes(hp)] for nb, hp in chains]
        khw, rw, ktl, btl, v = rd(khw_s), rd(rw_s), rd(ktl_s), rd(btl_s), rd(v_s)
        z = jnp.zeros_like(khw[0])
        l4 = [cat([jnp.where(m_lo, a, z), jnp.where(m_lo, z, a), jnp.where(m_lo, b, z), jnp.where(m_lo, z, b)])
              for a, b in zip(khw, rw)]
        scb = [_dot(a, cat([b, b]), NT) for a, b in zip(l4, btl)]
        sck = [_dot(a, cat([b, b]), NT) for a, b in zip(l4, ktl)]
        nmat = [jnp.where(strict, a[:2 * ck], 0.0) for a in scb]
        arb = [jnp.where(lower, a[2 * ck:], 0.0) for a in scb]
        akk = [jnp.where(strict, a[:2 * ck], 0.0) for a in sck]
        ark = [jnp.where(lower, a[2 * ck:], 0.0) for a in sck]
        inv = [eye - a for a in nmat]
        pw = nmat
        steps = 1
        while 2 * steps < ck:
            pw = [dot1(a, a) for a in pw]
            inv = [dot1(a, eye + b) for a, b in zip(inv, pw)]
            steps *= 2
        s_pair = [s_o[nb, hp] for nb, hp in chains]
        ks = [_dot(cat([a, b]), s.astype(BF16), NT) for a, b, s in zip(khw, rw, s_pair)]
        av = [dot1(cat([a, b]), cat([x, x])) for a, b, x in zip(akk, ark, v)]
        sa2 = [jnp.where(own_lanes, dot1(t, cat([k[:ck], k[:ck]]) + a[:2 * ck]), 0.0)
               for t, k, a in zip(inv, ks, av)]
        y2 = [jnp.where(own_lanes, a[2 * ck:] - dot1(b, s), 0.0) for a, b, s in zip(av, arb, sa2)]
        for (nb, hp), k, y in zip(chains, ks, y2):
            y_s[nb, rows, lanes(hp)] = k[ck:] + y[:ck] + y[ck:]
        kht, bht = rd(kht_s), rd(bht_s)
        upd = [_dot(cat([x, (-(s[:ck] + s[ck:])).astype(BF16)]), cat([a, b]), TN)
               for x, s, a, b in zip(v, sa2, kht, bht)]
        for (nb, hp), s, u in zip(chains, s_pair, upd):
            s_o[nb, hp] = s * wch_s[nb, pl.ds(r0, 1), lanes(hp)] + jnp.where(blockdiag, u, 0.0)
        return carry

    lax.fori_loop(0, tt // ck, chunk_step, 0)

    fin = 2 if nb_all % 2 == 0 else 1

    def finish(i, carry):
        nbs = [i * fin + j for j in range(fin)]
        ys = [y_s[nb] for nb in nbs]
        mean = [_dot2l(y, head_ones) * (1.0 / HEAD_DIM) for y in ys]
        cen = [y - m for y, m in zip(ys, mean)]
        var = [_dot2l(c * c, head_ones) * (1.0 / HEAD_DIM) for c in cen]
        for nb, c, v in zip(nbs, cen, var):
            y = c * lax.rsqrt(v + GN_EPS) * lnw_ref[...] + lnb_ref[...] + bonus_s[nb]
            o_ref[nb] = (y * jax.nn.silu(g_ref[nb])).astype(BF16)
        return carry

    lax.fori_loop(0, nb_all // fin, finish, 0)


def _rwkv(cp, g, prev, s0p, rp):
    b, l, _ = cp.shape
    tt = min(256, l)
    ck = min(RWKV_CK, tt)
    nb = min(RWKV_NB, b)
    tblk = lambda w: pl.BlockSpec((nb, tt, w), lambda bi, ti: (bi, ti, 0))
    sblk = pl.BlockSpec((nb, 2, LANES, LANES), lambda bi, ti: (bi, 0, 0, 0))
    vec = _full((1, RWKV_WIDTH))
    hl = lax.broadcasted_iota(jnp.int32, (RWKV_WIDTH, RWKV_WIDTH), 0) // HEAD_DIM
    hc = lax.broadcasted_iota(jnp.int32, (RWKV_WIDTH, RWKV_WIDTH), 1) // HEAD_DIM
    tr = lax.broadcasted_iota(jnp.int32, (tt, tt), 0)
    tc = lax.broadcasted_iota(jnp.int32, (tt, tt), 1)
    same_chunk = (tr // ck) == (tc // ck)
    slab = lambda dt: pltpu.VMEM((nb, tt, RWKV_WIDTH), dt)
    return pl.pallas_call(
        functools.partial(_rwkv_kernel, tt=tt, ck=ck),
        grid=(b // nb, l // tt),
        in_specs=[tblk(CP_PAD), tblk(RWKV_WIDTH),
                  pl.BlockSpec((nb, 1, CP_PAD), lambda bi, ti: (bi, 0, 0)), sblk,
                  _full((1, CP_PAD)), vec, _full((LANES, RWKV_WIDTH)), vec, _full((LANES, RWKV_WIDTH)),
                  vec, vec, vec, vec, vec,
                  _full((RWKV_WIDTH, RWKV_WIDTH)), _full((tt, tt)), _full((tt, tt))],
        out_specs=[tblk(RWKV_WIDTH), sblk],
        out_shape=[jax.ShapeDtypeStruct((b, l, RWKV_WIDTH), BF16),
                   jax.ShapeDtypeStruct((b, 2, LANES, LANES), F32)],
        scratch_shapes=[pltpu.VMEM((nb, SUBLANES, CP_PAD), F32)] + [slab(BF16)] * 7 + [slab(F32)] * 3,
        compiler_params=_params(("parallel", "arbitrary")),
        name="rwkv",
    )(cp, g, prev, s0p, rp["mu"], rp["w0"], rp["w2"], rp["a0"], rp["a2"], rp["k_k"], rp["k_a"],
      rp["u"], rp["ln_w"], rp["ln_b"],
      (hl == hc).astype(BF16), (same_chunk & (tc <= tr)).astype(BF16), same_chunk.astype(BF16))


def _rwkv_params(mu, w0, w2, a0, a2, k_k, k_a, u, ln_w, ln_b):
    vec = lambda t: t.astype(F32).reshape(1, RWKV_WIDTH)
    mu_p = jnp.zeros((1, CP_PAD), F32).at[0, :RWKV_SHIFT].set(mu.astype(F32))
    w2_p = jnp.zeros((LANES, RWKV_WIDTH), F32).at[:RWKV_LORA].set(w2.astype(F32)).astype(BF16)
    a2_p = jnp.zeros((LANES, RWKV_WIDTH), F32).at[RWKV_LORA:2 * RWKV_LORA].set(a2.astype(F32)).astype(BF16)
    return dict(mu=mu_p, w0=vec(w0), w2=w2_p, a0=vec(a0), a2=a2_p, k_k=vec(k_k), k_a=vec(k_a),
                u=vec(u), ln_w=vec(ln_w), ln_b=vec(ln_b))


def _pair_state(s):
    b = s.shape[0]
    s = s.astype(F32).reshape(b, 2, 2, HEAD_DIM, HEAD_DIM)
    eye = jnp.eye(2, dtype=F32)
    return jnp.einsum("bphvk,hg->bphvgk", s, eye).reshape(b, 2, LANES, LANES)


def _unpair_state(sp):
    b = sp.shape[0]
    s = sp.reshape(b, 2, 2, HEAD_DIM, 2, HEAD_DIM)
    return jnp.stack([s[:, :, 0, :, 0, :], s[:, :, 1, :, 1, :]], axis=2).reshape(b, RWKV_HEADS, HEAD_DIM, HEAD_DIM)


def _outproj_kernel(x_ref, a_ref, b_ref, c_ref, w_ref, o_ref):
    o_ref[...] = x_ref[...] + _mix_out(a_ref, b_ref, c_ref, w_ref)


def _outproj(x2d, ma, mb, mc, w_out):
    n = x2d.shape[0]
    tm = min(512, n)
    row = lambda w: pl.BlockSpec((tm, w), lambda i: (i, 0))
    return pl.pallas_call(
        _outproj_kernel,
        grid=(n // tm,),
        in_specs=[row(D_MODEL), row(SB_WIDTH), row(SSM_WIDTH), row(RWKV_WIDTH), _full((D_MODEL, D_MODEL))],
        out_specs=row(D_MODEL),
        out_shape=jax.ShapeDtypeStruct((n, D_MODEL), F32),
        compiler_params=_params(("parallel",)),
        name="outproj",
    )(x2d, ma, mb, mc, w_out)


def _pad_w_in(w_in):
    w = w_in.astype(F32)
    pad = jnp.zeros((D_MODEL, CP_PAD - RWKV_SHIFT), F32)
    return jnp.concatenate([w[:, :COL_CP + RWKV_SHIFT], pad, w[:, COL_CP + RWKV_SHIFT:]], axis=1).astype(BF16)


def _layer(x2d, b, l, lp, past, prev_out):
    n = b * l
    outs = _inproj(x2d, l, lp["norm_w"], lp["w_in"], lp["q_w"], lp["k_w"], prev_out)
    if prev_out is not None:
        x2d, outs = outs[0], outs[1:]
    q, k, v, ga, u, gb, cp, gc, kt, vt = outs
    r3 = lambda t: t.reshape(b, l, t.shape[-1])
    q, k, v, ga, u, gb, cp, gc = map(r3, (q, k, v, ga, u, gb, cp, gc))

    if past is None:
        mix_a = _sb_prompt(q, k, v, ga)
        h0r = h0i = jnp.zeros((b, SSM_FLAT), F32)
        prev = jnp.zeros((b, 1, CP_PAD), F32)
        s0p = jnp.zeros((b, 2, LANES, LANES), F32)
    else:
        mix_a = _sb_sample(q, k, v, past["k"], past["v"], past["layer"], ga)
        h0r = past["ssm_re"].astype(F32).reshape(b, SSM_FLAT)
        h0i = past["ssm_im"].astype(F32).reshape(b, SSM_FLAT)
        prev = jnp.pad(past["shift"].astype(F32), ((0, 0), (0, CP_PAD - RWKV_SHIFT)))[:, None, :]
        s0p = _pair_state(past["wkv"])

    mix_b, hr, hi = _s5(u, gb, h0r, h0i, lp["s5"])
    mix_c, s_pair = _rwkv(cp, gc, prev, s0p, lp["rwkv"])

    pending = (mix_a.reshape(n, SB_WIDTH), mix_b.reshape(n, SSM_WIDTH), mix_c.reshape(n, RWKV_WIDTH), lp["w_out"])
    heads = lambda t: jnp.transpose(t.reshape(b, SB_WIDTH // HEAD_DIM, HEAD_DIM, l), (0, 3, 1, 2))
    state = (heads(kt), heads(vt), hr.reshape(b, SSM_GROUPS, SSM_STATE), hi.reshape(b, SSM_GROUPS, SSM_STATE),
             _unpair_state(s_pair), cp[:, -1, :RWKV_SHIFT])
    return x2d, pending, state


def kernel(x_prompt, x_sample, cache_k, cache_v, state_ssm_re, state_ssm_im, state_wkv, state_shift, norm_w, w_in, q_norm_w, k_norm_w, ssm_lambda_re, ssm_lambda_im, ssm_log_dt, ssm_b_re, ssm_b_im, ssm_c_re, ssm_c_im, ssm_d, ssm_w_glu, ssm_b_glu, rwkv_mu, rwkv_w0, rwkv_w2, rwkv_a0, rwkv_a2, rwkv_k_k, rwkv_k_a, rwkv_u, rwkv_ln_w, rwkv_ln_b, w_out):
    depth = w_in.shape[0]
    (bp, lp_len, _), (bs, ls_len, _) = x_prompt.shape, x_sample.shape
    xp, xs = x_prompt.reshape(bp * lp_len, D_MODEL), x_sample.reshape(bs * ls_len, D_MODEL)
    out_p = out_s = None
    ck_all = cache_k.reshape(cache_k.shape[:3] + (SB_WIDTH,))
    cv_all = cache_v.reshape(cache_v.shape[:3] + (SB_WIDTH,))
    p_states, s_states = [], []
    for i in range(depth):
        tile8 = lambda t: jnp.tile(t.astype(F32), SB_WIDTH // HEAD_DIM).reshape(1, SB_WIDTH)
        lp = dict(
            norm_w=norm_w[i].astype(F32).reshape(1, D_MODEL), w_in=_pad_w_in(w_in[i]),
            q_w=tile8(q_norm_w[i]), k_w=tile8(k_norm_w[i]),
            s5=_s5_params(ssm_lambda_re[i], ssm_lambda_im[i], ssm_log_dt[i], ssm_b_re[i], ssm_b_im[i],
                          ssm_c_re[i], ssm_c_im[i], ssm_d[i], ssm_w_glu[i], ssm_b_glu[i]),
            rwkv=_rwkv_params(rwkv_mu[i], rwkv_w0[i], rwkv_w2[i], rwkv_a0[i], rwkv_a2[i], rwkv_k_k[i],
                              rwkv_k_a[i], rwkv_u[i].reshape(-1), rwkv_ln_w[i], rwkv_ln_b[i]),
            w_out=w_out[i].astype(BF16))
        xp, out_p, st_p = _layer(xp, bp, lp_len, lp, None, out_p)
        past = dict(k=ck_all, v=cv_all, layer=i,
                    ssm_re=state_ssm_re[i], ssm_im=state_ssm_im[i], wkv=state_wkv[i], shift=state_shift[i])
        xs, out_s, st_s = _layer(xs, bs, ls_len, lp, past, out_s)
        p_states.append(st_p)
        s_states.append(st_s)
    xp = _outproj(xp, *out_p).reshape(x_prompt.shape)
    xs = _outproj(xs, *out_s).reshape(x_sample.shape)
    stack = lambda states, j: jnp.stack([s[j] for s in states], axis=0)
    return ((xp, xs) + tuple(stack(p_states, j) for j in range(6))
            + tuple(stack(s_states, j) for j in range(6)))
```

```python
import functools
import math

import jax
import jax.numpy as jnp
from jax import lax
from jax.experimental import pallas as pl
from jax.experimental.pallas import tpu as pltpu

F32 = jnp.float32
BF16 = jnp.bfloat16

D_MODEL = 1024
HEAD_DIM = 64
SB_WIDTH = D_MODEL // 2
SSM_WIDTH = D_MODEL // 4
SSM_GROUP = 16
SSM_GROUPS = SSM_WIDTH // SSM_GROUP
SSM_STATE = 64
SSM_FLAT = SSM_GROUPS * SSM_STATE
RWKV_WIDTH = D_MODEL // 4
RWKV_HEADS = RWKV_WIDTH // HEAD_DIM
RWKV_LORA = 32
RWKV_SHIFT = 3 * RWKV_WIDTH + 2 * RWKV_LORA
RMS_EPS = 1e-6
GN_EPS = 64e-5
DECAY_SCALE = math.exp(-0.5)

LANES = 128
SUBLANES = 8
CP_PAD = 7 * LANES
COL_Q, COL_K, COL_V, COL_GA = 0, 512, 1024, 1536
COL_U, COL_GB, COL_CP, COL_GC = 2048, 2304, 2560, 2560 + CP_PAD
D_IN_PAD = COL_GC + RWKV_WIDTH
VMEM_LIMIT = 56 * 1024 * 1024

NN = (((1,), (0,)), ((), ()))
NT = (((1,), (1,)), ((), ()))
TN = (((0,), (0,)), ((), ()))


def _dot(a, b, dims=NN):
    return lax.dot_general(a, b, dims, preferred_element_type=F32)


def _split(a):
    hi = a.astype(BF16)
    lo = (a - hi.astype(F32)).astype(BF16)
    return hi, lo


def _dot2l(a, b_exact, dims=NN):
    ah, al = _split(a)
    return _dot(ah, b_exact, dims) + _dot(al, b_exact, dims)


def _dot2r(a_exact, b, dims=NN):
    bh, bl = _split(b)
    return _dot(a_exact, bh, dims) + _dot(a_exact, bl, dims)


def _params(sem):
    return pltpu.CompilerParams(dimension_semantics=sem, vmem_limit_bytes=VMEM_LIMIT)


def _full(shape):
    nd = len(shape)
    return pl.BlockSpec(shape, lambda *_: (0,) * nd)


def _head_rms(t, w):
    lane = lax.broadcasted_iota(jnp.int32, (1, LANES), 1)
    lo = lane < HEAD_DIM
    outs = []
    for j in range(t.shape[1] // LANES):
        blk = t[:, j * LANES:(j + 1) * LANES]
        sq = blk * blk
        s_lo = jnp.sum(jnp.where(lo, sq, 0.0), axis=-1, keepdims=True)
        s_hi = jnp.sum(jnp.where(lo, 0.0, sq), axis=-1, keepdims=True)
        ms = jnp.where(lo, s_lo, s_hi) * (1.0 / HEAD_DIM)
        outs.append(blk * lax.rsqrt(ms + RMS_EPS) * w[:, j * LANES:(j + 1) * LANES])
    return jnp.concatenate(outs, axis=-1)


def _mix_out(a_ref, b_ref, c_ref, w_ref):
    acc = _dot(a_ref[...], w_ref[0:SB_WIDTH, :])
    acc = acc + _dot(b_ref[...], w_ref[SB_WIDTH:SB_WIDTH + SSM_WIDTH, :])
    return acc + _dot(c_ref[...], w_ref[SB_WIDTH + SSM_WIDTH:, :])


def _inproj_kernel(*refs, prev_out, aliased):
    n_in = 5 + (4 if prev_out else 0) + (2 if aliased else 0)
    ins, outs = refs[:n_in], refs[n_in:]
    if prev_out:
        x_ref, a_ref, b_ref, c_ref, wo_ref, nw_ref, w_ref, qw_ref, kw_ref = ins[:9]
        xo_ref, outs = outs[0], outs[1:]
        x = x_ref[...] + _mix_out(a_ref, b_ref, c_ref, wo_ref)
        xo_ref[...] = x
    else:
        x_ref, nw_ref, w_ref, qw_ref, kw_ref = ins[:5]
        x = x_ref[...]
    q_o, k_o, v_o, ga_o, u_o, gb_o, cp_o, gc_o, kt_o, vt_o = outs
    ms = jnp.mean(x * x, axis=-1, keepdims=True)
    h = (x * lax.rsqrt(ms + RMS_EPS) * nw_ref[...]).astype(BF16)

    def proj(a, b):
        return _dot(h, w_ref[:, a:b])

    q = _head_rms(proj(COL_Q, COL_K), qw_ref[...])
    q_o[...] = (q * (HEAD_DIM ** -0.5)).astype(BF16)
    k = _head_rms(proj(COL_K, COL_V), kw_ref[...])
    v = proj(COL_V, COL_GA)
    k_o[...] = k
    v_o[...] = v
    kt_o[0, 0] = k.T
    vt_o[0, 0] = v.T
    ga_o[...] = proj(COL_GA, COL_U)
    u_o[...] = proj(COL_U, COL_GB)
    gb_o[...] = proj(COL_GB, COL_CP)
    cp_o[...] = proj(COL_CP, COL_GC)
    gc_o[...] = proj(COL_GC, D_IN_PAD)


def _inproj(x2d, seq, nw, w_pad, qw, kw, prev_out=None, kv_all=None, layer=0, depth=1):
    n = x2d.shape[0]
    tm = min(512, seq)
    per = seq // tm
    widths = (SB_WIDTH, SB_WIDTH, SB_WIDTH, SB_WIDTH, SSM_WIDTH, SSM_WIDTH, CP_PAD, RWKV_WIDTH)
    dtypes = (BF16,) + (F32,) * 7
    row = lambda w: pl.BlockSpec((tm, w), lambda i: (i, 0))
    ins, in_specs = [x2d], [row(D_MODEL)]
    if prev_out is not None:
        ins += list(prev_out)
        in_specs += [row(SB_WIDTH), row(SSM_WIDTH), row(RWKV_WIDTH), _full((D_MODEL, D_MODEL))]
        widths, dtypes = (D_MODEL,) + widths, (F32,) + dtypes
    ins += [nw, w_pad, qw, kw]
    in_specs += [_full((1, D_MODEL)), _full((D_MODEL, D_IN_PAD)), _full((1, SB_WIDTH)), _full((1, SB_WIDTH))]
    aliases = {}
    if kv_all is not None:
        aliases = {len(ins): len(widths), len(ins) + 1: len(widths) + 1}
        ins += list(kv_all)
        in_specs += [pl.BlockSpec(memory_space=pl.ANY)] * 2
    return pl.pallas_call(
        functools.partial(_inproj_kernel, prev_out=prev_out is not None, aliased=kv_all is not None),
        grid=(n // tm,),
        in_specs=in_specs,
        out_specs=([row(w) for w in widths]
                   + [pl.BlockSpec((1, 1, SB_WIDTH, tm), lambda i: (layer, i // per, 0, i % per))] * 2),
        out_shape=([jax.ShapeDtypeStruct((n, w), dt) for w, dt in zip(widths, dtypes)]
                   + [jax.ShapeDtypeStruct((depth, n // seq, SB_WIDTH, seq), F32)] * 2),
        input_output_aliases=aliases,
        compiler_params=_params(("parallel",)),
        name="inproj",
    )(*ins)


SB_TK = 128
SB_PAIRS = SB_WIDTH // LANES


SB_U = 2
LOG2E = math.log2(math.e)


def _sb_weights(qs, kbs, carries, tri, masks):
    return _sb_weights_of(_sb_scores(qs, kbs), carries, tri, masks)


def _sb_scores(qs, kbs):
    z = []
    for q, kb in zip(qs, kbs):
        tk = kb[0].shape[0]
        zp = _dot(q, jnp.concatenate(kb, axis=0), NT)
        z.append([zp[:, u * tk:(u + 1) * tk] for u in range(len(kb))])
    return z


def _sb_weights_of(scores, carries, tri, masks):
    pu = [(p, u) for p in range(len(scores)) for u in range(len(scores[p]))]
    z = {k: scores[k[0]][k[1]] for k in pu}
    hl = {}
    for k in pu:
        sp = jnp.maximum(z[k], 0.0) + jnp.log(1.0 + jnp.exp2(jnp.abs(z[k]) * (-LOG2E)))
        if masks[k[1]] is not None:
            sp = jnp.where(masks[k[1]], sp, 0.0)
        hl[k] = sp.astype(BF16)
    cs = {k: _dot(hl[k], tri) for k in pu}
    weights, new_carries = [], []
    for p in range(len(scores)):
        carry, row = carries[p], []
        for u in range(len(scores[p])):
            incl = cs[(p, u)] + carry
            a = jnp.exp2((z[(p, u)] - incl) * LOG2E)
            if masks[u] is not None:
                a = jnp.where(masks[u], a, 0.0)
            row.append(a.astype(BF16))
            carry = jnp.broadcast_to(incl[:, 0:1], incl.shape)
        weights.append(row)
        new_carries.append(carry)
    return weights, new_carries


def _sb_values(a2, vbs, head_lo):
    parts = []
    for vb in vbs:
        zero = jnp.zeros_like(vb)
        parts += [jnp.where(head_lo, vb, zero), jnp.where(head_lo, zero, vb)]
    return _dot(a2, jnp.concatenate(parts, axis=0))


def _sb_all_pairs(q, diag_kv, n_diag, past_k, past_v, n_iter, tri, qs_s, a_s, acc_s, carry_s):
    tq, tk = q.shape[0], SB_TK
    head_lo = lax.broadcasted_iota(jnp.int32, (1, LANES), 1) < HEAD_DIM
    pairs = range(SB_PAIRS)
    qs = []
    for p in pairs:
        qp = q[:, p * LANES:(p + 1) * LANES]
        zero = jnp.zeros_like(qp)
        qs.append(jnp.concatenate([jnp.where(head_lo, qp, zero), jnp.where(head_lo, zero, qp)], axis=0))
        qs_s[p] = qs[p]
    rg = tq // n_diag
    row = lax.broadcasted_iota(jnp.int32, (2 * rg, tk), 0)
    col = lax.broadcasted_iota(jnp.int32, (2 * rg, tk), 1)
    causal = col < jnp.where(row >= rg, row - rg, row)
    diag = [[diag_kv(d, p) for d in range(n_diag)] for p in pairs]
    groups = [(p, r) for p in pairs for r in range(n_diag)]
    rows_of = lambda x, r: jnp.concatenate([x[r * rg:(r + 1) * rg], x[tq + r * rg:tq + (r + 1) * rg]], axis=0)
    ws, carries = _sb_weights([rows_of(qs[p], r) for p, r in groups],
                              [[diag[p][d][0] for d in range(r, -1, -1)] for p, r in groups],
                              [jnp.zeros((2 * rg, tk), F32)] * len(groups), tri, [causal] + [None] * (n_diag - 1))
    for p in pairs:
        accs, c_lo, c_hi = [], [], []
        for r in range(n_diag):
            g = groups.index((p, r))
            a2 = jnp.concatenate([half for a in ws[g] for half in (a[:rg], a[rg:])], axis=1)
            accs.append(_sb_values(a2, [diag[p][d][1] for d in range(r, -1, -1)], head_lo))
            c_lo.append(carries[g][:rg])
            c_hi.append(carries[g][rg:])
        acc_s[p] = jnp.concatenate(accs, axis=0)
        carry_s[p] = jnp.concatenate(c_lo + c_hi, axis=0)
    a_s[...] = jnp.zeros_like(a_s)

    def values(prev_start):
        for p in pairs:
            vbs = [past_v(prev_start + (SB_U - 1 - u) * tk, p) for u in range(SB_U)]
            acc_s[p] += _sb_values(a_s[p], vbs, head_lo)

    span = SB_U * tk

    def step(it):
        start = (n_iter - 1 - it) * span
        prev_start = jnp.where(it == 0, 0, (n_iter - it) * span)
        kbs = [[past_k(start + (SB_U - 1 - u) * tk, p) for u in range(SB_U)] for p in pairs]
        scores = _sb_scores([qs_s[p] for p in pairs], kbs)
        values(prev_start)
        ws, carries = _sb_weights_of(scores, [carry_s[p] for p in pairs], tri, [None] * SB_U)
        for p in pairs:
            for u in range(SB_U):
                a_s[p, :, (2 * u) * tk:(2 * u + 1) * tk] = ws[p][u][:tq]
                a_s[p, :, (2 * u + 1) * tk:(2 * u + 2) * tk] = ws[p][u][tq:]
            carry_s[p] = carries[p]

    odd = n_iter % 2

    @pl.when(odd == 1)
    def _():
        step(jnp.int32(0))

    def two_steps(i, c):
        step(odd + 2 * i)
        step(odd + 2 * i + 1)
        return c

    lax.fori_loop(0, n_iter // 2, two_steps, 0)
    values(0)
    return jnp.concatenate([acc_s[p] for p in range(SB_PAIRS)], axis=1)


def _sb_tri():
    j = lax.broadcasted_iota(jnp.int32, (SB_TK, SB_TK), 0)
    s = lax.broadcasted_iota(jnp.int32, (SB_TK, SB_TK), 1)
    return (j >= s).astype(BF16)


def _sb_scratch(tq):
    return [pltpu.VMEM((SB_PAIRS, 2 * tq, LANES), BF16), pltpu.VMEM((SB_PAIRS, tq, SB_U * 2 * SB_TK), BF16),
            pltpu.VMEM((SB_PAIRS, tq, LANES), F32), pltpu.VMEM((SB_PAIRS, 2 * tq, SB_TK), F32)]


def _sb_block_reader(ref2d):
    def read(start, p):
        if not isinstance(start, int):
            start = pl.multiple_of(start, SB_TK)
        return ref2d[pl.ds(start, SB_TK), p * LANES:(p + 1) * LANES].astype(BF16)
    return read


def _sb_prompt_kernel(q_ref, k_ref, v_ref, g_ref, tri_ref, o_ref, qs_s, a_s, acc_s, carry_s, *, tq):
    qi = pl.program_id(1)
    base = pl.multiple_of(qi * tq, tq)
    read_k, read_v = _sb_block_reader(k_ref.at[0]), _sb_block_reader(v_ref.at[0])

    def diag_kv(d, p):
        return read_k(base + d * SB_TK, p), read_v(base + d * SB_TK, p)

    n_iter = qi * (tq // (SB_U * SB_TK))
    y = _sb_all_pairs(q_ref[0], diag_kv, tq // SB_TK, read_k, read_v, n_iter, tri_ref[...],
                      qs_s, a_s, acc_s, carry_s)
    o_ref[0] = (y * jax.nn.silu(g_ref[0])).astype(BF16)


def _sb_prompt(q, k, v, g):
    b, l, _ = q.shape
    tq = SB_U * SB_TK
    qblk = pl.BlockSpec((1, tq, SB_WIDTH), lambda bi, qi: (bi, qi, 0))
    kvblk = pl.BlockSpec((1, l, SB_WIDTH), lambda bi, qi: (bi, 0, 0))
    return pl.pallas_call(
        functools.partial(_sb_prompt_kernel, tq=tq),
        grid=(b, l // tq),
        in_specs=[qblk, kvblk, kvblk, qblk, _full((SB_TK, SB_TK))],
        out_specs=qblk,
        out_shape=jax.ShapeDtypeStruct((b, l, SB_WIDTH), BF16),
        scratch_shapes=_sb_scratch(tq),
        compiler_params=_params(("parallel", "arbitrary")),
        name="sb_prompt",
    )(q, k, v, g, _sb_tri())


def _sb_sample_kernel(q_ref, k_ref, v_ref, ck_ref, cv_ref, g_ref, tri_ref, o_ref, qs_s, a_s, acc_s, carry_s,
                      *, n_iter):
    tq = q_ref.shape[1]

    def diag_kv(d, p):
        ls = slice(p * LANES, (p + 1) * LANES)
        pad = jnp.zeros((SB_TK - tq, LANES), BF16)
        return (jnp.concatenate([k_ref[0, :, ls].astype(BF16), pad], axis=0),
                jnp.concatenate([v_ref[0, :, ls].astype(BF16), pad], axis=0))

    y = _sb_all_pairs(q_ref[0], diag_kv, 1, _sb_block_reader(ck_ref.at[0, 0]), _sb_block_reader(cv_ref.at[0, 0]),
                      n_iter, tri_ref[...], qs_s, a_s, acc_s, carry_s)
    o_ref[0] = (y * jax.nn.silu(g_ref[0])).astype(BF16)


def _sb_sample(q, k, v, ck, cv, layer, g):
    b, l, _ = q.shape
    past = ck.shape[2]
    qblk = pl.BlockSpec((1, l, SB_WIDTH), lambda bi: (bi, 0, 0))
    cblk = pl.BlockSpec((1, 1, past, SB_WIDTH), lambda bi: (layer, bi, 0, 0))
    return pl.pallas_call(
        functools.partial(_sb_sample_kernel, n_iter=past // (SB_U * SB_TK)),
        grid=(b,),
        in_specs=[qblk, qblk, qblk, cblk, cblk, qblk, _full((SB_TK, SB_TK))],
        out_specs=qblk,
        out_shape=jax.ShapeDtypeStruct((b, l, SB_WIDTH), BF16),
        scratch_shapes=_sb_scratch(l),
        compiler_params=_params(("parallel",)),
        name="sb_sample",
    )(q, k, v, ck, cv, g, _sb_tri())


def _s5_kernel(u_ref, g_ref, h0r_ref, h0i_ref, ar_ref, ai_ref, b_ref, c_ref, d_ref,
               wg_ref, bg_ref, o_ref, hr_o, hi_o, bu_s, h_s, ut_s, gt_s, ot_s, *, tt):
    ti = pl.program_id(1)
    rows = tt * SUBLANES
    slabs = SSM_WIDTH // LANES

    @pl.when(ti == 0)
    def _():
        hr_o[...] = h0r_ref[...]
        hi_o[...] = h0i_ref[...]

    def time_major(ref, slab_s):
        for s in range(SUBLANES):
            for j in range(slabs):
                slab_s[j, pl.ds(s, tt, stride=SUBLANES), :] = ref[s, :, j * LANES:(j + 1) * LANES]
        return jnp.concatenate([slab_s[j] for j in range(slabs)], axis=1)

    u = time_major(u_ref, ut_s)
    bu_s[...] = _dot(u.astype(BF16), b_ref[...])

    ar = jnp.broadcast_to(ar_ref[...], (SUBLANES, SSM_FLAT))
    ai = jnp.broadcast_to(ai_ref[...], (SUBLANES, SSM_FLAT))

    def step(t, carry):
        hr, hi = carry
        r0 = pl.multiple_of(t * SUBLANES, SUBLANES)
        br = bu_s[pl.ds(r0, SUBLANES), 0:SSM_FLAT]
        bi = bu_s[pl.ds(r0, SUBLANES), SSM_FLAT:2 * SSM_FLAT]
        nr = ar * hr - ai * hi + br
        ni = ar * hi + ai * hr + bi
        h_s[pl.ds(r0, SUBLANES), 0:SSM_FLAT] = nr
        h_s[pl.ds(r0, SUBLANES), SSM_FLAT:2 * SSM_FLAT] = ni
        return nr, ni

    hr, hi = lax.fori_loop(0, tt, step, (hr_o[...], hi_o[...]))
    hr_o[...] = hr
    hi_o[...] = hi

    y = _dot(h_s[...].astype(BF16), c_ref[...]) + d_ref[...] * u
    y = jax.nn.gelu(y)
    z = _dot(y.astype(BF16), wg_ref[...]) + bg_ref[...]
    out = z[:, :SSM_WIDTH] * jax.nn.sigmoid(z[:, SSM_WIDTH:])
    out = out * jax.nn.silu(time_major(g_ref, gt_s))
    for j in range(slabs):
        ot_s[j] = out[:, j * LANES:(j + 1) * LANES]
    for s in range(SUBLANES):
        for j in range(slabs):
            o_ref[s, :, j * LANES:(j + 1) * LANES] = ot_s[j, pl.ds(s, tt, stride=SUBLANES), :].astype(BF16)


def _s5(u, g, h0r, h0i, sp):
    b, l, _ = u.shape
    tt = min(128, l)
    rows = tt * SUBLANES
    ublk = pl.BlockSpec((SUBLANES, tt, SSM_WIDTH), lambda bi, ti: (bi, ti, 0))
    sblk = pl.BlockSpec((SUBLANES, SSM_FLAT), lambda bi, ti: (bi, 0))
    slab = pltpu.VMEM((SSM_WIDTH // LANES, rows, LANES), F32)
    return pl.pallas_call(
        functools.partial(_s5_kernel, tt=tt),
        grid=(b // SUBLANES, l // tt),
        in_specs=[ublk, ublk, sblk, sblk,
                  _full((1, SSM_FLAT)), _full((1, SSM_FLAT)),
                  _full((SSM_WIDTH, 2 * SSM_FLAT)),
                  _full((2 * SSM_FLAT, SSM_WIDTH)), _full((1, SSM_WIDTH)),
                  _full((SSM_WIDTH, 2 * SSM_WIDTH)), _full((1, 2 * SSM_WIDTH))],
        out_specs=[ublk, sblk, sblk],
        out_shape=[jax.ShapeDtypeStruct((b, l, SSM_WIDTH), BF16),
                   jax.ShapeDtypeStruct((b, SSM_FLAT), F32),
                   jax.ShapeDtypeStruct((b, SSM_FLAT), F32)],
        scratch_shapes=[pltpu.VMEM((rows, 2 * SSM_FLAT), F32), pltpu.VMEM((rows, 2 * SSM_FLAT), F32),
                        slab, slab, slab],
        compiler_params=_params(("parallel", "arbitrary")),
        name="s5",
    )(u, g, h0r, h0i, sp["ar"], sp["ai"], sp["b"], sp["c"], sp["d"], sp["w_glu"], sp["b_glu"])


def _s5_params(lam_re, lam_im, log_dt, b_re, b_im, c_re, c_im, d, w_glu, b_glu):
    dt = jnp.exp(log_dt.astype(F32))[:, None]
    lr = jnp.minimum(lam_re.astype(F32), -1e-4)
    li = lam_im.astype(F32)
    er = jnp.exp(lr * dt)
    ar, ai = er * jnp.cos(li * dt), er * jnp.sin(li * dt)
    den = lr * lr + li * li
    fr = ((ar - 1.0) * lr + ai * li) / den
    fi = (ai * lr - (ar - 1.0) * li) / den
    br, bi = b_re.astype(F32), b_im.astype(F32)
    bbr = fr[..., None] * br - fi[..., None] * bi
    bbi = fr[..., None] * bi + fi[..., None] * br
    eye = jnp.eye(SSM_GROUPS, dtype=F32)
    bd_in = lambda m: jnp.einsum("gpc,gh->gchp", m, eye).reshape(SSM_WIDTH, SSM_FLAT)
    bd_out = lambda m: jnp.einsum("gcp,gh->gphc", m, eye).reshape(SSM_FLAT, SSM_WIDTH)
    b_all = jnp.concatenate([bd_in(bbr), bd_in(bbi)], axis=1)
    c_all = jnp.concatenate([bd_out(c_re.astype(F32)), -bd_out(c_im.astype(F32))], axis=0)
    return dict(ar=ar.reshape(1, SSM_FLAT), ai=ai.reshape(1, SSM_FLAT), b=b_all.astype(BF16),
                c=c_all.astype(BF16), d=d.astype(F32).reshape(1, SSM_WIDTH),
                w_glu=w_glu.astype(BF16), b_glu=b_glu.astype(F32).reshape(1, 2 * SSM_WIDTH))


RWKV_CK = 64
RWKV_NB = 8


def _rwkv_kernel(cp_ref, g_ref, prev_ref, s0_ref, mu_ref, w0_ref, w2_ref, a0_ref, a2_ref, kk_ref, ka_ref,
                 ub_ref, lnw_ref, lnb_ref, ho_ref, tri_ref, ones_ref, o_ref, s_o, prev_s,
                 khw_s, rw_s, ktl_s, btl_s, v_s, kht_s, bht_s, wch_s, bonus_s, y_s, *, tt, ck):
    ti = pl.program_id(1)

    @pl.when(ti == 0)
    def _():
        s_o[...] = s0_ref[...]
        prev_s[...] = jnp.broadcast_to(prev_ref[...], prev_s.shape)

    w = RWKV_WIDTH
    nb_all = cp_ref.shape[0]
    head_ones, chunk_tri, chunk_ones = ho_ref[...], tri_ref[...], ones_ref[...]

    def prep(nb, carry):
        cp = cp_ref[nb]
        trow = lax.broadcasted_iota(jnp.int32, (tt, 1), 0)
        shifted = jnp.where(trow == 0, prev_s[nb, 0:1, :], pltpu.roll(cp, 1, 0))
        prev_s[nb] = jnp.broadcast_to(cp[tt - 1:tt, :], (SUBLANES, CP_PAD))
        xc = cp + mu_ref[...] * (shifted - cp)
        r, k, v, lora = xc[:, 0:w], xc[:, w:2 * w], xc[:, 2 * w:3 * w], xc[:, 3 * w:]
        logw = -DECAY_SCALE * jax.nn.sigmoid(w0_ref[...] + _dot(jnp.tanh(lora).astype(BF16), w2_ref[...]))
        a = jax.nn.sigmoid(a0_ref[...] + _dot(lora.astype(BF16), a2_ref[...]))
        cum = _dot2r(chunk_tri, logw)
        tot = _dot2r(chunk_ones, logw)
        kkv = k * kk_ref[...]
        kh = kkv * lax.rsqrt(_dot2l(kkv * kkv, head_ones) + 1e-12)
        kt = k * (1.0 + (a - 1.0) * ka_ref[...])
        bvec = a * kh
        w_inv = jnp.exp(-cum)
        w_end = jnp.exp(tot - cum)
        khw_s[nb] = (kh * jnp.exp(cum - logw)).astype(BF16)
        rw_s[nb] = (r * jnp.exp(cum)).astype(BF16)
        ktl_s[nb] = (kt * w_inv).astype(BF16)
        btl_s[nb] = (bvec * w_inv).astype(BF16)
        v_s[nb] = v.astype(BF16)
        kht_s[nb] = (kt * w_end).astype(BF16)
        bht_s[nb] = (bvec * w_end).astype(BF16)
        wch_s[nb] = jnp.exp(tot)
        bonus_s[nb] = _dot2l(r * kt * ub_ref[...], head_ones) * v
        return carry

    lax.fori_loop(0, nb_all, prep, 0)

    lane = lax.broadcasted_iota(jnp.int32, (1, LANES), 1)
    m_lo = lane < HEAD_DIM
    ri = lax.broadcasted_iota(jnp.int32, (2 * ck, 2 * ck), 0)
    cj = lax.broadcasted_iota(jnp.int32, (2 * ck, 2 * ck), 1)
    same_head = (ri >= ck) == (cj >= ck)
    strict = same_head & (cj < ri)
    lower = same_head & (cj <= ri)
    eye = (ri == cj).astype(F32)
    rr = lax.broadcasted_iota(jnp.int32, (2 * ck, LANES), 0)
    rl = lax.broadcasted_iota(jnp.int32, (2 * ck, LANES), 1)
    own_lanes = (rr >= ck) == (rl >= HEAD_DIM)
    sr = lax.broadcasted_iota(jnp.int32, (LANES, LANES), 0) // HEAD_DIM
    sc = lax.broadcasted_iota(jnp.int32, (LANES, LANES), 1) // HEAD_DIM
    blockdiag = sr == sc
    dot1 = lambda x, y: _dot(x.astype(BF16), y.astype(BF16))

    cat = lambda xs: jnp.concatenate(xs, axis=0)

    def chunk_step(c, carry):
        r0 = pl.multiple_of(c * ck, ck)
        rows = pl.ds(r0, ck)
        chains = [(nb, hp) for nb in range(nb_all) for hp in range(w // LANES)]
        lanes = lambda hp: slice(hp * LANES, (hp + 1) * LANES)
        rd = lambda ref: [ref[nb, rows, lanes(hp)] for nb, hp in chains]
        khw, rw, ktl, btl, v = rd(khw_s), rd(rw_s), rd(ktl_s), rd(btl_s), rd(v_s)
        z = jnp.zeros_like(khw[0])
        l4 = [cat([jnp.where(m_lo, a, z), jnp.where(m_lo, z, a), jnp.where(m_lo, b, z), jnp.where(m_lo, z, b)])
              for a, b in zip(khw, rw)]
        scb = [_dot(a, cat([b, b]), NT) for a, b in zip(l4, btl)]
        sck = [_dot(a, cat([b, b]), NT) for a, b in zip(l4, ktl)]
        nmat = [jnp.where(strict, a[:2 * ck], 0.0) for a in scb]
        arb = [jnp.where(lower, a[2 * ck:], 0.0) for a in scb]
        akk = [jnp.where(strict, a[:2 * ck], 0.0) for a in sck]
        ark = [jnp.where(lower, a[2 * ck:], 0.0) for a in sck]
        inv = [eye - a for a in nmat]
        pw = nmat
        steps = 1
        while 2 * steps < ck:
            pw = [dot1(a, a) for a in pw]
            inv = [dot1(a, eye + b) for a, b in zip(inv, pw)]
            steps *= 2
        s_pair = [s_o[nb, hp] for nb, hp in chains]
        ks = [_dot(cat([a, b]), s.astype(BF16), NT) for a, b, s in zip(khw, rw, s_pair)]
        av = [dot1(cat([a, b]), cat([x, x])) for a, b, x in zip(akk, ark, v)]
        sa2 = [jnp.where(own_lanes, dot1(t, cat([k[:ck], k[:ck]]) + a[:2 * ck]), 0.0)
               for t, k, a in zip(inv, ks, av)]
        y2 = [jnp.where(own_lanes, a[2 * ck:] - dot1(b, s), 0.0) for a, b, s in zip(av, arb, sa2)]
        for (nb, hp), k, y in zip(chains, ks, y2):
            y_s[nb, rows, lanes(hp)] = k[ck:] + y[:ck] + y[ck:]
        kht, bht = rd(kht_s), rd(bht_s)
        upd = [_dot(cat([x, (-(s[:ck] + s[ck:])).astype(BF16)]), cat([a, b]), TN)
               for x, s, a, b in zip(v, sa2, kht, bht)]
        for (nb, hp), s, u in zip(chains, s_pair, upd):
            s_o[nb, hp] = s * wch_s[nb, pl.ds(r0, 1), lanes(hp)] + jnp.where(blockdiag, u, 0.0)
        return carry

    lax.fori_loop(0, tt // ck, chunk_step, 0)

    fin = 2 if nb_all % 2 == 0 else 1

    def finish(i, carry):
        nbs = [i * fin + j for j in range(fin)]
        ys = [y_s[nb] for nb in nbs]
        mean = [_dot2l(y, head_ones) * (1.0 / HEAD_DIM) for y in ys]
        cen = [y - m for y, m in zip(ys, mean)]
        var = [_dot2l(c * c, head_ones) * (1.0 / HEAD_DIM) for c in cen]
        for nb, c, v in zip(nbs, cen, var):
            y = c * lax.rsqrt(v + GN_EPS) * lnw_ref[...] + lnb_ref[...] + bonus_s[nb]
            o_ref[nb] = (y * jax.nn.silu(g_ref[nb])).astype(BF16)
        return carry

    lax.fori_loop(0, nb_all // fin, finish, 0)


def _rwkv(cp, g, prev, s0p, rp):
    b, l, _ = cp.shape
    tt = min(256, l)
    ck = min(RWKV_CK, tt)
    nb = min(RWKV_NB, b)
    tblk = lambda w: pl.BlockSpec((nb, tt, w), lambda bi, ti: (bi, ti, 0))
    sblk = pl.BlockSpec((nb, 2, LANES, LANES), lambda bi, ti: (bi, 0, 0, 0))
    vec = _full((1, RWKV_WIDTH))
    hl = lax.broadcasted_iota(jnp.int32, (RWKV_WIDTH, RWKV_WIDTH), 0) // HEAD_DIM
    hc = lax.broadcasted_iota(jnp.int32, (RWKV_WIDTH, RWKV_WIDTH), 1) // HEAD_DIM
    tr = lax.broadcasted_iota(jnp.int32, (tt, tt), 0)
    tc = lax.broadcasted_iota(jnp.int32, (tt, tt), 1)
    same_chunk = (tr // ck) == (tc // ck)
    slab = lambda dt: pltpu.VMEM((nb, tt, RWKV_WIDTH), dt)
    return pl.pallas_call(
        functools.partial(_rwkv_kernel, tt=tt, ck=ck),
        grid=(b // nb, l // tt),
        in_specs=[tblk(CP_PAD), tblk(RWKV_WIDTH),
                  pl.BlockSpec((nb, 1, CP_PAD), lambda bi, ti: (bi, 0, 0)), sblk,
                  _full((1, CP_PAD)), vec, _full((LANES, RWKV_WIDTH)), vec, _full((LANES, RWKV_WIDTH)),
                  vec, vec, vec, vec, vec,
                  _full((RWKV_WIDTH, RWKV_WIDTH)), _full((tt, tt)), _full((tt, tt))],
        out_specs=[tblk(RWKV_WIDTH), sblk],
        out_shape=[jax.ShapeDtypeStruct((b, l, RWKV_WIDTH), BF16),
                   jax.ShapeDtypeStruct((b, 2, LANES, LANES), F32)],
        scratch_shapes=[pltpu.VMEM((nb, SUBLANES, CP_PAD), F32)] + [slab(BF16)] * 7 + [slab(F32)] * 3,
        compiler_params=_params(("parallel", "arbitrary")),
        name="rwkv",
    )(cp, g, prev, s0p, rp["mu"], rp["w0"], rp["w2"], rp["a0"], rp["a2"], rp["k_k"], rp["k_a"],
      rp["u"], rp["ln_w"], rp["ln_b"],
      (hl == hc).astype(BF16), (same_chunk & (tc <= tr)).astype(BF16), same_chunk.astype(BF16))


def _rwkv_params(mu, w0, w2, a0, a2, k_k, k_a, u, ln_w, ln_b):
    vec = lambda t: t.astype(F32).reshape(1, RWKV_WIDTH)
    mu_p = jnp.zeros((1, CP_PAD), F32).at[0, :RWKV_SHIFT].set(mu.astype(F32))
    w2_p = jnp.zeros((LANES, RWKV_WIDTH), F32).at[:RWKV_LORA].set(w2.astype(F32)).astype(BF16)
    a2_p = jnp.zeros((LANES, RWKV_WIDTH), F32).at[RWKV_LORA:2 * RWKV_LORA].set(a2.astype(F32)).astype(BF16)
    return dict(mu=mu_p, w0=vec(w0), w2=w2_p, a0=vec(a0), a2=a2_p, k_k=vec(k_k), k_a=vec(k_a),
                u=vec(u), ln_w=vec(ln_w), ln_b=vec(ln_b))


def _pair_state(s):
    b = s.shape[0]
    s = s.astype(F32).reshape(b, 2, 2, HEAD_DIM, HEAD_DIM)
    eye = jnp.eye(2, dtype=F32)
    return jnp.einsum("bphvk,hg->bphvgk", s, eye).reshape(b, 2, LANES, LANES)


def _unpair_state(sp):
    b = sp.shape[0]
    s = sp.reshape(b, 2, 2, HEAD_DIM, 2, HEAD_DIM)
    return jnp.stack([s[:, :, 0, :, 0, :], s[:, :, 1, :, 1, :]], axis=2).reshape(b, RWKV_HEADS, HEAD_DIM, HEAD_DIM)


def _outproj_kernel(x_ref, a_ref, b_ref, c_ref, w_ref, o_ref):
    o_ref[...] = x_ref[...] + _mix_out(a_ref, b_ref, c_ref, w_ref)


def _outproj(x2d, ma, mb, mc, w_out):
    n = x2d.shape[0]
    tm = min(512, n)
    row = lambda w: pl.BlockSpec((tm, w), lambda i: (i, 0))
    return pl.pallas_call(
        _outproj_kernel,
        grid=(n // tm,),
        in_specs=[row(D_MODEL), row(SB_WIDTH), row(SSM_WIDTH), row(RWKV_WIDTH), _full((D_MODEL, D_MODEL))],
        out_specs=row(D_MODEL),
        out_shape=jax.ShapeDtypeStruct((n, D_MODEL), F32),
        compiler_params=_params(("parallel",)),
        name="outproj",
    )(x2d, ma, mb, mc, w_out)


def _pad_w_in(w_in):
    w = w_in.astype(F32)
    pad = jnp.zeros((D_MODEL, CP_PAD - RWKV_SHIFT), F32)
    return jnp.concatenate([w[:, :COL_CP + RWKV_SHIFT], pad, w[:, COL_CP + RWKV_SHIFT:]], axis=1).astype(BF16)


def _layer(x2d, b, l, lp, past, prev_out, kv_all=None, layer=0, depth=1):
    n = b * l
    outs = _inproj(x2d, l, lp["norm_w"], lp["w_in"], lp["q_w"], lp["k_w"], prev_out, kv_all, layer, depth)
    if prev_out is not None:
        x2d, outs = outs[0], outs[1:]
    q, k, v, ga, u, gb, cp, gc, kt, vt = outs
    r3 = lambda t: t.reshape(b, l, t.shape[-1])
    q, k, v, ga, u, gb, cp, gc = map(r3, (q, k, v, ga, u, gb, cp, gc))

    if past is None:
        mix_a = _sb_prompt(q, k, v, ga)
        h0r = h0i = jnp.zeros((b, SSM_FLAT), F32)
        prev = jnp.zeros((b, 1, CP_PAD), F32)
        s0p = jnp.zeros((b, 2, LANES, LANES), F32)
    else:
        mix_a = _sb_sample(q, k, v, past["k"], past["v"], past["layer"], ga)
        h0r = past["ssm_re"].astype(F32).reshape(b, SSM_FLAT)
        h0i = past["ssm_im"].astype(F32).reshape(b, SSM_FLAT)
        prev = jnp.pad(past["shift"].astype(F32), ((0, 0), (0, CP_PAD - RWKV_SHIFT)))[:, None, :]
        s0p = _pair_state(past["wkv"])

    mix_b, hr, hi = _s5(u, gb, h0r, h0i, lp["s5"])
    mix_c, s_pair = _rwkv(cp, gc, prev, s0p, lp["rwkv"])

    pending = (mix_a.reshape(n, SB_WIDTH), mix_b.reshape(n, SSM_WIDTH), mix_c.reshape(n, RWKV_WIDTH), lp["w_out"])
    state = (kt, vt, hr.reshape(b, SSM_GROUPS, SSM_STATE), hi.reshape(b, SSM_GROUPS, SSM_STATE),
             _unpair_state(s_pair), cp[:, -1, :RWKV_SHIFT])
    return x2d, pending, state


def kernel(x_prompt, x_sample, cache_k, cache_v, state_ssm_re, state_ssm_im, state_wkv, state_shift, norm_w, w_in, q_norm_w, k_norm_w, ssm_lambda_re, ssm_lambda_im, ssm_log_dt, ssm_b_re, ssm_b_im, ssm_c_re, ssm_c_im, ssm_d, ssm_w_glu, ssm_b_glu, rwkv_mu, rwkv_w0, rwkv_w2, rwkv_a0, rwkv_a2, rwkv_k_k, rwkv_k_a, rwkv_u, rwkv_ln_w, rwkv_ln_b, w_out):
    depth = w_in.shape[0]
    (bp, lp_len, _), (bs, ls_len, _) = x_prompt.shape, x_sample.shape
    xp, xs = x_prompt.reshape(bp * lp_len, D_MODEL), x_sample.reshape(bs * ls_len, D_MODEL)
    out_p = out_s = None
    ck_all = cache_k.reshape(cache_k.shape[:3] + (SB_WIDTH,))
    cv_all = cache_v.reshape(cache_v.shape[:3] + (SB_WIDTH,))
    p_states, s_states = [], []
    for i in range(depth):
        tile8 = lambda t: jnp.tile(t.astype(F32), SB_WIDTH // HEAD_DIM).reshape(1, SB_WIDTH)
        lp = dict(
            norm_w=norm_w[i].astype(F32).reshape(1, D_MODEL), w_in=_pad_w_in(w_in[i]),
            q_w=tile8(q_norm_w[i]), k_w=tile8(k_norm_w[i]),
            s5=_s5_params(ssm_lambda_re[i], ssm_lambda_im[i], ssm_log_dt[i], ssm_b_re[i], ssm_b_im[i],
                          ssm_c_re[i], ssm_c_im[i], ssm_d[i], ssm_w_glu[i], ssm_b_glu[i]),
            rwkv=_rwkv_params(rwkv_mu[i], rwkv_w0[i], rwkv_w2[i], rwkv_a0[i], rwkv_a2[i], rwkv_k_k[i],
                              rwkv_k_a[i], rwkv_u[i].reshape(-1), rwkv_ln_w[i], rwkv_ln_b[i]),
            w_out=w_out[i].astype(BF16))
        kv_p = None if i == 0 else p_states[-1][:2]
        kv_s = None if i == 0 else s_states[-1][:2]
        xp, out_p, st_p = _layer(xp, bp, lp_len, lp, None, out_p, kv_p, i, depth)
        past = dict(k=ck_all, v=cv_all, layer=i,
                    ssm_re=state_ssm_re[i], ssm_im=state_ssm_im[i], wkv=state_wkv[i], shift=state_shift[i])
        xs, out_s, st_s = _layer(xs, bs, ls_len, lp, past, out_s, kv_s, i, depth)
        p_states.append(st_p)
        s_states.append(st_s)
    xp = _outproj(xp, *out_p).reshape(x_prompt.shape)
    xs = _outproj(xs, *out_s).reshape(x_sample.shape)
    def gather(states):
        kv = tuple(jnp.transpose(t.reshape(t.shape[:2] + (SB_WIDTH // HEAD_DIM, HEAD_DIM, t.shape[-1])),
                                 (0, 1, 4, 2, 3)) for t in states[-1][:2])
        return kv + tuple(jnp.stack([s[j] for s in states], axis=0) for j in range(2, 6))

    return (xp, xs) + gather(p_states) + gather(s_states)
```
